```python
import math
import jax
import jax.numpy as jnp
from jax import lax
import numpy as np

D_MODEL = 1024
BATCH = 4
SEQ = 8192
DEPTH = 2

HEAD_DIM = 64
SB_HEADS = 4
MOBA_HEADS = 4
NSA_HEADS = 8
NSA_KV_GROUPS = 2
NSA_HPG = NSA_HEADS // NSA_KV_GROUPS
N_BRANCHES = 3
SB_WIDTH = SB_HEADS * HEAD_DIM
MOBA_WIDTH = MOBA_HEADS * HEAD_DIM
NSA_WIDTH = NSA_HEADS * HEAD_DIM
NSA_KV_WIDTH = NSA_KV_GROUPS * HEAD_DIM
MIX_WIDTH = SB_WIDTH + MOBA_WIDTH + NSA_WIDTH
N_IN = 3 * SB_WIDTH + 3 * MOBA_WIDTH + NSA_WIDTH + 6 * NSA_KV_WIDTH + N_BRANCHES * NSA_HEADS + N_BRANCHES * D_MODEL

SB_Q_BLOCK = 128
SPARSE_Q_BLOCK = 64
MOBA_BLOCK = 256
MOBA_TOPK = 3
CMP_BLOCK = 32
CMP_STRIDE = 16
CMP_HIDDEN = 4 * HEAD_DIM
SLC_BLOCK = 64
SLC_TOPN = 16
WINDOW = 512
N_BUCKETS = 32
REL_MAX_DISTANCE = 128
N_BIAS_HEADS = MOBA_HEADS + NSA_HEADS
D_FF = -(-8 * D_MODEL // (3 * 256)) * 256
NORM_EPS = 1e-6
NEG = -1e30
BIG = 1e30
TINY = 1e-30

kernel_name = 'hybrid_stickbreak_moba_nsa_block'


def rms_norm(x, g):
    xf = x.astype(jnp.float32)
    y = xf * lax.rsqrt(jnp.mean(xf * xf, axis=-1, keepdims=True) + NORM_EPS)
    return (y * g.astype(jnp.float32)).astype(x.dtype)


def masked_softmax(logits, mask):
    l = jnp.where(mask, logits.astype(jnp.float32), NEG)
    p = jnp.where(mask, jnp.exp(l - jnp.max(l, axis=-1, keepdims=True)), 0.0)
    return p / jnp.maximum(jnp.sum(p, axis=-1, keepdims=True), TINY)


def t5_bucket(dist):
    n = jnp.maximum(dist, 0)
    max_exact = N_BUCKETS // 2
    nf = jnp.maximum(n, 1).astype(jnp.float32)
    large = max_exact + (jnp.log(nf / max_exact) / math.log(REL_MAX_DISTANCE / max_exact) * (N_BUCKETS - max_exact)).astype(jnp.int32)
    large = jnp.minimum(large, N_BUCKETS - 1)
    return jnp.where(n < max_exact, n, large)


def to_heads(t, n_heads):
    b, s, _ = t.shape
    return t.reshape(b, s, n_heads, HEAD_DIM).transpose(0, 2, 1, 3)


def from_heads(t):
    b, h, s, d = t.shape
    return t.transpose(0, 2, 1, 3).reshape(b, s, h * d)


def stick_breaking_attention(q, k, v):
    s_len = q.shape[2]
    scale = HEAD_DIM ** -0.5
    outs = []
    for blk in range(s_len // SB_Q_BLOCK):
        q0 = blk * SB_Q_BLOCK
        kl = q0 + SB_Q_BLOCK
        z = jnp.einsum('bhqd,bhkd->bhqk', q[:, :, q0:kl], k[:, :, :kl]).astype(jnp.float32) * scale
        causal = jnp.arange(kl)[None, :] < (q0 + jnp.arange(SB_Q_BLOCK))[:, None]
        log_stay = jnp.where(causal, jax.nn.log_sigmoid(-z), 0.0)
        log_after = lax.cumsum(log_stay, axis=3, reverse=True) - log_stay
        a = jnp.where(causal, jnp.exp(jax.nn.log_sigmoid(z) + log_after), 0.0)
        outs.append(jnp.einsum('bhqk,bhkd->bhqd', a.astype(v.dtype), v[:, :, :kl]))
    return jnp.concatenate(outs, axis=2)


def moba_attention(q, k, v, tbl):
    b, h, s_len, d = q.shape
    qb = SPARSE_Q_BLOCK
    nblk = -(-s_len // MOBA_BLOCK)
    pad = nblk * MOBA_BLOCK - s_len
    kp = jnp.pad(k, ((0, 0), (0, 0), (0, pad), (0, 0)))
    vp = jnp.pad(v, ((0, 0), (0, 0), (0, pad), (0, 0)))
    kb = kp.reshape(b, h, nblk, MOBA_BLOCK, d)
    vb = vp.reshape(b, h, nblk, MOBA_BLOCK, d)
    kmean = jnp.mean(kb.astype(jnp.float32), axis=3).astype(k.dtype)
    topk = min(MOBA_TOPK, nblk)
    scale = HEAD_DIM ** -0.5
    bi = jnp.arange(b)[:, None, None, None]
    hi = jnp.arange(h)[None, :, None, None]
    hb = jnp.arange(h)[None, :, None, None, None]
    in_blk = jnp.arange(MOBA_BLOCK)
    blk_ids = jnp.arange(nblk)

    def chunk(c):
        q0 = c * qb
        tpos = q0 + jnp.arange(qb)
        cur = q0 // MOBA_BLOCK
        qc = lax.dynamic_slice_in_dim(q, q0, qb, axis=2)
        score = jnp.einsum('bhqd,bhnd->bhqn', qc, kmean).astype(jnp.float32)
        score = jnp.where(blk_ids < cur, score, NEG)
        _, idx = lax.top_k(score, topk)
        k_sel = kb[bi, hi, idx]
        v_sel = vb[bi, hi, idx]
        pos_sel = idx[..., None] * MOBA_BLOCK + in_blk
        l_sel = jnp.einsum('bhqd,bhqnkd->bhqnk', qc, k_sel).astype(jnp.float32) * scale
        l_sel = l_sel + tbl[hb, t5_bucket(tpos[:, None, None] - pos_sel)]
        own0 = cur * MOBA_BLOCK
        k_own = lax.dynamic_slice_in_dim(kp, own0, MOBA_BLOCK, axis=2)
        v_own = lax.dynamic_slice_in_dim(vp, own0, MOBA_BLOCK, axis=2)
        dist_own = tpos[:, None] - (own0 + in_blk)[None, :]
        l_own = jnp.einsum('bhqd,bhkd->bhqk', qc, k_own).astype(jnp.float32) * scale + tbl[:, t5_bucket(dist_own)]
        m_sel = jnp.broadcast_to((jnp.arange(topk) < cur)[:, None], (topk, MOBA_BLOCK)).reshape(topk * MOBA_BLOCK)
        mask = jnp.concatenate([jnp.broadcast_to(m_sel, (qb, topk * MOBA_BLOCK)), dist_own >= 0], axis=-1)
        logits = jnp.concatenate([l_sel.reshape(b, h, qb, topk * MOBA_BLOCK), l_own], axis=-1)
        p = masked_softmax(logits, mask)
        p_sel = p[..., :topk * MOBA_BLOCK].reshape(b, h, qb, topk, MOBA_BLOCK).astype(v.dtype)
        p_own = p[..., topk * MOBA_BLOCK:].astype(v.dtype)
        return jnp.einsum('bhqnk,bhqnkd->bhqd', p_sel, v_sel) + jnp.einsum('bhqk,bhkd->bhqd', p_own, v_own)

    o = lax.map(chunk, jnp.arange(s_len // qb))
    return o.transpose(1, 2, 0, 3, 4).reshape(b, h, s_len, d)


def nsa_compress(t, pos, w1, w2):
    b, g, s_len, d = t.shape
    n_cmp = (s_len - CMP_BLOCK) // CMP_STRIDE + 1
    idx = np.arange(n_cmp)[:, None] * CMP_STRIDE + np.arange(CMP_BLOCK)[None, :]
    blocks = (t[:, :, idx] + pos).reshape(b, g, n_cmp, CMP_BLOCK * d)
    return jax.nn.gelu(blocks @ w1) @ w2


def nsa_attention(q, k_cmp, v_cmp, k_slc, v_slc, k_win, v_win, gates, k_cmp_norm, cmp_pos, cmp_w1, cmp_w2, tbl):
    b, g, hpg, s_len, d = q.shape
    qb = SPARSE_Q_BLOCK
    scale = HEAD_DIM ** -0.5
    n_cmp = (s_len - CMP_BLOCK) // CMP_STRIDE + 1
    cmp_start = np.arange(n_cmp) * CMP_STRIDE
    cmp_end = jnp.asarray(cmp_start + CMP_BLOCK - 1)
    kc = rms_norm(nsa_compress(k_cmp, cmp_pos[0], cmp_w1[0], cmp_w2[0]), k_cmp_norm)
    vc = nsa_compress(v_cmp, cmp_pos[1], cmp_w1[1], cmp_w2[1])
    n_slc = s_len // SLC_BLOCK
    slc_start = np.arange(n_slc) * SLC_BLOCK
    overlap = jnp.asarray(((cmp_start[:, None] < slc_start[None, :] + SLC_BLOCK)
                           & (cmp_start[:, None] + CMP_BLOCK > slc_start[None, :])).astype(np.float32))
    n_sel = min(SLC_TOPN, n_slc)
    ks_blk = k_slc.reshape(b, g, n_slc, SLC_BLOCK, d)
    vs_blk = v_slc.reshape(b, g, n_slc, SLC_BLOCK, d)
    kw_pad = jnp.pad(k_win, ((0, 0), (0, 0), (WINDOW, 0), (0, 0)))
    vw_pad = jnp.pad(v_win, ((0, 0), (0, 0), (WINDOW, 0), (0, 0)))
    bi = jnp.arange(b)[:, None, None, None]
    gi = jnp.arange(g)[None, :, None, None]
    gb = jnp.arange(g)[None, :, None, None, None, None]
    hb = jnp.arange(hpg)[None, None, :, None, None, None]
    blk_ids = jnp.arange(n_slc)
    in_blk = jnp.arange(SLC_BLOCK)
    win_off = jnp.arange(qb + WINDOW)

    def chunk(c):
        q0 = c * qb
        tpos = q0 + jnp.arange(qb)
        qc = lax.dynamic_slice_in_dim(q, q0, qb, axis=3)
        dist_c = tpos[:, None] - cmp_end[None, :]
        l_c = jnp.einsum('bghqd,bgnd->bghqn', qc, kc).astype(jnp.float32) * scale + tbl[:, :, t5_bucket(dist_c)]
        p_c = masked_softmax(l_c, dist_c >= 0)
        o_c = jnp.einsum('bghqn,bgnd->bghqd', p_c.astype(vc.dtype), vc)
        imp = jnp.einsum('bghqn,nm->bgqm', p_c, overlap)
        cur = tpos // SLC_BLOCK
        forced = (blk_ids == 0) | (blk_ids == cur[:, None]) | (blk_ids == cur[:, None] - 1)
        score = jnp.where(forced, BIG, imp)
        score = jnp.where(blk_ids <= cur[:, None], score, NEG)
        _, sidx = lax.top_k(score, n_sel)
        k_sel = ks_blk[bi, gi, sidx]
        v_sel = vs_blk[bi, gi, sidx]
        dist_s = tpos[:, None, None] - (sidx[..., None] * SLC_BLOCK + in_blk)
        l_s = jnp.einsum('bghqd,bgqnkd->bghqnk', qc, k_sel).astype(jnp.float32) * scale
        l_s = l_s + tbl[gb, hb, t5_bucket(dist_s)[:, :, None]]
        p_s = masked_softmax(l_s.reshape(b, g, hpg, qb, n_sel * SLC_BLOCK),
                             (dist_s >= 0).reshape(b, g, 1, qb, n_sel * SLC_BLOCK))
        o_s = jnp.einsum('bghqnk,bgqnkd->bghqd', p_s.reshape(b, g, hpg, qb, n_sel, SLC_BLOCK).astype(v_sel.dtype), v_sel)
        kwin = lax.dynamic_slice_in_dim(kw_pad, q0, qb + WINDOW, axis=2)
        vwin = lax.dynamic_slice_in_dim(vw_pad, q0, qb + WINDOW, axis=2)
        pos_w = q0 - WINDOW + win_off
        dist_w = tpos[:, None] - pos_w[None, :]
        m_w = (dist_w >= 0) & (dist_w < WINDOW) & (pos_w >= 0)[None, :]
        l_w = jnp.einsum('bghqd,bgkd->bghqk', qc, kwin).astype(jnp.float32) * scale + tbl[:, :, t5_bucket(dist_w)]
        p_w = masked_softmax(l_w, m_w)
        o_w = jnp.einsum('bghqk,bgkd->bghqd', p_w.astype(vwin.dtype), vwin)
        return o_c, o_s, o_w

    o_c, o_s, o_w = lax.map(chunk, jnp.arange(s_len // qb))

    def unchunk(o):
        return o.transpose(1, 2, 3, 0, 4, 5).reshape(b, g, hpg, s_len, d)

    return gates[0] * unchunk(o_c) + gates[1] * unchunk(o_s) + gates[2] * unchunk(o_w)


def split_points():
    widths = [SB_WIDTH] * 3 + [MOBA_WIDTH] * 3 + [NSA_WIDTH] + [NSA_KV_WIDTH] * 6 + [N_BRANCHES * NSA_HEADS, N_BRANCHES * D_MODEL]
    return [int(p) for p in np.cumsum(widths)[:-1]]


def hybrid_layer(x, rel_bias, attn_norm, w_in, moba_q_norm, moba_k_norm, nsa_q_norm, nsa_k_norm,
                 nsa_cmp_pos, nsa_cmp_w1, nsa_cmp_w2, w_branch, w_out, ffn_norm, w_gate_up, w_down):
    b, s_len, _ = x.shape
    h = rms_norm(x, attn_norm)
    (sb_q, sb_k, sb_v, mb_q, mb_k, mb_v, ns_q, ns_kc, ns_vc, ns_ks, ns_vs, ns_kw, ns_vw,
     ns_gate, br_gate) = jnp.split(h @ w_in, split_points(), axis=-1)
    tbl = rel_bias.T
    o_a = from_heads(stick_breaking_attention(to_heads(sb_q, SB_HEADS), to_heads(sb_k, SB_HEADS), to_heads(sb_v, SB_HEADS)))
    o_b = from_heads(moba_attention(rms_norm(to_heads(mb_q, MOBA_HEADS), moba_q_norm),
                                    rms_norm(to_heads(mb_k, MOBA_HEADS), moba_k_norm),
                                    to_heads(mb_v, MOBA_HEADS), tbl[:MOBA_HEADS]))
    nq = rms_norm(ns_q.reshape(b, s_len, NSA_KV_GROUPS, NSA_HPG, HEAD_DIM).transpose(0, 2, 3, 1, 4), nsa_q_norm)

    def kv(t):
        return t.reshape(b, s_len, NSA_KV_GROUPS, HEAD_DIM).transpose(0, 2, 1, 3)

    nsa_gates = jax.nn.sigmoid(ns_gate.reshape(b, s_len, 3, NSA_KV_GROUPS, NSA_HPG)).transpose(2, 0, 3, 4, 1)[..., None]
    o_c = nsa_attention(nq, kv(ns_kc), kv(ns_vc), rms_norm(kv(ns_ks), nsa_k_norm[1]), kv(ns_vs),
                        rms_norm(kv(ns_kw), nsa_k_norm[2]), kv(ns_vw), nsa_gates, nsa_k_norm[0],
                        nsa_cmp_pos, nsa_cmp_w1, nsa_cmp_w2,
                        tbl[MOBA_HEADS:].reshape(NSA_KV_GROUPS, NSA_HPG, N_BUCKETS))
    o_c = o_c.transpose(0, 3, 1, 2, 4).reshape(b, s_len, NSA_WIDTH)
    g = jax.nn.sigmoid(br_gate.reshape(b, s_len, N_BRANCHES, D_MODEL))
    mix = (g[:, :, 0] * (o_a @ w_branch[:SB_WIDTH])
           + g[:, :, 1] * (o_b @ w_branch[SB_WIDTH:SB_WIDTH + MOBA_WIDTH])
           + g[:, :, 2] * (o_c @ w_branch[SB_WIDTH + MOBA_WIDTH:]))
    x = x + mix @ w_out
    gate, up = jnp.split(rms_norm(x, ffn_norm) @ w_gate_up, 2, axis=-1)
    return x + (jax.nn.silu(gate) * up) @ w_down


def setup_inputs(seed: int = 0) -> dict:
    key = jax.random.key(seed)
    ks = jax.random.split(key, 18)

    def nrm(k, shape, scale):
        return jax.random.normal(k, shape, jnp.float32) * scale

    w_branch = jnp.concatenate([
        nrm(ks[11], (DEPTH, SB_WIDTH, D_MODEL), SB_WIDTH ** -0.5),
        nrm(ks[16], (DEPTH, MOBA_WIDTH, D_MODEL), MOBA_WIDTH ** -0.5),
        nrm(ks[17], (DEPTH, NSA_WIDTH, D_MODEL), NSA_WIDTH ** -0.5)], axis=1)
    return {
        'x': nrm(ks[0], (BATCH, SEQ, D_MODEL), 1.0),
        'rel_bias': nrm(ks[1], (N_BUCKETS, N_BIAS_HEADS), 0.2),
        'attn_norm': 1.0 + nrm(ks[2], (DEPTH, D_MODEL), 0.02),
        'w_in': nrm(ks[3], (DEPTH, D_MODEL, N_IN), D_MODEL ** -0.5),
        'moba_q_norm': 1.0 + nrm(ks[4], (DEPTH, HEAD_DIM), 0.02),
        'moba_k_norm': 1.0 + nrm(ks[5], (DEPTH, HEAD_DIM), 0.02),
        'nsa_q_norm': 1.0 + nrm(ks[6], (DEPTH, HEAD_DIM), 0.02),
        'nsa_k_norm': 1.0 + nrm(ks[7], (DEPTH, 3, HEAD_DIM), 0.02),
        'nsa_cmp_pos': nrm(ks[8], (DEPTH, 2, CMP_BLOCK, HEAD_DIM), 0.1),
        'nsa_cmp_w1': nrm(ks[9], (DEPTH, 2, CMP_BLOCK * HEAD_DIM, CMP_HIDDEN), (CMP_BLOCK * HEAD_DIM) ** -0.5),
        'nsa_cmp_w2': nrm(ks[10], (DEPTH, 2, CMP_HIDDEN, HEAD_DIM), CMP_HIDDEN ** -0.5),
        'w_branch': w_branch,
        'w_out': nrm(ks[12], (DEPTH, D_MODEL, D_MODEL), D_MODEL ** -0.5),
        'ffn_norm': 1.0 + nrm(ks[13], (DEPTH, D_MODEL), 0.02),
        'w_gate_up': nrm(ks[14], (DEPTH, D_MODEL, 2 * D_FF), D_MODEL ** -0.5),
        'w_down': nrm(ks[15], (DEPTH, D_FF, D_MODEL), D_FF ** -0.5),
    }


def reference(x, rel_bias, attn_norm, w_in, moba_q_norm, moba_k_norm, nsa_q_norm, nsa_k_norm,
              nsa_cmp_pos, nsa_cmp_w1, nsa_cmp_w2, w_branch, w_out, ffn_norm, w_gate_up, w_down):
    for layer in range(DEPTH):
        x = hybrid_layer(x, rel_bias, attn_norm[layer], w_in[layer], moba_q_norm[layer], moba_k_norm[layer],
                         nsa_q_norm[layer], nsa_k_norm[layer], nsa_cmp_pos[layer], nsa_cmp_w1[layer],
                         nsa_cmp_w2[layer], w_branch[layer], w_out[layer], ffn_norm[layer],
                         w_gate_up[layer], w_down[layer])
    return x
```

```python
import functools
import math

import jax
import jax.numpy as jnp
from jax import lax
from jax.experimental import pallas as pl
from jax.experimental.pallas import tpu as pltpu

HEAD_DIM = 64
SB_HEADS = 4
MOBA_HEADS = 4
NSA_HEADS = 8
NSA_KV_GROUPS = 2
NSA_HPG = NSA_HEADS // NSA_KV_GROUPS
N_BRANCHES = 3
MOBA_BLOCK = 256
MOBA_TOPK = 3
CMP_BLOCK = 32
CMP_STRIDE = 16
CMP_HIDDEN = 4 * HEAD_DIM
SLC_BLOCK = 64
SLC_TOPN = 16
WINDOW = 512
N_BUCKETS = 32
REL_MAX_DISTANCE = 128
NORM_EPS = 1e-6
NEG = -1e30
BIG = 1e30
TINY = 1e-30

LANES = 128
PAIR = 2 * HEAD_DIM
VMEM_LIMIT = 56 * 1024 * 1024

F32 = jnp.float32
BF16 = jnp.bfloat16

BLK_SB_Q, BLK_SB_K, BLK_SB_V = 0, 2, 4
BLK_MB_Q, BLK_MB_K, BLK_MB_V = 6, 8, 10
BLK_NS_Q = 12
BLK_NS_KS, BLK_NS_VS, BLK_NS_KW, BLK_NS_VW = 16, 18, 20, 22
N_QKV_BLOCKS = 24
QKV_MODES = (1, 1, 0, 0, 0, 0, 2, 2, 2, 2, 0, 0, 2, 2, 2, 2, 2, 2, 0, 0, 2, 2, 0, 0)


def _dot(a, b):
    return jnp.dot(a, b, preferred_element_type=F32)


def _dot_t(a, b):
    return lax.dot_general(a, b, (((1,), (1,)), ((), ())), preferred_element_type=F32)


def _dot_hilo(a, b):
    hi = a.astype(BF16)
    lo = (a - hi.astype(F32)).astype(BF16)
    return _dot(hi, b) + _dot(lo, b)


def _rms(x, g):
    ms = jnp.mean(x * x, axis=-1, keepdims=True)
    return x * lax.rsqrt(ms + NORM_EPS) * g


def _stack_heads(q):
    tq = q.shape[0]
    lo = lax.broadcasted_iota(jnp.int32, (tq, LANES), 1) < HEAD_DIM
    zero = jnp.zeros((tq, LANES), q.dtype)
    parts = []
    for c in range(q.shape[1] // LANES):
        blk = q[:, c * LANES:(c + 1) * LANES]
        parts.append(jnp.where(lo, blk, zero))
        parts.append(jnp.where(lo, zero, blk))
    return jnp.concatenate(parts, axis=0)


def _unstack_heads(o, tq):
    lo = lax.broadcasted_iota(jnp.int32, (tq, LANES), 1) < HEAD_DIM
    pairs = []
    for c in range(o.shape[0] // (2 * tq)):
        a = o[(2 * c) * tq:(2 * c + 1) * tq]
        b = o[(2 * c + 1) * tq:(2 * c + 2) * tq]
        pairs.append(jnp.where(lo, a, b))
    return pairs[0] if len(pairs) == 1 else jnp.concatenate(pairs, axis=1)


def _t5_bucket(dist):
    n = jnp.maximum(dist, 0)
    max_exact = N_BUCKETS // 2
    nf = jnp.maximum(n, 1).astype(F32)
    large = max_exact + (jnp.log(nf / max_exact) / math.log(REL_MAX_DISTANCE / max_exact)
                         * (N_BUCKETS - max_exact)).astype(jnp.int32)
    large = jnp.minimum(large, N_BUCKETS - 1)
    return jnp.where(n < max_exact, n, large)


def _bias_tile(tbl_ref, n_heads, dist):
    tq, w = dist.shape
    bucket = _t5_bucket(dist)
    rows = []
    for h in range(n_heads):
        t = jnp.broadcast_to(tbl_ref[h:h + 1, :], (tq, LANES))
        chunks = [jnp.take_along_axis(t, bucket[:, c * LANES:(c + 1) * LANES], axis=1)
                  for c in range(w // LANES)]
        rows.append(chunks[0] if len(chunks) == 1 else jnp.concatenate(chunks, axis=1))
    return jnp.concatenate(rows, axis=0)


def _far_bias(tbl_ref, n_heads, tq):
    return jnp.concatenate(
        [jnp.broadcast_to(tbl_ref[h:h + 1, N_BUCKETS - 1:N_BUCKETS], (tq, 1)) for h in range(n_heads)], axis=0)


def _softmax_init(m_ref, l_ref, acc_ref):
    m_ref[...] = jnp.full(m_ref.shape, NEG, F32)
    l_ref[...] = jnp.zeros(l_ref.shape, F32)
    acc_ref[...] = jnp.zeros(acc_ref.shape, F32)


def _softmax_update(m_ref, l_ref, acc_ref, logits, mask, v):
    lm = logits if mask is None else jnp.where(mask, logits, NEG)
    m_old = m_ref[...]
    m_new = jnp.maximum(m_old, jnp.max(lm, axis=-1, keepdims=True))
    p = jnp.exp(lm - m_new)
    if mask is not None:
        p = jnp.where(mask, p, 0.0)
    alpha = jnp.exp(m_old - m_new)
    l_ref[...] = alpha * l_ref[...] + jnp.sum(p, axis=-1, keepdims=True)
    acc_ref[...] = alpha * acc_ref[...] + _dot(p.astype(BF16), v)
    m_ref[...] = m_new


def _softmax_result(l_ref, acc_ref):
    return acc_ref[...] / jnp.maximum(l_ref[...], TINY)


def _top_n_mask(score, n):
    col = lax.broadcasted_iota(jnp.int32, score.shape, 1).astype(F32)
    sel = jnp.zeros(score.shape, F32)
    s = score
    for _ in range(n):
        m = jnp.max(s, axis=-1, keepdims=True)
        idx = jnp.min(jnp.where(s == m, col, float(LANES)), axis=-1, keepdims=True)
        pick = col == idx
        sel = jnp.where(pick, 1.0, sel)
        s = jnp.where(pick, -jnp.inf, s)
    return sel


def _inproj_kernel(x_ref, g_ref, wq_ref, wa_ref, gain_ref, qkv_ref, kc_ref, vc_ref, gate_ref):
    tm = x_ref.shape[0]
    h = _rms(x_ref[...], g_ref[...]).astype(BF16)
    lo = lax.broadcasted_iota(jnp.int32, (tm, LANES), 1) < HEAD_DIM
    for c in range(N_QKV_BLOCKS // 2):
        y = _dot(h, wq_ref[:, c * 2 * LANES:(c + 1) * 2 * LANES])
        for s in range(2):
            blk = 2 * c + s
            yb = y[:, s * LANES:(s + 1) * LANES]
            mode = QKV_MODES[blk]
            if mode == 2:
                sq = yb * yb
                s_lo = jnp.sum(jnp.where(lo, sq, 0.0), axis=-1, keepdims=True)
                s_hi = jnp.sum(jnp.where(lo, 0.0, sq), axis=-1, keepdims=True)
                ms = jnp.where(lo, s_lo, s_hi) * (1.0 / HEAD_DIM)
                yb = yb * lax.rsqrt(ms + NORM_EPS)
            if mode >= 1:
                yb = yb * gain_ref[:, blk * LANES:(blk + 1) * LANES]
            qkv_ref[:, blk * LANES:(blk + 1) * LANES] = yb.astype(BF16)
    aux = _dot(h, wa_ref[...])
    kc_ref[...] = aux[:, 0:LANES]
    vc_ref[...] = aux[:, LANES:2 * LANES]
    gate_ref[...] = jax.nn.sigmoid(aux[:, 2 * LANES:4 * LANES])


def _inproj(x2, g, wq, wa, gain, tm):
    t, d = x2.shape
    nq = wq.shape[1]
    return pl.pallas_call(
        _inproj_kernel,
        grid=(t // tm,),
        in_specs=[
            pl.BlockSpec((tm, d), lambda i: (i, 0)),
            pl.BlockSpec((1, d), lambda i: (0, 0)),
            pl.BlockSpec((d, nq), lambda i: (0, 0)),
            pl.BlockSpec((d, 4 * LANES), lambda i: (0, 0)),
            pl.BlockSpec((1, nq), lambda i: (0, 0)),
        ],
        out_specs=[
            pl.BlockSpec((tm, nq), lambda i: (i, 0)),
            pl.BlockSpec((tm, LANES), lambda i: (i, 0)),
            pl.BlockSpec((tm, LANES), lambda i: (i, 0)),
            pl.BlockSpec((tm, 2 * LANES), lambda i: (i, 0)),
        ],
        out_shape=[
            jax.ShapeDtypeStruct((t, nq), BF16),
            jax.ShapeDtypeStruct((t, LANES), F32),
            jax.ShapeDtypeStruct((t, LANES), F32),
            jax.ShapeDtypeStruct((t, 2 * LANES), F32),
        ],
        compiler_params=pltpu.CompilerParams(
            dimension_semantics=("parallel",), vmem_limit_bytes=VMEM_LIMIT),
        name="inproj",
    )(x2, g, wq, wa, gain)


def _sb_kernel(q_ref, k_ref, v_ref, o_ref, r_ref, acc_ref):
    tq = q_ref.shape[0]
    rows = 2 * tq
    qi = pl.program_id(2)
    qs = _stack_heads(q_ref[...])
    r_i = lax.broadcasted_iota(jnp.int32, (rows, tq), 0) & (tq - 1)
    c_i = lax.broadcasted_iota(jnp.int32, (rows, tq), 1)
    strict = c_i < r_i
    after = jnp.where(lax.broadcasted_iota(jnp.int32, (tq, tq), 0) > lax.broadcasted_iota(jnp.int32, (tq, tq), 1),
                      1.0, 0.0).astype(BF16)
    r_ref[...] = jnp.zeros(r_ref.shape, F32)
    acc_ref[...] = jnp.zeros(acc_ref.shape, F32)

    def block(kb, diag):
        start = pl.multiple_of(kb * tq, tq)
        k = k_ref[pl.ds(start, tq), :]
        v = v_ref[pl.ds(start, tq), :]
        z = _dot_t(qs, k)
        log_stay = -(jnp.maximum(z, 0.0) + jnp.log1p(jnp.exp(-jnp.abs(z))))
        ls = jnp.where(strict, log_stay, 0.0) if diag else log_stay
        log_after = _dot_hilo(ls, after) + r_ref[...]
        a = jnp.exp(z + log_stay + log_after)
        if diag:
            a = jnp.where(strict, a, 0.0)
        acc_ref[...] += _dot(a.astype(BF16), v)
        r_ref[...] += jnp.sum(ls, axis=-1, keepdims=True)

    block(qi, True)

    def body(j, carry):
        block(qi - 1 - j, False)
        return carry

    lax.fori_loop(0, qi, body, 0)
    o_ref[...] = _unstack_heads(acc_ref[...], tq).astype(o_ref.dtype)


def _sb_attention(qkv, b, s, tq):
    t = b * s
    nq = s // tq
    return pl.pallas_call(
        _sb_kernel,
        grid=(b, SB_HEADS // 2, nq),
        in_specs=[
            pl.BlockSpec((tq, LANES), lambda bi, p, i: (bi * nq + i, BLK_SB_Q + p)),
            pl.BlockSpec((s, LANES), lambda bi, p, i: (bi, BLK_SB_K + p)),
            pl.BlockSpec((s, LANES), lambda bi, p, i: (bi, BLK_SB_V + p)),
        ],
        out_specs=pl.BlockSpec((tq, LANES), lambda bi, p, i: (bi * nq + i, p)),
        out_shape=jax.ShapeDtypeStruct((t, SB_HEADS * HEAD_DIM), BF16),
        scratch_shapes=[pltpu.VMEM((2 * tq, 1), F32), pltpu.VMEM((2 * tq, LANES), F32)],
        compiler_params=pltpu.CompilerParams(
            dimension_semantics=("parallel", "parallel", "arbitrary"), vmem_limit_bytes=VMEM_LIMIT),
        name="stickbreak",
    )(qkv, qkv, qkv)


def _moba_kernel(q_ref, k_ref, v_ref, tbl_ref, o_ref, km_ref, bown_ref, bprev_ref, m_ref, l_ref, acc_ref, *, nblk):
    tq = q_ref.shape[0]
    rows = 2 * tq
    qi = pl.program_id(2)

    @pl.when(qi == 0)
    def _():
        km_ref[...] = jnp.zeros(km_ref.shape, F32)

        def mean_body(n, carry):
            kb = k_ref[pl.ds(pl.multiple_of(n * tq, tq), tq), :].astype(F32)
            km_ref[pl.ds(n, 1), :] = jnp.mean(kb, axis=0, keepdims=True)
            return carry

        lax.fori_loop(0, nblk, mean_body, 0)
        i = lax.broadcasted_iota(jnp.int32, (tq, tq), 0)
        j = lax.broadcasted_iota(jnp.int32, (tq, tq), 1)
        bown_ref[...] = _bias_tile(tbl_ref, 2, i - j)
        bprev_ref[...] = _bias_tile(tbl_ref, 2, tq + i - j)

    qs = _stack_heads(q_ref[...])
    col = lax.broadcasted_iota(jnp.int32, (rows, LANES), 1)
    past = col < qi
    score = jnp.where(past, _dot_t(qs, km_ref[...].astype(BF16)), NEG)
    sel = jnp.where(past, _top_n_mask(score, MOBA_TOPK), 0.0)
    far = _far_bias(tbl_ref, 2, tq)
    _softmax_init(m_ref, l_ref, acc_ref)

    def block(n, bias, mask):
        start = pl.multiple_of(n * tq, tq)
        logits = _dot_t(qs, k_ref[pl.ds(start, tq), :]) + bias
        _softmax_update(m_ref, l_ref, acc_ref, logits, mask, v_ref[pl.ds(start, tq), :])

    def chosen(n):
        hit = jnp.max(jnp.where(col == n, sel, 0.0), axis=-1, keepdims=True)
        return jnp.broadcast_to(hit, (rows, tq)) > 0.5

    def far_body(n, carry):
        block(n, far, chosen(n))
        return carry

    lax.fori_loop(0, jnp.maximum(qi - 1, 0), far_body, 0)

    @pl.when(qi >= 1)
    def _():
        block(qi - 1, bprev_ref[...], chosen(qi - 1))

    r_i = lax.broadcasted_iota(jnp.int32, (rows, tq), 0) & (tq - 1)
    c_i = lax.broadcasted_iota(jnp.int32, (rows, tq), 1)
    block(qi, bown_ref[...], c_i <= r_i)
    o_ref[...] = _unstack_heads(_softmax_result(l_ref, acc_ref), tq).astype(o_ref.dtype)


def _moba_attention(qkv, tbl, b, s):
    t = b * s
    tq = MOBA_BLOCK
    nq = s // tq
    rows = 2 * tq
    return pl.pallas_call(
        functools.partial(_moba_kernel, nblk=nq),
        grid=(b, MOBA_HEADS // 2, nq),
        in_specs=[
            pl.BlockSpec((tq, LANES), lambda bi, p, i: (bi * nq + i, BLK_MB_Q + p)),
            pl.BlockSpec((s, LANES), lambda bi, p, i: (bi, BLK_MB_K + p)),
            pl.BlockSpec((s, LANES), lambda bi, p, i: (bi, BLK_MB_V + p)),
            pl.BlockSpec((None, 8, LANES), lambda bi, p, i: (p, 0, 0)),
        ],
        out_specs=pl.BlockSpec((tq, LANES), lambda bi, p, i: (bi * nq + i, p)),
        out_shape=jax.ShapeDtypeStruct((t, MOBA_HEADS * HEAD_DIM), BF16),
        scratch_shapes=[
            pltpu.VMEM((LANES, LANES), F32),
            pltpu.VMEM((rows, tq), F32),
            pltpu.VMEM((rows, tq), F32),
            pltpu.VMEM((rows, 1), F32),
            pltpu.VMEM((rows, 1), F32),
            pltpu.VMEM((rows, LANES), F32),
        ],
        compiler_params=pltpu.CompilerParams(
            dimension_semantics=("arbitrary", "arbitrary", "arbitrary"), vmem_limit_bytes=VMEM_LIMIT),
        name="moba",
    )(qkv, qkv, qkv, tbl)


def _compress_kernel(rk_ref, rv_ref, wk_ref, wv_ref, pos_ref, w2_ref, kn_ref, kc_ref, vc_ref):
    n = rk_ref.shape[0]
    hid = CMP_HIDDEN
    for idx, (r_ref, w_ref, o_ref) in enumerate(((rk_ref, wk_ref, kc_ref), (rv_ref, wv_ref, vc_ref))):
        r = r_ref[...]
        first = _dot((r + pos_ref[2 * idx:2 * idx + 1, :]).astype(BF16), w_ref[:, :2 * hid])
        second = _dot((r + pos_ref[2 * idx + 1:2 * idx + 2, :]).astype(BF16), w_ref[:, 2 * hid:])
        h = jax.nn.gelu(first + pltpu.roll(second, shift=n - 1, axis=0))
        for g in range(NSA_KV_GROUPS):
            y = _dot(h[:, g * hid:(g + 1) * hid].astype(BF16), w2_ref[idx])
            if idx == 0:
                y = _rms(y, kn_ref[...])
            o_ref[g] = y.astype(o_ref.dtype)


def _compress(rk, rv, wk, wv, pos, w2, kn):
    b, n, width = rk.shape
    return pl.pallas_call(
        _compress_kernel,
        grid=(b,),
        in_specs=[
            pl.BlockSpec((None, n, width), lambda i: (i, 0, 0)),
            pl.BlockSpec((None, n, width), lambda i: (i, 0, 0)),
            pl.BlockSpec(wk.shape, lambda i: (0, 0)),
            pl.BlockSpec(wv.shape, lambda i: (0, 0)),
            pl.BlockSpec(pos.shape, lambda i: (0, 0)),
            pl.BlockSpec(w2.shape, lambda i: (0, 0, 0)),
            pl.BlockSpec(kn.shape, lambda i: (0, 0)),
        ],
        out_specs=[
            pl.BlockSpec((None, NSA_KV_GROUPS, n, LANES), lambda i: (i, 0, 0, 0)),
            pl.BlockSpec((None, NSA_KV_GROUPS, n, LANES), lambda i: (i, 0, 0, 0)),
        ],
        out_shape=[
            jax.ShapeDtypeStruct((b, NSA_KV_GROUPS, n, LANES), BF16),
            jax.ShapeDtypeStruct((b, NSA_KV_GROUPS, n, LANES), BF16),
        ],
        compiler_params=pltpu.CompilerParams(
            dimension_semantics=("parallel",), vmem_limit_bytes=VMEM_LIMIT),
        name="nsa_compress",
    )(rk, rv, wk, wv, pos, w2, kn)


def _nsa_kernel(q_ref, kc_ref, vc_ref, ks_ref, vs_ref, kw_ref, vw_ref, gate_ref, tbl_ref, o_ref,
                bdiag_ref, bnear_ref, ov_ref, m_ref, l_ref, acc_ref, tot_ref):
    tq = q_ref.shape[0]
    nh = NSA_HPG
    rows = nh * tq
    ncmp = kc_ref.shape[0]
    qi = pl.program_id(2)
    q0 = qi * tq

    @pl.when(qi == 0)
    def _():
        i = lax.broadcasted_iota(jnp.int32, (tq, tq), 0)
        j = lax.broadcasted_iota(jnp.int32, (tq, tq), 1)
        bdiag_ref[...] = _bias_tile(tbl_ref, nh, i - j)
        bnear_ref[...] = _bias_tile(tbl_ref, nh, tq + i - j)
        cs = lax.broadcasted_iota(jnp.int32, (ncmp, LANES), 0) * CMP_STRIDE
        ss = lax.broadcasted_iota(jnp.int32, (ncmp, LANES), 1) * SLC_BLOCK
        ov_ref[...] = jnp.where((cs < ss + SLC_BLOCK) & (cs + CMP_BLOCK > ss), 1.0, 0.0).astype(BF16)

    qs = _stack_heads(q_ref[...])
    far = _far_bias(tbl_ref, nh, tq)
    gates = gate_ref[...]

    def gate(branch):
        return jnp.concatenate(
            [gates[:, branch * nh + h:branch * nh + h + 1] for h in range(nh)], axis=0)

    i_c = lax.broadcasted_iota(jnp.int32, (tq, ncmp), 0)
    n_c = lax.broadcasted_iota(jnp.int32, (tq, ncmp), 1)
    dist_c = q0 + i_c - (n_c * CMP_STRIDE + CMP_BLOCK - 1)
    r_c = lax.broadcasted_iota(jnp.int32, (rows, ncmp), 0) & (tq - 1)
    n_r = lax.broadcasted_iota(jnp.int32, (rows, ncmp), 1)
    valid_c = q0 + r_c - (n_r * CMP_STRIDE + CMP_BLOCK - 1) >= 0
    logit_c = jnp.where(valid_c, _dot_t(qs, kc_ref[...]) + _bias_tile(tbl_ref, nh, dist_c), NEG)
    p_c = jnp.where(valid_c, jnp.exp(logit_c - jnp.max(logit_c, axis=-1, keepdims=True)), 0.0)
    p_c = p_c / jnp.maximum(jnp.sum(p_c, axis=-1, keepdims=True), TINY)
    tot_ref[...] = gate(0) * _dot(p_c.astype(BF16), vc_ref[...])

    p_sum = p_c[0:tq]
    for h in range(1, nh):
        p_sum = p_sum + p_c[h * tq:(h + 1) * tq]
    imp = _dot_hilo(p_sum, ov_ref[...])
    blk = lax.broadcasted_iota(jnp.int32, (tq, LANES), 1)
    cur = (q0 + lax.broadcasted_iota(jnp.int32, (tq, LANES), 0)) // SLC_BLOCK
    forced = (blk == 0) | (blk == cur) | (blk == cur - 1)
    score = jnp.where(blk <= cur, jnp.where(forced, BIG, imp), NEG)
    sel = jnp.where(blk <= cur, _top_n_mask(score, SLC_TOPN), 0.0).astype(BF16)

    r_i = lax.broadcasted_iota(jnp.int32, (rows, tq), 0) & (tq - 1)
    c_i = lax.broadcasted_iota(jnp.int32, (rows, tq), 1)
    causal = c_i <= r_i
    per_tile = tq // SLC_BLOCK

    def block(k_ref, v_ref, kb, bias, mask):
        start = pl.multiple_of(kb * tq, tq)
        logits = _dot_t(qs, k_ref[pl.ds(start, tq), :]) + bias
        _softmax_update(m_ref, l_ref, acc_ref, logits, mask, v_ref[pl.ds(start, tq), :])

    def chosen(kb):
        m_i = lax.broadcasted_iota(jnp.int32, (LANES, tq), 0)
        j_i = lax.broadcasted_iota(jnp.int32, (LANES, tq), 1)
        expand = jnp.where(m_i == kb * per_tile + j_i // SLC_BLOCK, 1.0, 0.0).astype(BF16)
        hit = _dot(sel, expand)
        return jnp.concatenate([hit] * nh, axis=0) > 0.5

    _softmax_init(m_ref, l_ref, acc_ref)

    def sel_body(kb, carry):
        block(ks_ref, vs_ref, kb, far, chosen(kb))
        return carry

    lax.fori_loop(0, jnp.maximum(qi - 1, 0), sel_body, 0)

    @pl.when(qi >= 1)
    def _():
        block(ks_ref, vs_ref, qi - 1, bnear_ref[...], chosen(qi - 1))

    block(ks_ref, vs_ref, qi, bdiag_ref[...], chosen(qi) & causal)
    tot_ref[...] += gate(1) * _softmax_result(l_ref, acc_ref)

    _softmax_init(m_ref, l_ref, acc_ref)
    nwin = WINDOW // tq
    for r in range(nwin, 1, -1):
        @pl.when(qi >= r)
        def _(r=r):
            block(kw_ref, vw_ref, qi - r, far, (c_i > r_i) if r == nwin else None)

    @pl.when(qi >= 1)
    def _():
        block(kw_ref, vw_ref, qi - 1, bnear_ref[...], None)

    block(kw_ref, vw_ref, qi, bdiag_ref[...], causal)
    tot_ref[...] += gate(2) * _softmax_result(l_ref, acc_ref)
    o_ref[...] = _unstack_heads(tot_ref[...], tq).astype(o_ref.dtype)


def _nsa_attention(qkv, kc, vc, gates, tbl, b, s, tq):
    t = b * s
    nq = s // tq
    ncmp = kc.shape[2]
    rows = NSA_HPG * tq
    kv_spec = lambda blk: pl.BlockSpec((s, LANES), lambda bi, g, i: (bi, blk + g))
    return pl.pallas_call(
        _nsa_kernel,
        grid=(b, NSA_KV_GROUPS, nq),
        in_specs=[
            pl.BlockSpec((tq, 2 * LANES), lambda bi, g, i: (bi * nq + i, BLK_NS_Q // 2 + g)),
            pl.BlockSpec((None, None, ncmp, LANES), lambda bi, g, i: (bi, g, 0, 0)),
            pl.BlockSpec((None, None, ncmp, LANES), lambda bi, g, i: (bi, g, 0, 0)),
            kv_spec(BLK_NS_KS), kv_spec(BLK_NS_VS), kv_spec(BLK_NS_KW), kv_spec(BLK_NS_VW),
            pl.BlockSpec((tq, LANES), lambda bi, g, i: (bi * nq + i, g)),
            pl.BlockSpec((None, 8, LANES), lambda bi, g, i: (g, 0, 0)),
        ],
        out_specs=pl.BlockSpec((tq, 2 * LANES), lambda bi, g, i: (bi * nq + i, g)),
        out_shape=jax.ShapeDtypeStruct((t, NSA_HEADS * HEAD_DIM), BF16),
        scratch_shapes=[
            pltpu.VMEM((rows, tq), F32),
            pltpu.VMEM((rows, tq), F32),
            pltpu.VMEM((ncmp, LANES), BF16),
            pltpu.VMEM((rows, 1), F32),
            pltpu.VMEM((rows, 1), F32),
            pltpu.VMEM((rows, LANES), F32),
            pltpu.VMEM((rows, LANES), F32),
        ],
        compiler_params=pltpu.CompilerParams(
            dimension_semantics=("arbitrary", "arbitrary", "arbitrary"), vmem_limit_bytes=VMEM_LIMIT),
        name="nsa",
    )(qkv, kc, vc, qkv, qkv, qkv, qkv, gates, tbl)


def _merge_kernel(x_ref, g_ref, oa_ref, ob_ref, oc_ref, wg_ref, wb_ref, wo_ref, o_ref):
    x = x_ref[...]
    d = x.shape[1]
    h = _rms(x, g_ref[...]).astype(BF16)
    mix = None
    row = 0
    for br, src in enumerate((oa_ref, ob_ref, oc_ref)):
        width = src.shape[1]
        gate = jax.nn.sigmoid(_dot(h, wg_ref[:, br * d:(br + 1) * d]))
        term = gate * _dot(src[...], wb_ref[row:row + width, :])
        mix = term if mix is None else mix + term
        row += width
    o_ref[...] = x + _dot(mix.astype(BF16), wo_ref[...])


def _merge(x2, g, oa, ob, oc, wg, wb, wo, tm):
    t, d = x2.shape
    row = lambda w: pl.BlockSpec((tm, w), lambda i: (i, 0))
    full = lambda a: pl.BlockSpec(a.shape, lambda i: (0, 0))
    return pl.pallas_call(
        _merge_kernel,
        grid=(t // tm,),
        in_specs=[row(d), full(g), row(oa.shape[1]), row(ob.shape[1]), row(oc.shape[1]),
                  full(wg), full(wb), full(wo)],
        out_specs=row(d),
        out_shape=jax.ShapeDtypeStruct((t, d), F32),
        compiler_params=pltpu.CompilerParams(
            dimension_semantics=("parallel",), vmem_limit_bytes=VMEM_LIMIT),
        name="merge",
    )(x2, g, oa, ob, oc, wg, wb, wo)


def _ffn_kernel(x_ref, g_ref, wgu_ref, wd_ref, o_ref, *, n_chunks):
    x = x_ref[...]
    d_ff = wd_ref.shape[0]
    ch = d_ff // n_chunks
    h = _rms(x, g_ref[...]).astype(BF16)
    out = x
    for c in range(n_chunks):
        gate = _dot(h, wgu_ref[:, c * ch:(c + 1) * ch])
        up = _dot(h, wgu_ref[:, d_ff + c * ch:d_ff + (c + 1) * ch])
        out = out + _dot((jax.nn.silu(gate) * up).astype(BF16), wd_ref[c * ch:(c + 1) * ch, :])
    o_ref[...] = out


def _ffn(x2, g, wgu, wd, tm):
    t, d = x2.shape
    d_ff = wd.shape[0]
    n_chunks = 2 if d_ff % (2 * LANES) == 0 else 1
    return pl.pallas_call(
        functools.partial(_ffn_kernel, n_chunks=n_chunks),
        grid=(t // tm,),
        in_specs=[
            pl.BlockSpec((tm, d), lambda i: (i, 0)),
            pl.BlockSpec((1, d), lambda i: (0, 0)),
            pl.BlockSpec(wgu.shape, lambda i: (0, 0)),
            pl.BlockSpec(wd.shape, lambda i: (0, 0)),
        ],
        out_specs=pl.BlockSpec((tm, d), lambda i: (i, 0)),
        out_shape=jax.ShapeDtypeStruct((t, d), F32),
        compiler_params=pltpu.CompilerParams(
            dimension_semantics=("parallel",), vmem_limit_bytes=VMEM_LIMIT),
        name="swiglu",
    )(x2, g, wgu, wd)


def _dup_groups(w):
    d = w.shape[0]
    w = w.reshape(d, NSA_KV_GROUPS, 1, HEAD_DIM)
    return jnp.broadcast_to(w, (d, NSA_KV_GROUPS, 2, HEAD_DIM)).reshape(d, NSA_KV_GROUPS * PAIR)


def _pair_gain(g, scale=1.0):
    return jnp.concatenate([g, g]) * scale


def _layer_params(w_in, moba_q_norm, moba_k_norm, nsa_q_norm, nsa_k_norm, nsa_cmp_pos, nsa_cmp_w1, nsa_cmp_w2):
    d = w_in.shape[0]
    sbw, mbw, nsw, kvw = SB_HEADS * HEAD_DIM, MOBA_HEADS * HEAD_DIM, NSA_HEADS * HEAD_DIM, NSA_KV_GROUPS * HEAD_DIM
    o = 3 * sbw + 3 * mbw + nsw
    kc_w, vc_w, ks_w, vs_w, kw_w, vw_w = (w_in[:, o + i * kvw:o + (i + 1) * kvw] for i in range(6))
    o += 6 * kvw
    gate_w = w_in[:, o:o + N_BRANCHES * NSA_HEADS].reshape(d, N_BRANCHES, NSA_KV_GROUPS, NSA_HPG)
    o += N_BRANCHES * NSA_HEADS
    wg = w_in[:, o:].astype(BF16)
    wq = jnp.concatenate([w_in[:, :3 * sbw + 3 * mbw + nsw], _dup_groups(ks_w), _dup_groups(vs_w),
                          _dup_groups(kw_w), _dup_groups(vw_w)], axis=1).astype(BF16)
    gate_cols = []
    for g in range(NSA_KV_GROUPS):
        cols = gate_w[:, :, g, :].reshape(d, N_BRANCHES * NSA_HPG)
        gate_cols.append(jnp.pad(cols, ((0, 0), (0, LANES - N_BRANCHES * NSA_HPG))))
    wa = jnp.concatenate([kc_w, vc_w] + gate_cols, axis=1).astype(BF16)
    scale = HEAD_DIM ** -0.5
    ones = jnp.ones((LANES,), F32)
    gains = [ones * scale] * 2 + [ones] * 4
    gains += [_pair_gain(moba_q_norm, scale)] * 2 + [_pair_gain(moba_k_norm)] * 2 + [ones] * 2
    gains += [_pair_gain(nsa_q_norm, scale)] * 4
    gains += [_pair_gain(nsa_k_norm[1])] * 2 + [ones] * 2 + [_pair_gain(nsa_k_norm[2])] * 2 + [ones] * 2
    gain = jnp.concatenate(gains).reshape(1, N_QKV_BLOCKS * LANES)

    half = CMP_BLOCK // 2
    hid = CMP_HIDDEN

    def spread(w1):
        w1 = w1.reshape(2, half, HEAD_DIM, hid)
        cols = []
        for part in range(2):
            for g in range(NSA_KV_GROUPS):
                z = jnp.zeros((half, NSA_KV_GROUPS, HEAD_DIM, hid), F32).at[:, g].set(w1[part])
                cols.append(z.reshape(half * kvw, hid))
        return jnp.concatenate(cols, axis=1).astype(BF16)

    def spread_pos(p):
        p = p.reshape(2, half, 1, HEAD_DIM)
        return jnp.broadcast_to(p, (2, half, NSA_KV_GROUPS, HEAD_DIM)).reshape(2, half * kvw)

    pos = jnp.concatenate([spread_pos(nsa_cmp_pos[0]), spread_pos(nsa_cmp_pos[1])], axis=0)
    w2 = jnp.concatenate([nsa_cmp_w2, nsa_cmp_w2], axis=-1).astype(BF16)
    kn = _pair_gain(nsa_k_norm[0]).reshape(1, LANES)
    return dict(wq=wq, wa=wa, gain=gain, wg=wg, wk=spread(nsa_cmp_w1[0]), wv=spread(nsa_cmp_w1[1]),
                pos=pos, w2=w2, kn=kn)


def _bias_tables(rel_bias):
    tbl = jnp.pad(rel_bias.T, ((0, 0), (0, LANES - N_BUCKETS)))

    def rows(t, n):
        t = t.reshape(n, -1, LANES)
        return jnp.pad(t, ((0, 0), (0, 8 - t.shape[1]), (0, 0)))

    return rows(tbl[:MOBA_HEADS], MOBA_HEADS // 2), rows(tbl[MOBA_HEADS:], NSA_KV_GROUPS)


def kernel(x, rel_bias, attn_norm, w_in, moba_q_norm, moba_k_norm, nsa_q_norm, nsa_k_norm, nsa_cmp_pos,
           nsa_cmp_w1, nsa_cmp_w2, w_branch, w_out, ffn_norm, w_gate_up, w_down):
    b, s, d = x.shape
    assert s % MOBA_BLOCK == 0 and s % (CMP_STRIDE * 8) == 0
    t = b * s
    tm = min(512, t)
    tq_sb = min(256, s)
    tq_nsa = 128
    tbl_moba, tbl_nsa = _bias_tables(rel_bias)
    x2 = x.reshape(t, d)
    for layer in range(w_in.shape[0]):
        p = _layer_params(w_in[layer], moba_q_norm[layer], moba_k_norm[layer], nsa_q_norm[layer],
                          nsa_k_norm[layer], nsa_cmp_pos[layer], nsa_cmp_w1[layer], nsa_cmp_w2[layer])
        g_attn = attn_norm[layer].reshape(1, d)
        qkv, kc_in, vc_in, gates = _inproj(x2, g_attn, p["wq"], p["wa"], p["gain"], tm)
        o_a = _sb_attention(qkv, b, s, tq_sb)
        o_b = _moba_attention(qkv, tbl_moba, b, s)
        chunk = CMP_STRIDE * LANES
        kc, vc = _compress(kc_in.reshape(b, s // CMP_STRIDE, chunk), vc_in.reshape(b, s // CMP_STRIDE, chunk),
                           p["wk"], p["wv"], p["pos"], p["w2"], p["kn"])
        o_c = _nsa_attention(qkv, kc, vc, gates, tbl_nsa, b, s, tq_nsa)
        x2 = _merge(x2, g_attn, o_a, o_b, o_c, p["wg"], w_branch[layer].astype(BF16),
                    w_out[layer].astype(BF16), tm)
        x2 = _ffn(x2, ffn_norm[layer].reshape(1, d), w_gate_up[layer].astype(BF16),
                  w_down[layer].astype(BF16), tm)
    return x2.reshape(b, s, d)
```

```python
import functools
import math

import jax
import jax.numpy as jnp
from jax import lax
from jax.experimental import pallas as pl
from jax.experimental.pallas import tpu as pltpu

HEAD_DIM = 64
SB_HEADS = 4
MOBA_HEADS = 4
NSA_HEADS = 8
NSA_KV_GROUPS = 2
NSA_HPG = NSA_HEADS // NSA_KV_GROUPS
N_BRANCHES = 3
MOBA_BLOCK = 256
MOBA_TOPK = 3
CMP_BLOCK = 32
CMP_STRIDE = 16
CMP_HIDDEN = 4 * HEAD_DIM
SLC_BLOCK = 64
SLC_TOPN = 16
WINDOW = 512
N_BUCKETS = 32
REL_MAX_DISTANCE = 128
NORM_EPS = 1e-6
NEG = -1e30
BIG = 1e30
TINY = 1e-30
LOG2E = math.log2(math.e)

LANES = 128
SUBLANES = 8
PAIR = 2 * HEAD_DIM
VMEM_LIMIT = 56 * 1024 * 1024

F32 = jnp.float32
BF16 = jnp.bfloat16

K_SB, K_MB, K_NS_SLC, K_NS_WIN = 0, 2, 4, 6
N_K_BLOCKS = 8
F_SB_Q, F_MB_Q, F_NS_Q, F_SB_V, F_MB_V, F_NS_SLC_V, F_NS_WIN_V = 0, 2, 4, 8, 10, 12, 14
N_F_BLOCKS = 16
K_MODES = (0, 0, 2, 2, 2, 2, 2, 2)
F_MODES = (1, 1, 2, 2, 2, 2, 2, 2, 0, 0, 0, 0, 3, 3, 3, 3)


SLC_SHIFT = SLC_BLOCK.bit_length() - 1
MOBA_SHIFT = MOBA_BLOCK.bit_length() - 1
assert 1 << SLC_SHIFT == SLC_BLOCK and 1 << MOBA_SHIFT == MOBA_BLOCK


def _ceil_to(n, m):
    return -(-n // m) * m


def _dot(a, b):
    return jnp.dot(a, b, preferred_element_type=F32)


def _dot_hilo_rhs(a, b):
    hi = b.astype(BF16)
    lo = (b - hi.astype(F32)).astype(BF16)
    return _dot(a, hi) + _dot(a, lo)


def _rms(x, g):
    ms = jnp.mean(x * x, axis=-1, keepdims=True)
    return x * lax.rsqrt(ms + NORM_EPS) * g


def _stack_heads_t(q_t):
    tq = q_t.shape[1]
    lo = lax.broadcasted_iota(jnp.int32, (LANES, tq), 0) < HEAD_DIM
    zero = jnp.zeros((LANES, tq), q_t.dtype)
    parts = []
    for c in range(q_t.shape[0] // LANES):
        blk = q_t[c * LANES:(c + 1) * LANES, :]
        parts.append(jnp.where(lo, blk, zero))
        parts.append(jnp.where(lo, zero, blk))
    return jnp.concatenate(parts, axis=1)


def _t5_bucket(dist):
    n = jnp.maximum(dist, 0)
    max_exact = N_BUCKETS // 2
    nf = jnp.maximum(n, 1).astype(F32)
    large = max_exact + (jnp.log(nf / max_exact) / math.log(REL_MAX_DISTANCE / max_exact)
                         * (N_BUCKETS - max_exact)).astype(jnp.int32)
    large = jnp.minimum(large, N_BUCKETS - 1)
    return jnp.where(n < max_exact, n, large)


def _bias_tile_t(tbl_ref, n_heads, dist):
    nk, tq = dist.shape
    bucket = _t5_bucket(dist)
    cols = []
    for h in range(n_heads):
        t = jnp.broadcast_to(tbl_ref[h:h + 1, :], (nk, LANES))
        far = t[:, N_BUCKETS - 1:N_BUCKETS]
        for c in range(tq // LANES):
            cols.append((jnp.take_along_axis(t, bucket[:, c * LANES:(c + 1) * LANES], axis=1) - far) * LOG2E)
    return jnp.concatenate(cols, axis=1)


def _top_n_mask_t(score, n):
    row = lax.broadcasted_iota(jnp.int32, score.shape, 0).astype(F32)
    sel = jnp.zeros(score.shape, F32)
    s = score
    for _ in range(n):
        m = jnp.max(s, axis=0, keepdims=True)
        idx = jnp.min(jnp.where(s == m, row, float(score.shape[0])), axis=0, keepdims=True)
        pick = row == idx
        sel = jnp.where(pick, 1.0, sel)
        s = jnp.where(pick, -jnp.inf, s)
    return sel


def _post(yb, lo, mode, gain):
    if mode == 2:
        sq = yb * yb
        s_lo = jnp.sum(jnp.where(lo, sq, 0.0), axis=-1, keepdims=True)
        s_hi = jnp.sum(jnp.where(lo, 0.0, sq), axis=-1, keepdims=True)
        yb = yb * lax.rsqrt(jnp.where(lo, s_lo, s_hi) * (1.0 / HEAD_DIM) + NORM_EPS)
    if mode in (1, 2):
        yb = yb * gain
    return yb


def _inproj_kernel(x_ref, g_ref, w_ref, wa_ref, gain_ref, k_ref, f_ref, kc_ref, vc_ref, gate_ref):
    tm = x_ref.shape[0]
    h = _rms(x_ref[...], g_ref[...]).astype(BF16)
    lo = lax.broadcasted_iota(jnp.int32, (tm, LANES), 1) < HEAD_DIM
    for c in range((N_K_BLOCKS + N_F_BLOCKS) // 2):
        y = _dot(h, w_ref[:, c * 2 * LANES:(c + 1) * 2 * LANES])
        for s in range(2):
            blk = 2 * c + s
            gain = gain_ref[:, blk * LANES:(blk + 1) * LANES]
            yb = y[:, s * LANES:(s + 1) * LANES]
            if blk < N_K_BLOCKS:
                k_ref[:, blk * LANES:(blk + 1) * LANES] = _post(yb, lo, K_MODES[blk], gain).astype(BF16)
            else:
                fb = blk - N_K_BLOCKS
                yb = jnp.where(lo, yb, 1.0) if F_MODES[fb] == 3 else _post(yb, lo, F_MODES[fb], gain)
                f_ref[fb * LANES:(fb + 1) * LANES, :] = yb.T.astype(BF16)
    aux = _dot(h, wa_ref[...])
    kc_ref[...] = aux[:, 0:LANES]
    vc_ref[...] = aux[:, LANES:2 * LANES]
    for g in range(NSA_KV_GROUPS):
        gate_ref[g * LANES:(g + 1) * LANES, :] = jax.nn.sigmoid(aux[:, (2 + g) * LANES:(3 + g) * LANES]).T


def _inproj(x2, g, w, wa, gain, tm):
    t, d = x2.shape
    nw = w.shape[1]
    return pl.pallas_call(
        _inproj_kernel,
        grid=(t // tm,),
        in_specs=[
            pl.BlockSpec((tm, d), lambda i: (i, 0)),
            pl.BlockSpec((1, d), lambda i: (0, 0)),
            pl.BlockSpec((d, nw), lambda i: (0, 0)),
            pl.BlockSpec((d, 4 * LANES), lambda i: (0, 0)),
            pl.BlockSpec((1, nw), lambda i: (0, 0)),
        ],
        out_specs=[
            pl.BlockSpec((tm, N_K_BLOCKS * LANES), lambda i: (i, 0)),
            pl.BlockSpec((N_F_BLOCKS * LANES, tm), lambda i: (0, i)),
            pl.BlockSpec((tm, LANES), lambda i: (i, 0)),
            pl.BlockSpec((tm, LANES), lambda i: (i, 0)),
            pl.BlockSpec((NSA_KV_GROUPS * LANES, tm), lambda i: (0, i)),
        ],
        out_shape=[
            jax.ShapeDtypeStruct((t, N_K_BLOCKS * LANES), BF16),
            jax.ShapeDtypeStruct((N_F_BLOCKS * LANES, t), BF16),
            jax.ShapeDtypeStruct((t, LANES), F32),
            jax.ShapeDtypeStruct((t, LANES), F32),
            jax.ShapeDtypeStruct((NSA_KV_GROUPS * LANES, t), F32),
        ],
        compiler_params=pltpu.CompilerParams(
            dimension_semantics=("parallel",), vmem_limit_bytes=VMEM_LIMIT),
        name="inproj",
    )(x2, g, w, wa, gain)


def _sb_kernel(q_ref, k_ref, v_ref, o_ref, r_ref, acc_ref, *, tk):
    tq = q_ref.shape[1]
    cols = 2 * tq
    nsub = tq // tk
    qi = pl.program_id(2)
    qs = _stack_heads_t(q_ref[...])
    j_i = lax.broadcasted_iota(jnp.int32, (tk, cols), 0)
    t_i = lax.broadcasted_iota(jnp.int32, (tk, cols), 1) & (tq - 1)
    after = jnp.where(lax.broadcasted_iota(jnp.int32, (tk, tk), 1) > lax.broadcasted_iota(jnp.int32, (tk, tk), 0),
                      1.0, 0.0).astype(BF16)
    r_ref[...] = jnp.zeros(r_ref.shape, F32)
    acc_ref[...] = jnp.zeros(acc_ref.shape, F32)

    def block(kb, strict):
        start = pl.multiple_of(kb * tk, tk)
        z = _dot(k_ref[pl.ds(start, tk), :], qs)
        log_stay = -(jnp.maximum(z, 0.0) + jnp.log(1.0 + jnp.exp(-jnp.abs(z))))
        ls = log_stay if strict is None else jnp.where(strict, log_stay, 0.0)
        log_after = _dot_hilo_rhs(after, ls) + r_ref[...]
        a = jnp.exp(z + log_stay + log_after)
        if strict is not None:
            a = jnp.where(strict, a, 0.0)
        acc_ref[...] += _dot(v_ref[:, pl.ds(start, tk)], a.astype(BF16))
        r_ref[...] += jnp.sum(ls, axis=0, keepdims=True)

    for u in range(nsub - 1, -1, -1):
        block(qi * nsub + u, u * tk + j_i < t_i)

    def body(j, carry):
        block(qi * nsub - 1 - j, None)
        return carry

    lax.fori_loop(0, qi * nsub, body, 0)
    lo = lax.broadcasted_iota(jnp.int32, (LANES, tq), 0) < HEAD_DIM
    acc = acc_ref[...]
    o_ref[...] = jnp.where(lo, acc[:, :tq], acc[:, tq:]).T.astype(o_ref.dtype)


def _sb_attention(kmat, feat, b, s, tq, tk):
    t = b * s
    nq = s // tq
    return pl.pallas_call(
        functools.partial(_sb_kernel, tk=tk),
        grid=(b, SB_HEADS // 2, nq),
        in_specs=[
            pl.BlockSpec((LANES, tq), lambda bi, p, i: (F_SB_Q + p, bi * nq + i)),
            pl.BlockSpec((s, LANES), lambda bi, p, i: (bi, K_SB + p)),
            pl.BlockSpec((LANES, s), lambda bi, p, i: (F_SB_V + p, bi)),
        ],
        out_specs=pl.BlockSpec((tq, LANES), lambda bi, p, i: (bi * nq + i, p)),
        out_shape=jax.ShapeDtypeStruct((t, SB_HEADS * HEAD_DIM), BF16),
        scratch_shapes=[pltpu.VMEM((1, 2 * tq), F32), pltpu.VMEM((LANES, 2 * tq), F32)],
        compiler_params=pltpu.CompilerParams(
            dimension_semantics=("parallel", "parallel", "arbitrary"), vmem_limit_bytes=VMEM_LIMIT),
        name="stickbreak",
    )(feat, kmat, feat)


def _softmax_init(m_ref, l_ref, acc_ref):
    m_ref[...] = jnp.full(m_ref.shape, NEG, F32)
    if l_ref is not None:
        l_ref[...] = jnp.zeros(l_ref.shape, F32)
    acc_ref[...] = jnp.zeros(acc_ref.shape, F32)


def _softmax_update(m_ref, l_ref, acc_ref, logits, mask, v_t):
    lm = logits if mask is None else jnp.where(mask, logits, -jnp.inf)
    m_old = m_ref[...]
    m_new = jnp.maximum(m_old, jnp.max(lm, axis=0, keepdims=True))
    p = jnp.exp2(lm - m_new)
    alpha = jnp.exp2(m_old - m_new)
    if l_ref is not None:
        l_ref[...] = alpha * l_ref[...] + jnp.sum(p, axis=0, keepdims=True)
    acc_ref[...] = alpha * acc_ref[...] + _dot(v_t, p.astype(BF16))
    m_ref[...] = m_new


def _moba_kernel(q_ref, k_ref, v_ref, tbl_ref, o_ref, km_ref, bias_ref, sel_ref, m_ref, l_ref, acc_ref, *, nblk):
    tk = MOBA_BLOCK
    tq = q_ref.shape[1]
    nsub = tq // tk
    cols = 2 * tq
    qi = pl.program_id(2)

    @pl.when(qi == 0)
    def _():
        km_ref[...] = jnp.zeros(km_ref.shape, F32)

        def mean_body(n, carry):
            kb = k_ref[pl.ds(pl.multiple_of(n * tk, tk), tk), :].astype(F32)
            km_ref[pl.ds(n, 1), :] = jnp.mean(kb, axis=0, keepdims=True)
            return carry

        lax.fori_loop(0, nblk, mean_body, 0)
        j = lax.broadcasted_iota(jnp.int32, (tk, tq), 0)
        i = lax.broadcasted_iota(jnp.int32, (tk, tq), 1)
        for e in range(-1, nsub):
            bias_ref[e + 1] = _bias_tile_t(tbl_ref, 2, i - tk * e - j)

    qs = _stack_heads_t(q_ref[...])
    sub_s = lax.broadcasted_iota(jnp.int32, (LANES, cols), 1) & (tq - 1)
    past = lax.broadcasted_iota(jnp.int32, (LANES, cols), 0) < nsub * qi + (sub_s >> MOBA_SHIFT)
    score = jnp.where(past, _dot(km_ref[...].astype(BF16), qs), NEG)
    sel_ref[...] = jnp.where(past, _top_n_mask_t(score, MOBA_TOPK), 0.0)
    _softmax_init(m_ref, l_ref, acc_ref)

    def block(n, bias, mask):
        start = pl.multiple_of(n * tk, tk)
        logits = _dot(k_ref[pl.ds(start, tk), :], qs)
        if bias is not None:
            logits = logits + bias
        _softmax_update(m_ref, l_ref, acc_ref, logits, mask, v_ref[:, pl.ds(start, tk)])

    def chosen(n):
        return jnp.broadcast_to(sel_ref[pl.ds(n, 1), :], (tk, cols)) > 0.5

    def far_body(n, carry):
        block(n, None, chosen(n))
        return carry

    lax.fori_loop(0, jnp.maximum(nsub * qi - 1, 0), far_body, 0)

    @pl.when(qi >= 1)
    def _():
        block(nsub * qi - 1, bias_ref[0], chosen(nsub * qi - 1))

    j_i = lax.broadcasted_iota(jnp.int32, (tk, cols), 0)
    t_i = lax.broadcasted_iota(jnp.int32, (tk, cols), 1) & (tq - 1)
    for e in range(nsub):
        own = (t_i >> MOBA_SHIFT) == e
        block(nsub * qi + e, bias_ref[e + 1], (own & (j_i <= t_i - tk * e)) | chosen(nsub * qi + e))
    out = acc_ref[...] / jnp.maximum(l_ref[...], TINY)
    lo = lax.broadcasted_iota(jnp.int32, (LANES, tq), 0) < HEAD_DIM
    o_ref[...] = jnp.where(lo, out[:, :tq], out[:, tq:]).T.astype(o_ref.dtype)


def _moba_attention(kmat, feat, tbl, b, s, tq):
    t = b * s
    tk = MOBA_BLOCK
    nq = s // tq
    cols = 2 * tq
    return pl.pallas_call(
        functools.partial(_moba_kernel, nblk=s // tk),
        grid=(b, MOBA_HEADS // 2, nq),
        in_specs=[
            pl.BlockSpec((LANES, tq), lambda bi, p, i: (F_MB_Q + p, bi * nq + i)),
            pl.BlockSpec((s, LANES), lambda bi, p, i: (bi, K_MB + p)),
            pl.BlockSpec((LANES, s), lambda bi, p, i: (F_MB_V + p, bi)),
            pl.BlockSpec((None, SUBLANES, LANES), lambda bi, p, i: (p, 0, 0)),
        ],
        out_specs=pl.BlockSpec((tq, LANES), lambda bi, p, i: (bi * nq + i, p)),
        out_shape=jax.ShapeDtypeStruct((t, MOBA_HEADS * HEAD_DIM), BF16),
        scratch_shapes=[
            pltpu.VMEM((LANES, LANES), F32),
            pltpu.VMEM((tq // tk + 1, tk, cols), F32),
            pltpu.VMEM((LANES, cols), F32),
            pltpu.VMEM((1, cols), F32),
            pltpu.VMEM((1, cols), F32),
            pltpu.VMEM((LANES, cols), F32),
        ],
        compiler_params=pltpu.CompilerParams(
            dimension_semantics=("arbitrary", "arbitrary", "arbitrary"), vmem_limit_bytes=VMEM_LIMIT),
        name="moba",
    )(feat, kmat, feat, tbl)


def _compress_kernel(rk_ref, rv_ref, wk_ref, wv_ref, pos_ref, w2_ref, kn_ref, kc_ref, vc_ref):
    n = rk_ref.shape[0]
    hid = CMP_HIDDEN
    lo = lax.broadcasted_iota(jnp.int32, (n, LANES), 1) < HEAD_DIM
    for idx, (r_ref, w_ref) in enumerate(((rk_ref, wk_ref), (rv_ref, wv_ref))):
        r = r_ref[...]
        first = _dot((r + pos_ref[2 * idx:2 * idx + 1, :]).astype(BF16), w_ref[:, :2 * hid])
        second = _dot((r + pos_ref[2 * idx + 1:2 * idx + 2, :]).astype(BF16), w_ref[:, 2 * hid:])
        h = jax.nn.gelu(first + pltpu.roll(second, shift=n - 1, axis=0))
        for g in range(NSA_KV_GROUPS):
            y = _dot(h[:, g * hid:(g + 1) * hid].astype(BF16), w2_ref[idx])
            if idx == 0:
                kc_ref[g] = _rms(y, kn_ref[...]).astype(kc_ref.dtype)
            else:
                vc_ref[g] = jnp.where(lo, y, 1.0).T.astype(vc_ref.dtype)


def _compress(rk, rv, wk, wv, pos, w2, kn):
    b, n, width = rk.shape
    return pl.pallas_call(
        _compress_kernel,
        grid=(b,),
        in_specs=[
            pl.BlockSpec((None, n, width), lambda i: (i, 0, 0)),
            pl.BlockSpec((None, n, width), lambda i: (i, 0, 0)),
            pl.BlockSpec(wk.shape, lambda i: (0, 0)),
            pl.BlockSpec(wv.shape, lambda i: (0, 0)),
            pl.BlockSpec(pos.shape, lambda i: (0, 0)),
            pl.BlockSpec(w2.shape, lambda i: (0, 0, 0)),
            pl.BlockSpec(kn.shape, lambda i: (0, 0)),
        ],
        out_specs=[
            pl.BlockSpec((None, NSA_KV_GROUPS, n, LANES), lambda i: (i, 0, 0, 0)),
            pl.BlockSpec((None, NSA_KV_GROUPS, LANES, n), lambda i: (i, 0, 0, 0)),
        ],
        out_shape=[
            jax.ShapeDtypeStruct((b, NSA_KV_GROUPS, n, LANES), BF16),
            jax.ShapeDtypeStruct((b, NSA_KV_GROUPS, LANES, n), BF16),
        ],
        compiler_params=pltpu.CompilerParams(
            dimension_semantics=("parallel",), vmem_limit_bytes=VMEM_LIMIT),
        name="nsa_compress",
    )(rk, rv, wk, wv, pos, w2, kn)


def _nsa_kernel(q_ref, kc_ref, vc_ref, ks_ref, vs_ref, kw_ref, vw_ref, gate_ref, tbl_ref, o_ref,
                bdiag_ref, bnear_ref, ov_ref, sel_ref, m_ref, acc_ref, tot_ref, lc_ref):
    tq = q_ref.shape[1]
    nh = NSA_HPG
    cols = nh * tq
    ncmp = kc_ref.shape[0]
    qi = pl.program_id(2)
    q0 = qi * tq

    @pl.when(qi == 0)
    def _():
        j = lax.broadcasted_iota(jnp.int32, (tq, tq), 0)
        i = lax.broadcasted_iota(jnp.int32, (tq, tq), 1)
        bdiag_ref[...] = _bias_tile_t(tbl_ref, nh, i - j)
        bnear_ref[...] = _bias_tile_t(tbl_ref, nh, tq + i - j)
        ss = lax.broadcasted_iota(jnp.int32, (LANES, ncmp), 0) * SLC_BLOCK
        cs = lax.broadcasted_iota(jnp.int32, (LANES, ncmp), 1) * CMP_STRIDE
        ov_ref[...] = jnp.where((cs < ss + SLC_BLOCK) & (cs + CMP_BLOCK > ss), 1.0, 0.0).astype(BF16)

    qs = _stack_heads_t(q_ref[...])

    def gate(branch):
        return jnp.concatenate([gate_ref[branch * nh + h:branch * nh + h + 1, :] for h in range(nh)], axis=1)

    def result():
        acc = acc_ref[...]
        return acc[:HEAD_DIM] / jnp.maximum(acc[HEAD_DIM:], TINY)

    n_i = lax.broadcasted_iota(jnp.int32, (ncmp, cols), 0)
    t_i = lax.broadcasted_iota(jnp.int32, (ncmp, cols), 1) & (tq - 1)
    valid_c = q0 + t_i - (n_i * CMP_STRIDE + CMP_BLOCK - 1) >= 0
    per_tile_c = tq // CMP_STRIDE
    below = _ceil_to(-(-(REL_MAX_DISTANCE + CMP_BLOCK) // CMP_STRIDE), SUBLANES)
    band = below + per_tile_c
    band_start = pl.multiple_of(jnp.maximum(qi * per_tile_c - below, 0), SUBLANES)
    n_b = band_start + lax.broadcasted_iota(jnp.int32, (band, tq), 0)
    dist_b = q0 + lax.broadcasted_iota(jnp.int32, (band, tq), 1) - (n_b * CMP_STRIDE + CMP_BLOCK - 1)
    lc_ref[...] = _dot(kc_ref[...], qs)
    lc_ref[pl.ds(band_start, band), :] += _bias_tile_t(tbl_ref, nh, dist_b)
    logit_c = jnp.where(valid_c, lc_ref[...], NEG)
    p_c = jnp.where(valid_c, jnp.exp2(logit_c - jnp.max(logit_c, axis=0, keepdims=True)), 0.0)
    p_c = p_c / jnp.maximum(jnp.sum(p_c, axis=0, keepdims=True), TINY)
    tot_ref[...] = gate(0) * _dot(vc_ref[...], p_c.astype(BF16))[:HEAD_DIM]

    p_sum = p_c[:, 0:tq]
    for h in range(1, nh):
        p_sum = p_sum + p_c[:, h * tq:(h + 1) * tq]
    imp = _dot_hilo_rhs(ov_ref[...], p_sum)
    blk = lax.broadcasted_iota(jnp.int32, (LANES, tq), 0)
    cur = (q0 + lax.broadcasted_iota(jnp.int32, (LANES, tq), 1)) >> SLC_SHIFT
    forced = (blk == 0) | (blk == cur) | (blk == cur - 1)
    score = jnp.where(blk <= cur, jnp.where(forced, BIG, imp), NEG)
    sel = jnp.where(blk <= cur, _top_n_mask_t(score, SLC_TOPN), 0.0)
    sel_ref[...] = jnp.concatenate([sel] * nh, axis=1)

    j_i = lax.broadcasted_iota(jnp.int32, (tq, cols), 0)
    i_i = lax.broadcasted_iota(jnp.int32, (tq, cols), 1) & (tq - 1)
    causal = j_i <= i_i
    per_tile = tq // SLC_BLOCK

    def block(k_ref, v_ref, kb, bias, mask):
        start = pl.multiple_of(kb * tq, tq)
        logits = _dot(k_ref[pl.ds(start, tq), :], qs)
        if bias is not None:
            logits = logits + bias
        _softmax_update(m_ref, None, acc_ref, logits, mask, v_ref[:, pl.ds(start, tq)])

    def chosen(kb):
        parts = [jnp.broadcast_to(sel_ref[pl.ds(kb * per_tile + r, 1), :], (SLC_BLOCK, cols)) for r in range(per_tile)]
        return jnp.concatenate(parts, axis=0) > 0.5

    _softmax_init(m_ref, None, acc_ref)

    def sel_body(kb, carry):
        block(ks_ref, vs_ref, kb, None, chosen(kb))
        return carry

    lax.fori_loop(0, jnp.maximum(qi - 1, 0), sel_body, 0)

    @pl.when(qi >= 1)
    def _():
        block(ks_ref, vs_ref, qi - 1, bnear_ref[...], chosen(qi - 1))

    block(ks_ref, vs_ref, qi, bdiag_ref[...], chosen(qi) & causal)
    tot_ref[...] += gate(1) * result()

    _softmax_init(m_ref, None, acc_ref)
    nwin = WINDOW // tq
    for r in range(nwin, 1, -1):
        @pl.when(qi >= r)
        def _(r=r):
            block(kw_ref, vw_ref, qi - r, None, (j_i > i_i) if r == nwin else None)

    @pl.when(qi >= 1)
    def _():
        block(kw_ref, vw_ref, qi - 1, bnear_ref[...], None)

    block(kw_ref, vw_ref, qi, bdiag_ref[...], causal)
    tot = tot_ref[...] + gate(2) * result()
    pairs = [jnp.concatenate([tot[:, (2 * c) * tq:(2 * c + 1) * tq], tot[:, (2 * c + 1) * tq:(2 * c + 2) * tq]], axis=0).T
             for c in range(nh // 2)]
    o_ref[...] = jnp.concatenate(pairs, axis=1).astype(o_ref.dtype)


def _nsa_attention(kmat, feat, kc, vc, gates_t, tbl, b, s, tq):
    t = b * s
    nq = s // tq
    ncmp = kc.shape[2]
    cols = NSA_HPG * tq
    k_spec = lambda blk: pl.BlockSpec((s, LANES), lambda bi, g, i: (bi, blk + g))
    v_spec = lambda blk: pl.BlockSpec((LANES, s), lambda bi, g, i: (blk + g, bi))
    return pl.pallas_call(
        _nsa_kernel,
        grid=(b, NSA_KV_GROUPS, nq),
        in_specs=[
            pl.BlockSpec((2 * LANES, tq), lambda bi, g, i: (F_NS_Q // 2 + g, bi * nq + i)),
            pl.BlockSpec((None, None, ncmp, LANES), lambda bi, g, i: (bi, g, 0, 0)),
            pl.BlockSpec((None, None, LANES, ncmp), lambda bi, g, i: (bi, g, 0, 0)),
            k_spec(K_NS_SLC), v_spec(F_NS_SLC_V), k_spec(K_NS_WIN), v_spec(F_NS_WIN_V),
            pl.BlockSpec((LANES, tq), lambda bi, g, i: (g, bi * nq + i)),
            pl.BlockSpec((None, SUBLANES, LANES), lambda bi, g, i: (g, 0, 0)),
        ],
        out_specs=pl.BlockSpec((tq, 2 * LANES), lambda bi, g, i: (bi * nq + i, g)),
        out_shape=jax.ShapeDtypeStruct((t, NSA_HEADS * HEAD_DIM), BF16),
        scratch_shapes=[
            pltpu.VMEM((tq, cols), F32),
            pltpu.VMEM((tq, cols), F32),
            pltpu.VMEM((LANES, ncmp), BF16),
            pltpu.VMEM((LANES, cols), F32),
            pltpu.VMEM((1, cols), F32),
            pltpu.VMEM((LANES, cols), F32),
            pltpu.VMEM((HEAD_DIM, cols), F32),
            pltpu.VMEM((ncmp, cols), F32),
        ],
        compiler_params=pltpu.CompilerParams(
            dimension_semantics=("arbitrary", "arbitrary", "arbitrary"), vmem_limit_bytes=VMEM_LIMIT),
        name="nsa",
    )(feat, kc, vc, kmat, feat, kmat, feat, gates_t, tbl)


def _merge_kernel(x_ref, g_ref, oa_ref, ob_ref, oc_ref, wg_ref, wb_ref, wo_ref, o_ref):
    x = x_ref[...]
    d = x.shape[1]
    h = _rms(x, g_ref[...]).astype(BF16)
    mix = None
    row = 0
    for br, src in enumerate((oa_ref, ob_ref, oc_ref)):
        width = src.shape[1]
        gate = jax.nn.sigmoid(_dot(h, wg_ref[:, br * d:(br + 1) * d]))
        term = gate * _dot(src[...], wb_ref[row:row + width, :])
        mix = term if mix is None else mix + term
        row += width
    o_ref[...] = x + _dot(mix.astype(BF16), wo_ref[...])


def _merge(x2, g, oa, ob, oc, wg, wb, wo, tm):
    t, d = x2.shape
    row = lambda w: pl.BlockSpec((tm, w), lambda i: (i, 0))
    full = lambda a: pl.BlockSpec(a.shape, lambda i: (0, 0))
    return pl.pallas_call(
        _merge_kernel,
        grid=(t // tm,),
        in_specs=[row(d), full(g), row(oa.shape[1]), row(ob.shape[1]), row(oc.shape[1]),
                  full(wg), full(wb), full(wo)],
        out_specs=row(d),
        out_shape=jax.ShapeDtypeStruct((t, d), F32),
        compiler_params=pltpu.CompilerParams(
            dimension_semantics=("parallel",), vmem_limit_bytes=VMEM_LIMIT),
        name="merge",
    )(x2, g, oa, ob, oc, wg, wb, wo)


def _ffn_kernel(x_ref, g_ref, wgu_ref, wd_ref, o_ref, *, n_chunks):
    x = x_ref[...]
    d_ff = wd_ref.shape[0]
    ch = d_ff // n_chunks
    h = _rms(x, g_ref[...]).astype(BF16)
    out = x
    for c in range(n_chunks):
        gate = _dot(h, wgu_ref[:, c * ch:(c + 1) * ch])
        up = _dot(h, wgu_ref[:, d_ff + c * ch:d_ff + (c + 1) * ch])
        out = out + _dot((jax.nn.silu(gate) * up).astype(BF16), wd_ref[c * ch:(c + 1) * ch, :])
    o_ref[...] = out


def _ffn(x2, g, wgu, wd, tm):
    t, d = x2.shape
    d_ff = wd.shape[0]
    n_chunks = 2 if d_ff % (2 * LANES) == 0 else 1
    return pl.pallas_call(
        functools.partial(_ffn_kernel, n_chunks=n_chunks),
        grid=(t // tm,),
        in_specs=[
            pl.BlockSpec((tm, d), lambda i: (i, 0)),
            pl.BlockSpec((1, d), lambda i: (0, 0)),
            pl.BlockSpec(wgu.shape, lambda i: (0, 0)),
            pl.BlockSpec(wd.shape, lambda i: (0, 0)),
        ],
        out_specs=pl.BlockSpec((tm, d), lambda i: (i, 0)),
        out_shape=jax.ShapeDtypeStruct((t, d), F32),
        compiler_params=pltpu.CompilerParams(
            dimension_semantics=("parallel",), vmem_limit_bytes=VMEM_LIMIT),
        name="swiglu",
    )(x2, g, wgu, wd)


def _dup_groups(w):
    d = w.shape[0]
    w = w.reshape(d, NSA_KV_GROUPS, 1, HEAD_DIM)
    return jnp.broadcast_to(w, (d, NSA_KV_GROUPS, 2, HEAD_DIM)).reshape(d, NSA_KV_GROUPS * PAIR)


def _half_groups(w):
    d = w.shape[0]
    w = w.reshape(d, NSA_KV_GROUPS, HEAD_DIM)
    return jnp.pad(w, ((0, 0), (0, 0), (0, HEAD_DIM))).reshape(d, NSA_KV_GROUPS * PAIR)


def _pair_gain(g, scale=1.0):
    return jnp.concatenate([g, g]) * scale


def _layer_params(w_in, moba_q_norm, moba_k_norm, nsa_q_norm, nsa_k_norm, nsa_cmp_pos, nsa_cmp_w1, nsa_cmp_w2):
    d = w_in.shape[0]
    sbw, mbw, nsw, kvw = SB_HEADS * HEAD_DIM, MOBA_HEADS * HEAD_DIM, NSA_HEADS * HEAD_DIM, NSA_KV_GROUPS * HEAD_DIM
    sb_q, sb_k, sb_v = (w_in[:, i * sbw:(i + 1) * sbw] for i in range(3))
    o = 3 * sbw
    mb_q, mb_k, mb_v = (w_in[:, o + i * mbw:o + (i + 1) * mbw] for i in range(3))
    o += 3 * mbw
    ns_q = w_in[:, o:o + nsw]
    o += nsw
    kc_w, vc_w, ks_w, vs_w, kw_w, vw_w = (w_in[:, o + i * kvw:o + (i + 1) * kvw] for i in range(6))
    o += 6 * kvw
    gate_w = w_in[:, o:o + N_BRANCHES * NSA_HEADS].reshape(d, N_BRANCHES, NSA_KV_GROUPS, NSA_HPG)
    o += N_BRANCHES * NSA_HEADS
    wg = w_in[:, o:].astype(BF16)
    w = jnp.concatenate([sb_k, mb_k, _dup_groups(ks_w), _dup_groups(kw_w),
                         sb_q, mb_q, ns_q, sb_v, mb_v, _half_groups(vs_w), _half_groups(vw_w)], axis=1).astype(BF16)
    gate_cols = []
    for g in range(NSA_KV_GROUPS):
        cols = gate_w[:, :, g, :].reshape(d, N_BRANCHES * NSA_HPG)
        gate_cols.append(jnp.pad(cols, ((0, 0), (0, LANES - N_BRANCHES * NSA_HPG))))
    wa = jnp.concatenate([kc_w, vc_w] + gate_cols, axis=1).astype(BF16)
    scale = HEAD_DIM ** -0.5
    ones = jnp.ones((LANES,), F32)
    gains = [ones] * 2 + [_pair_gain(moba_k_norm)] * 2 + [_pair_gain(nsa_k_norm[1])] * 2 + [_pair_gain(nsa_k_norm[2])] * 2
    gains += [ones * scale] * 2 + [_pair_gain(moba_q_norm, scale * LOG2E)] * 2
    gains += [_pair_gain(nsa_q_norm, scale * LOG2E)] * 4 + [ones] * 8
    gain = jnp.concatenate(gains).reshape(1, (N_K_BLOCKS + N_F_BLOCKS) * LANES)

    half = CMP_BLOCK // 2
    hid = CMP_HIDDEN

    def spread(w1):
        w1 = w1.reshape(2, half, HEAD_DIM, hid)
        cols = []
        for part in range(2):
            for g in range(NSA_KV_GROUPS):
                z = jnp.zeros((half, NSA_KV_GROUPS, HEAD_DIM, hid), F32).at[:, g].set(w1[part])
                cols.append(z.reshape(half * kvw, hid))
        return jnp.concatenate(cols, axis=1).astype(BF16)

    def spread_pos(p):
        p = p.reshape(2, half, 1, HEAD_DIM)
        return jnp.broadcast_to(p, (2, half, NSA_KV_GROUPS, HEAD_DIM)).reshape(2, half * kvw)

    pos = jnp.concatenate([spread_pos(nsa_cmp_pos[0]), spread_pos(nsa_cmp_pos[1])], axis=0)
    w2 = jnp.stack([jnp.concatenate([nsa_cmp_w2[0], nsa_cmp_w2[0]], axis=-1),
                    jnp.pad(nsa_cmp_w2[1], ((0, 0), (0, HEAD_DIM)))]).astype(BF16)
    kn = _pair_gain(nsa_k_norm[0]).reshape(1, LANES)
    return dict(w=w, wa=wa, gain=gain, wg=wg, wk=spread(nsa_cmp_w1[0]), wv=spread(nsa_cmp_w1[1]),
                pos=pos, w2=w2, kn=kn)


def _bias_tables(rel_bias):
    tbl = jnp.pad(rel_bias.T, ((0, 0), (0, LANES - N_BUCKETS)))

    def rows(t, n):
        t = t.reshape(n, -1, LANES)
        return jnp.pad(t, ((0, 0), (0, SUBLANES - t.shape[1]), (0, 0)))

    return rows(tbl[:MOBA_HEADS], MOBA_HEADS // 2), rows(tbl[MOBA_HEADS:], NSA_KV_GROUPS)


def kernel(x, rel_bias, attn_norm, w_in, moba_q_norm, moba_k_norm, nsa_q_norm, nsa_k_norm, nsa_cmp_pos,
           nsa_cmp_w1, nsa_cmp_w2, w_branch, w_out, ffn_norm, w_gate_up, w_down):
    b, s, d = x.shape
    t = b * s
    tm = min(512, t)
    tq_pair = 512
    tq_nsa = 256
    tk_sb = 256
    assert s % tq_pair == 0 and tq_pair % MOBA_BLOCK == 0 and tq_pair % tk_sb == 0 and WINDOW % tq_nsa == 0
    tbl_moba, tbl_nsa = _bias_tables(rel_bias)
    x2 = x.reshape(t, d)
    for layer in range(w_in.shape[0]):
        p = _layer_params(w_in[layer], moba_q_norm[layer], moba_k_norm[layer], nsa_q_norm[layer],
                          nsa_k_norm[layer], nsa_cmp_pos[layer], nsa_cmp_w1[layer], nsa_cmp_w2[layer])
        g_attn = attn_norm[layer].reshape(1, d)
        kmat, feat, kc_in, vc_in, gates_t = _inproj(x2, g_attn, p["w"], p["wa"], p["gain"], tm)
        o_a = _sb_attention(kmat, feat, b, s, tq_pair, tk_sb)
        o_b = _moba_attention(kmat, feat, tbl_moba, b, s, tq_pair)
        chunk = CMP_STRIDE * LANES
        kc, vc = _compress(kc_in.reshape(b, s // CMP_STRIDE, chunk), vc_in.reshape(b, s // CMP_STRIDE, chunk),
                           p["wk"], p["wv"], p["pos"], p["w2"], p["kn"])
        o_c = _nsa_attention(kmat, feat, kc, vc, gates_t, tbl_nsa, b, s, tq_nsa)
        x2 = _merge(x2, g_attn, o_a, o_b, o_c, p["wg"], w_branch[layer].astype(BF16),
                    w_out[layer].astype(BF16), tm)
        x2 = _ffn(x2, ffn_norm[layer].reshape(1, d), w_gate_up[layer].astype(BF16),
                  w_down[layer].astype(BF16), tm)
    return x2.reshape(b, s, d)
```

```python
import functools
import math

import jax
import jax.numpy as jnp
from jax import lax
from jax.experimental import pallas as pl
from jax.experimental.pallas import tpu as pltpu

HEAD_DIM = 64
SB_HEADS = 4
MOBA_HEADS = 4
NSA_HEADS = 8
NSA_KV_GROUPS = 2
NSA_HPG = NSA_HEADS // NSA_KV_GROUPS
N_BRANCHES = 3
MOBA_BLOCK = 256
MOBA_TOPK = 3
CMP_BLOCK = 32
CMP_STRIDE = 16
CMP_HIDDEN = 4 * HEAD_DIM
SLC_BLOCK = 64
SLC_TOPN = 16
WINDOW = 512
N_BUCKETS = 32
REL_MAX_DISTANCE = 128
NORM_EPS = 1e-6
NEG = -1e30
BIG = 1e30
TINY = 1e-30
LOG2E = math.log2(math.e)
MOBA_FAR_STATES = 2
NSA_SEL_STATES = 4
SB_UNDERFLOW_LOG2 = 160.0

LANES = 128
SUBLANES = 8
PAIR = 2 * HEAD_DIM
VMEM_LIMIT = 56 * 1024 * 1024

F32 = jnp.float32
BF16 = jnp.bfloat16

K_SB, K_MB, K_NS_SLC, K_NS_WIN = 0, 2, 4, 6
N_K_BLOCKS = 8
F_SB_Q, F_MB_Q, F_NS_Q, F_SB_V, F_MB_V, F_NS_SLC_V, F_NS_WIN_V = 0, 2, 4, 8, 10, 12, 14
N_F_BLOCKS = 16
K_MODES = (0, 0, 2, 2, 2, 2, 2, 2)
F_MODES = (1, 1, 2, 2, 2, 2, 2, 2, 0, 0, 0, 0, 3, 3, 3, 3)


SLC_SHIFT = SLC_BLOCK.bit_length() - 1
MOBA_SHIFT = MOBA_BLOCK.bit_length() - 1
assert 1 << SLC_SHIFT == SLC_BLOCK and 1 << MOBA_SHIFT == MOBA_BLOCK


def _ceil_to(n, m):
    return -(-n // m) * m


def _dot(a, b):
    return jnp.dot(a, b, preferred_element_type=F32)


def _dot_hilo_rhs(a, b):
    hi = b.astype(BF16)
    lo = (b - hi.astype(F32)).astype(BF16)
    return _dot(a, hi) + _dot(a, lo)


def _rms(x, g):
    ms = jnp.mean(x * x, axis=-1, keepdims=True)
    return x * lax.rsqrt(ms + NORM_EPS) * g


def _stack_heads_t(q_t):
    tq = q_t.shape[1]
    lo = lax.broadcasted_iota(jnp.int32, (LANES, tq), 0) < HEAD_DIM
    zero = jnp.zeros((LANES, tq), q_t.dtype)
    parts = []
    for c in range(q_t.shape[0] // LANES):
        blk = q_t[c * LANES:(c + 1) * LANES, :]
        parts.append(jnp.where(lo, blk, zero))
        parts.append(jnp.where(lo, zero, blk))
    return jnp.concatenate(parts, axis=1)


def _t5_bucket(dist):
    n = jnp.maximum(dist, 0)
    max_exact = N_BUCKETS // 2
    nf = jnp.maximum(n, 1).astype(F32)
    large = max_exact + (jnp.log(nf / max_exact) / math.log(REL_MAX_DISTANCE / max_exact)
                         * (N_BUCKETS - max_exact)).astype(jnp.int32)
    large = jnp.minimum(large, N_BUCKETS - 1)
    return jnp.where(n < max_exact, n, large)


def _bias_tile_t(tbl_ref, n_heads, dist):
    nk, tq = dist.shape
    bucket = _t5_bucket(dist)
    cols = []
    for h in range(n_heads):
        t = jnp.broadcast_to(tbl_ref[h:h + 1, :], (nk, LANES))
        far = t[:, N_BUCKETS - 1:N_BUCKETS]
        for c in range(tq // LANES):
            cols.append((jnp.take_along_axis(t, bucket[:, c * LANES:(c + 1) * LANES], axis=1) - far) * LOG2E)
    return jnp.concatenate(cols, axis=1)


def _top_n_mask_t(score, n):
    row = lax.broadcasted_iota(jnp.int32, score.shape, 0).astype(F32)
    sel = jnp.zeros(score.shape, F32)
    s = score
    for _ in range(n):
        m = jnp.max(s, axis=0, keepdims=True)
        idx = jnp.min(jnp.where(s == m, row, float(score.shape[0])), axis=0, keepdims=True)
        pick = row == idx
        sel = jnp.where(pick, 1.0, sel)
        s = jnp.where(pick, -jnp.inf, s)
    return sel


def _post(yb, lo, mode, gain):
    if mode == 2:
        sq = yb * yb
        s_lo = jnp.sum(jnp.where(lo, sq, 0.0), axis=-1, keepdims=True)
        s_hi = jnp.sum(jnp.where(lo, 0.0, sq), axis=-1, keepdims=True)
        yb = yb * lax.rsqrt(jnp.where(lo, s_lo, s_hi) * (1.0 / HEAD_DIM) + NORM_EPS)
    if mode in (1, 2):
        yb = yb * gain
    return yb


def _inproj_kernel(x_ref, g_ref, w_ref, wa_ref, gain_ref, k_ref, f_ref, kc_ref, vc_ref, gate_ref):
    tm = x_ref.shape[0]
    h = _rms(x_ref[...], g_ref[...]).astype(BF16)
    lo = lax.broadcasted_iota(jnp.int32, (tm, LANES), 1) < HEAD_DIM
    for c in range((N_K_BLOCKS + N_F_BLOCKS) // 2):
        y = _dot(h, w_ref[:, c * 2 * LANES:(c + 1) * 2 * LANES])
        for s in range(2):
            blk = 2 * c + s
            gain = gain_ref[:, blk * LANES:(blk + 1) * LANES]
            yb = y[:, s * LANES:(s + 1) * LANES]
            if blk < N_K_BLOCKS:
                k_ref[:, blk * LANES:(blk + 1) * LANES] = _post(yb, lo, K_MODES[blk], gain).astype(BF16)
            else:
                fb = blk - N_K_BLOCKS
                yb = jnp.where(lo, yb, 1.0) if F_MODES[fb] == 3 else _post(yb, lo, F_MODES[fb], gain)
                f_ref[fb * LANES:(fb + 1) * LANES, :] = yb.T.astype(BF16)
    aux = _dot(h, wa_ref[...])
    kc_ref[...] = aux[:, 0:LANES]
    vc_ref[...] = aux[:, LANES:2 * LANES]
    for g in range(NSA_KV_GROUPS):
        gate_ref[g * LANES:(g + 1) * LANES, :] = jax.nn.sigmoid(aux[:, (2 + g) * LANES:(3 + g) * LANES]).T


def _inproj(x2, g, w, wa, gain, tm):
    t, d = x2.shape
    nw = w.shape[1]
    return pl.pallas_call(
        _inproj_kernel,
        grid=(t // tm,),
        in_specs=[
            pl.BlockSpec((tm, d), lambda i: (i, 0)),
            pl.BlockSpec((1, d), lambda i: (0, 0)),
            pl.BlockSpec((d, nw), lambda i: (0, 0)),
            pl.BlockSpec((d, 4 * LANES), lambda i: (0, 0)),
            pl.BlockSpec((1, nw), lambda i: (0, 0)),
        ],
        out_specs=[
            pl.BlockSpec((tm, N_K_BLOCKS * LANES), lambda i: (i, 0)),
            pl.BlockSpec((N_F_BLOCKS * LANES, tm), lambda i: (0, i)),
            pl.BlockSpec((tm, LANES), lambda i: (i, 0)),
            pl.BlockSpec((tm, LANES), lambda i: (i, 0)),
            pl.BlockSpec((NSA_KV_GROUPS * LANES, tm), lambda i: (0, i)),
        ],
        out_shape=[
            jax.ShapeDtypeStruct((t, N_K_BLOCKS * LANES), BF16),
            jax.ShapeDtypeStruct((N_F_BLOCKS * LANES, t), BF16),
            jax.ShapeDtypeStruct((t, LANES), F32),
            jax.ShapeDtypeStruct((t, LANES), F32),
            jax.ShapeDtypeStruct((NSA_KV_GROUPS * LANES, t), F32),
        ],
        compiler_params=pltpu.CompilerParams(
            dimension_semantics=("parallel",), vmem_limit_bytes=VMEM_LIMIT),
        name="inproj",
    )(x2, g, w, wa, gain)


def _sb_kernel(q_ref, k_ref, v_ref, o_ref, r_ref, acc_ref, *, tk):
    tq = q_ref.shape[1]
    cols = 2 * tq
    nsub = tq // tk
    qi = pl.program_id(2)
    qs = _stack_heads_t(q_ref[...])
    j_i = lax.broadcasted_iota(jnp.int32, (tk, cols), 0)
    t_i = lax.broadcasted_iota(jnp.int32, (tk, cols), 1) & (tq - 1)
    after = jnp.where((lax.broadcasted_iota(jnp.int32, (tk, 2 * tk), 1) & (tk - 1))
                      > lax.broadcasted_iota(jnp.int32, (tk, 2 * tk), 0), 1.0, 0.0).astype(BF16)
    r_ref[...] = jnp.zeros(r_ref.shape, F32)
    acc_ref[...] = jnp.zeros(acc_ref.shape, F32)

    def block(kb, strict):
        start = pl.multiple_of(kb * tk, tk)
        z = _dot(k_ref[pl.ds(start, tk), :], qs)
        sp = jnp.maximum(z, 0.0) + jnp.log2(1.0 + jnp.exp2(-jnp.abs(z)))
        stay = sp if strict is None else jnp.where(strict, sp, 0.0)
        hi = stay.astype(BF16)
        lo = (stay - hi.astype(F32)).astype(BF16)
        after_sum = _dot(after, jnp.concatenate([hi, lo], axis=0))
        a = jnp.exp2(z - sp - after_sum - r_ref[...])
        if strict is not None:
            a = jnp.where(strict, a, 0.0)
        acc_ref[...] += _dot(v_ref[:, pl.ds(start, tk)], a.astype(BF16))
        r_ref[...] += jnp.sum(stay, axis=0, keepdims=True)

    for u in range(nsub - 1, -1, -1):
        block(qi * nsub + u, u * tk + j_i < t_i)

    def more(carry):
        j, r_min = carry
        return (j < qi * nsub) & (r_min < SB_UNDERFLOW_LOG2)

    def body(carry):
        j, _ = carry
        block(qi * nsub - 1 - j, None)
        return j + 1, jnp.min(r_ref[...])

    lax.while_loop(more, body, (0, jnp.min(r_ref[...])))
    lo = lax.broadcasted_iota(jnp.int32, (LANES, tq), 0) < HEAD_DIM
    acc = acc_ref[...]
    o_ref[...] = jnp.where(lo, acc[:, :tq], acc[:, tq:]).T.astype(o_ref.dtype)


def _sb_attention(kmat, feat, b, s, tq, tk):
    t = b * s
    nq = s // tq
    return pl.pallas_call(
        functools.partial(_sb_kernel, tk=tk),
        grid=(b, SB_HEADS // 2, nq),
        in_specs=[
            pl.BlockSpec((LANES, tq), lambda bi, p, i: (F_SB_Q + p, bi * nq + i)),
            pl.BlockSpec((s, LANES), lambda bi, p, i: (bi, K_SB + p)),
            pl.BlockSpec((LANES, s), lambda bi, p, i: (F_SB_V + p, bi)),
        ],
        out_specs=pl.BlockSpec((tq, LANES), lambda bi, p, i: (bi * nq + i, p)),
        out_shape=jax.ShapeDtypeStruct((t, SB_HEADS * HEAD_DIM), BF16),
        scratch_shapes=[pltpu.VMEM((1, 2 * tq), F32), pltpu.VMEM((LANES, 2 * tq), F32)],
        compiler_params=pltpu.CompilerParams(
            dimension_semantics=("parallel", "parallel", "arbitrary"), vmem_limit_bytes=VMEM_LIMIT),
        name="stickbreak",
    )(feat, kmat, feat)


def _softmax_init(m_ref, l_ref, acc_ref):
    m_ref[...] = jnp.full(m_ref.shape, NEG, F32)
    if l_ref is not None:
        l_ref[...] = jnp.zeros(l_ref.shape, F32)
    acc_ref[...] = jnp.zeros(acc_ref.shape, F32)


def _softmax_update(m_ref, l_ref, acc_ref, s, logits, mask, v_t):
    lm = logits if mask is None else jnp.where(mask, logits, -jnp.inf)
    m_old = m_ref[s]
    m_new = jnp.maximum(m_old, jnp.max(lm, axis=0, keepdims=True))
    p = jnp.exp2(lm - m_new)
    alpha = jnp.exp2(m_old - m_new)
    if l_ref is not None:
        l_ref[s] = alpha * l_ref[s] + jnp.sum(p, axis=0, keepdims=True)
    acc_ref[s] = alpha * acc_ref[s] + _dot(v_t, p.astype(BF16))
    m_ref[s] = m_new


def _softmax_merge(m_ref, l_ref, acc_ref, states):
    m = m_ref[states[0]]
    for s in states[1:]:
        m = jnp.maximum(m, m_ref[s])
    acc = l = None
    for s in states:
        w = jnp.exp2(m_ref[s] - m)
        acc = w * acc_ref[s] if acc is None else acc + w * acc_ref[s]
        if l_ref is not None:
            l = w * l_ref[s] if l is None else l + w * l_ref[s]
    return acc, l


def _moba_kernel(q_ref, k_ref, v_ref, tbl_ref, o_ref, km_ref, bias_ref, sel_ref, m_ref, l_ref, acc_ref, *, nblk):
    tk = MOBA_BLOCK
    tq = q_ref.shape[1]
    nsub = tq // tk
    cols = 2 * tq
    qi = pl.program_id(2)

    @pl.when(qi == 0)
    def _():
        km_ref[...] = jnp.zeros(km_ref.shape, F32)

        def mean_body(n, carry):
            kb = k_ref[pl.ds(pl.multiple_of(n * tk, tk), tk), :].astype(F32)
            km_ref[pl.ds(n, 1), :] = jnp.mean(kb, axis=0, keepdims=True)
            return carry

        lax.fori_loop(0, nblk, mean_body, 0)
        j = lax.broadcasted_iota(jnp.int32, (tk, tq), 0)
        i = lax.broadcasted_iota(jnp.int32, (tk, tq), 1)
        for e in range(-1, nsub):
            bias_ref[e + 1] = _bias_tile_t(tbl_ref, 2, i - tk * e - j)

    qs = _stack_heads_t(q_ref[...])
    sub_s = lax.broadcasted_iota(jnp.int32, (LANES, cols), 1) & (tq - 1)
    past = lax.broadcasted_iota(jnp.int32, (LANES, cols), 0) < nsub * qi + (sub_s >> MOBA_SHIFT)
    score = jnp.where(past, _dot(km_ref[...].astype(BF16), qs), NEG)
    sel_ref[...] = jnp.where(past, _top_n_mask_t(score, MOBA_TOPK), 0.0)
    _softmax_init(m_ref, l_ref, acc_ref)

    def tiles(*specs):
        ready = []
        for state, n, bias, mask in specs:
            start = pl.multiple_of(n * tk, tk)
            logits = _dot(k_ref[pl.ds(start, tk), :], qs)
            ready.append((state, logits if bias is None else logits + bias, mask, start))
        for state, logits, mask, start in ready:
            _softmax_update(m_ref, l_ref, acc_ref, state, logits, mask, v_ref[:, pl.ds(start, tk)])

    def chosen(n, live=None):
        row = sel_ref[pl.ds(n, 1), :]
        if live is not None:
            row = row * live.astype(F32)
        return jnp.broadcast_to(row, (tk, cols)) > 0.5

    n_far = jnp.maximum(nsub * qi - 1, 0)

    def far_body(i, carry):
        second = jnp.minimum(2 * i + 1, n_far - 1)
        tiles((0, 2 * i, None, chosen(2 * i)), (1, second, None, chosen(second, live=2 * i + 1 < n_far)))
        return carry

    lax.fori_loop(0, (n_far + 1) >> 1, far_body, 0)
    prev = jnp.maximum(nsub * qi - 1, 0)
    j_i = lax.broadcasted_iota(jnp.int32, (tk, cols), 0)
    t_i = lax.broadcasted_iota(jnp.int32, (tk, cols), 1) & (tq - 1)
    own = [(MOBA_FAR_STATES + 1 + e, nsub * qi + e, bias_ref[e + 1],
            (((t_i >> MOBA_SHIFT) == e) & (j_i <= t_i - tk * e)) | chosen(nsub * qi + e)) for e in range(nsub)]
    tiles((MOBA_FAR_STATES, prev, bias_ref[0], chosen(prev, live=qi >= 1)), *own)
    acc, l = _softmax_merge(m_ref, l_ref, acc_ref, list(range(MOBA_FAR_STATES + 1 + nsub)))
    out = acc / jnp.maximum(l, TINY)
    lo = lax.broadcasted_iota(jnp.int32, (LANES, tq), 0) < HEAD_DIM
    o_ref[...] = jnp.where(lo, out[:, :tq], out[:, tq:]).T.astype(o_ref.dtype)


def _moba_attention(kmat, feat, tbl, b, s, tq):
    t = b * s
    tk = MOBA_BLOCK
    nq = s // tq
    cols = 2 * tq
    n_states = MOBA_FAR_STATES + 1 + tq // tk
    return pl.pallas_call(
        functools.partial(_moba_kernel, nblk=s // tk),
        grid=(b, MOBA_HEADS // 2, nq),
        in_specs=[
            pl.BlockSpec((LANES, tq), lambda bi, p, i: (F_MB_Q + p, bi * nq + i)),
            pl.BlockSpec((s, LANES), lambda bi, p, i: (bi, K_MB + p)),
            pl.BlockSpec((LANES, s), lambda bi, p, i: (F_MB_V + p, bi)),
            pl.BlockSpec((None, SUBLANES, LANES), lambda bi, p, i: (p, 0, 0)),
        ],
        out_specs=pl.BlockSpec((tq, LANES), lambda bi, p, i: (bi * nq + i, p)),
        out_shape=jax.ShapeDtypeStruct((t, MOBA_HEADS * HEAD_DIM), BF16),
        scratch_shapes=[
            pltpu.VMEM((LANES, LANES), F32),
            pltpu.VMEM((tq // tk + 1, tk, cols), F32),
            pltpu.VMEM((LANES, cols), F32),
            pltpu.VMEM((n_states, 1, cols), F32),
            pltpu.VMEM((n_states, 1, cols), F32),
            pltpu.VMEM((n_states, LANES, cols), F32),
        ],
        compiler_params=pltpu.CompilerParams(
            dimension_semantics=("arbitrary", "arbitrary", "arbitrary"), vmem_limit_bytes=VMEM_LIMIT),
        name="moba",
    )(feat, kmat, feat, tbl)


def _compress_kernel(rk_ref, rv_ref, wk_ref, wv_ref, pos_ref, w2_ref, kn_ref, kc_ref, vc_ref):
    n = rk_ref.shape[0]
    hid = CMP_HIDDEN
    lo = lax.broadcasted_iota(jnp.int32, (n, LANES), 1) < HEAD_DIM
    for idx, (r_ref, w_ref) in enumerate(((rk_ref, wk_ref), (rv_ref, wv_ref))):
        r = r_ref[...]
        first = _dot((r + pos_ref[2 * idx:2 * idx + 1, :]).astype(BF16), w_ref[:, :2 * hid])
        second = _dot((r + pos_ref[2 * idx + 1:2 * idx + 2, :]).astype(BF16), w_ref[:, 2 * hid:])
        h = jax.nn.gelu(first + pltpu.roll(second, shift=n - 1, axis=0))
        for g in range(NSA_KV_GROUPS):
            y = _dot(h[:, g * hid:(g + 1) * hid].astype(BF16), w2_ref[idx])
            if idx == 0:
                kc_ref[g] = _rms(y, kn_ref[...]).astype(kc_ref.dtype)
            else:
                vc_ref[g] = jnp.where(lo, y, 1.0).T.astype(vc_ref.dtype)


def _compress(rk, rv, wk, wv, pos, w2, kn):
    b, n, width = rk.shape
    return pl.pallas_call(
        _compress_kernel,
        grid=(b,),
        in_specs=[
            pl.BlockSpec((None, n, width), lambda i: (i, 0, 0)),
            pl.BlockSpec((None, n, width), lambda i: (i, 0, 0)),
            pl.BlockSpec(wk.shape, lambda i: (0, 0)),
            pl.BlockSpec(wv.shape, lambda i: (0, 0)),
            pl.BlockSpec(pos.shape, lambda i: (0, 0)),
            pl.BlockSpec(w2.shape, lambda i: (0, 0, 0)),
            pl.BlockSpec(kn.shape, lambda i: (0, 0)),
        ],
        out_specs=[
            pl.BlockSpec((None, NSA_KV_GROUPS, n, LANES), lambda i: (i, 0, 0, 0)),
            pl.BlockSpec((None, NSA_KV_GROUPS, LANES, n), lambda i: (i, 0, 0, 0)),
        ],
        out_shape=[
            jax.ShapeDtypeStruct((b, NSA_KV_GROUPS, n, LANES), BF16),
            jax.ShapeDtypeStruct((b, NSA_KV_GROUPS, LANES, n), BF16),
        ],
        compiler_params=pltpu.CompilerParams(
            dimension_semantics=("parallel",), vmem_limit_bytes=VMEM_LIMIT),
        name="nsa_compress",
    )(rk, rv, wk, wv, pos, w2, kn)


def _nsa_kernel(q_ref, kc_ref, vc_ref, ks_ref, vs_ref, kw_ref, vw_ref, gate_ref, tbl_ref, o_ref,
                bdiag_ref, bnear_ref, ov_ref, sel_ref, m_ref, acc_ref, tot_ref, lc_ref):
    tq = q_ref.shape[1]
    nh = NSA_HPG
    cols = nh * tq
    ncmp = kc_ref.shape[0]
    qi = pl.program_id(2)
    q0 = qi * tq

    @pl.when(qi == 0)
    def _():
        j = lax.broadcasted_iota(jnp.int32, (tq, tq), 0)
        i = lax.broadcasted_iota(jnp.int32, (tq, tq), 1)
        bdiag_ref[...] = _bias_tile_t(tbl_ref, nh, i - j)
        bnear_ref[...] = _bias_tile_t(tbl_ref, nh, tq + i - j)
        ss = lax.broadcasted_iota(jnp.int32, (LANES, ncmp), 0) * SLC_BLOCK
        cs = lax.broadcasted_iota(jnp.int32, (LANES, ncmp), 1) * CMP_STRIDE
        ov_ref[...] = jnp.where((cs < ss + SLC_BLOCK) & (cs + CMP_BLOCK > ss), 1.0, 0.0).astype(BF16)

    qs = _stack_heads_t(q_ref[...])

    def gate(branch):
        return jnp.concatenate([gate_ref[branch * nh + h:branch * nh + h + 1, :] for h in range(nh)], axis=1)

    def result(states):
        acc, _ = _softmax_merge(m_ref, None, acc_ref, states)
        return acc[:HEAD_DIM] / jnp.maximum(acc[HEAD_DIM:], TINY)

    n_i = lax.broadcasted_iota(jnp.int32, (ncmp, cols), 0)
    t_i = lax.broadcasted_iota(jnp.int32, (ncmp, cols), 1) & (tq - 1)
    valid_c = q0 + t_i - (n_i * CMP_STRIDE + CMP_BLOCK - 1) >= 0
    per_tile_c = tq // CMP_STRIDE
    below = _ceil_to(-(-(REL_MAX_DISTANCE + CMP_BLOCK) // CMP_STRIDE), SUBLANES)
    band = below + per_tile_c
    band_start = pl.multiple_of(jnp.maximum(qi * per_tile_c - below, 0), SUBLANES)
    n_b = band_start + lax.broadcasted_iota(jnp.int32, (band, tq), 0)
    dist_b = q0 + lax.broadcasted_iota(jnp.int32, (band, tq), 1) - (n_b * CMP_STRIDE + CMP_BLOCK - 1)
    lc_ref[...] = _dot(kc_ref[...], qs)
    lc_ref[pl.ds(band_start, band), :] += _bias_tile_t(tbl_ref, nh, dist_b)
    logit_c = jnp.where(valid_c, lc_ref[...], NEG)
    p_c = jnp.where(valid_c, jnp.exp2(logit_c - jnp.max(logit_c, axis=0, keepdims=True)), 0.0)
    p_c = p_c / jnp.maximum(jnp.sum(p_c, axis=0, keepdims=True), TINY)
    tot_ref[...] = gate(0) * _dot(vc_ref[...], p_c.astype(BF16))[:HEAD_DIM]

    p_sum = p_c[:, 0:tq]
    for h in range(1, nh):
        p_sum = p_sum + p_c[:, h * tq:(h + 1) * tq]
    imp = _dot_hilo_rhs(ov_ref[...], p_sum)
    blk = lax.broadcasted_iota(jnp.int32, (LANES, tq), 0)
    cur = (q0 + lax.broadcasted_iota(jnp.int32, (LANES, tq), 1)) >> SLC_SHIFT
    forced = (blk == 0) | (blk == cur) | (blk == cur - 1)
    score = jnp.where(blk <= cur, jnp.where(forced, BIG, imp), NEG)
    sel = jnp.where(blk <= cur, _top_n_mask_t(score, SLC_TOPN), 0.0)
    sel_ref[...] = jnp.concatenate([sel] * nh, axis=1)

    j_i = lax.broadcasted_iota(jnp.int32, (tq, cols), 0)
    i_i = lax.broadcasted_iota(jnp.int32, (tq, cols), 1) & (tq - 1)
    causal = j_i <= i_i
    per_tile = tq // SLC_BLOCK

    nwin = WINDOW // tq
    n_sel_states = NSA_SEL_STATES
    _softmax_init(m_ref, None, acc_ref)

    def tiles(*specs):
        ready = []
        for state, k_ref, v_ref, kb, bias, mask, live in specs:
            if live is not None:
                kb = jnp.maximum(kb, 0)
                alive = jnp.full((tq, cols), live.astype(F32), F32) > 0.5
                mask = alive if mask is None else mask & alive
            start = pl.multiple_of(kb * tq, tq)
            logits = _dot(k_ref[pl.ds(start, tq), :], qs)
            if bias is not None:
                logits = logits + bias
            ready.append((state, logits, mask, v_ref, start))
        for state, logits, mask, v_ref, start in ready:
            _softmax_update(m_ref, None, acc_ref, state, logits, mask, v_ref[:, pl.ds(start, tq)])

    def chosen(kb):
        parts = [jnp.broadcast_to(sel_ref[pl.ds(kb * per_tile + r, 1), :], (SLC_BLOCK, cols)) for r in range(per_tile)]
        return jnp.concatenate(parts, axis=0) > 0.5

    n_far = jnp.maximum(qi - 1, 0)

    def sel_body(i, carry):
        second = jnp.minimum(2 * i + 1, n_far - 1)
        tiles((0, ks_ref, vs_ref, 2 * i, None, chosen(2 * i), None),
              (1, ks_ref, vs_ref, second, None, chosen(second), 2 * i + 1 < n_far))
        return carry

    lax.fori_loop(0, (n_far + 1) >> 1, sel_body, 0)
    near = jnp.maximum(qi - 1, 0)
    window = [(n_sel_states + r, kw_ref, vw_ref, qi - r, bnear_ref[...] if r == 1 else None,
               (j_i > i_i) if r == nwin else None, qi >= r) for r in range(nwin, 0, -1)]
    tiles((2, ks_ref, vs_ref, near, bnear_ref[...], chosen(near), qi >= 1),
          (3, ks_ref, vs_ref, qi, bdiag_ref[...], chosen(qi) & causal, None),
          *window,
          (n_sel_states, kw_ref, vw_ref, qi, bdiag_ref[...], causal, None))
    tot = (tot_ref[...] + gate(1) * result(list(range(n_sel_states)))
           + gate(2) * result(list(range(n_sel_states, n_sel_states + nwin + 1))))
    pairs = [jnp.concatenate([tot[:, (2 * c) * tq:(2 * c + 1) * tq], tot[:, (2 * c + 1) * tq:(2 * c + 2) * tq]], axis=0).T
             for c in range(nh // 2)]
    o_ref[...] = jnp.concatenate(pairs, axis=1).astype(o_ref.dtype)


def _nsa_attention(kmat, feat, kc, vc, gates_t, tbl, b, s, tq):
    t = b * s
    nq = s // tq
    ncmp = kc.shape[2]
    cols = NSA_HPG * tq
    n_states = NSA_SEL_STATES + WINDOW // tq + 1
    k_spec = lambda blk: pl.BlockSpec((s, LANES), lambda bi, g, i: (bi, blk + g))
    v_spec = lambda blk: pl.BlockSpec((LANES, s), lambda bi, g, i: (blk + g, bi))
    return pl.pallas_call(
        _nsa_kernel,
        grid=(b, NSA_KV_GROUPS, nq),
        in_specs=[
            pl.BlockSpec((2 * LANES, tq), lambda bi, g, i: (F_NS_Q // 2 + g, bi * nq + i)),
            pl.BlockSpec((None, None, ncmp, LANES), lambda bi, g, i: (bi, g, 0, 0)),
            pl.BlockSpec((None, None, LANES, ncmp), lambda bi, g, i: (bi, g, 0, 0)),
            k_spec(K_NS_SLC), v_spec(F_NS_SLC_V), k_spec(K_NS_WIN), v_spec(F_NS_WIN_V),
            pl.BlockSpec((LANES, tq), lambda bi, g, i: (g, bi * nq + i)),
            pl.BlockSpec((None, SUBLANES, LANES), lambda bi, g, i: (g, 0, 0)),
        ],
        out_specs=pl.BlockSpec((tq, 2 * LANES), lambda bi, g, i: (bi * nq + i, g)),
        out_shape=jax.ShapeDtypeStruct((t, NSA_HEADS * HEAD_DIM), BF16),
        scratch_shapes=[
            pltpu.VMEM((tq, cols), F32),
            pltpu.VMEM((tq, cols), F32),
            pltpu.VMEM((LANES, ncmp), BF16),
            pltpu.VMEM((LANES, cols), F32),
            pltpu.VMEM((n_states, 1, cols), F32),
            pltpu.VMEM((n_states, LANES, cols), F32),
            pltpu.VMEM((HEAD_DIM, cols), F32),
            pltpu.VMEM((ncmp, cols), F32),
        ],
        compiler_params=pltpu.CompilerParams(
            dimension_semantics=("arbitrary", "arbitrary", "arbitrary"), vmem_limit_bytes=VMEM_LIMIT),
        name="nsa",
    )(feat, kc, vc, kmat, feat, kmat, feat, gates_t, tbl)


def _merge_kernel(x_ref, g_ref, oa_ref, ob_ref, oc_ref, wg_ref, wb_ref, wo_ref, o_ref):
    x = x_ref[...]
    d = x.shape[1]
    h = _rms(x, g_ref[...]).astype(BF16)
    mix = None
    row = 0
    for br, src in enumerate((oa_ref, ob_ref, oc_ref)):
        width = src.shape[1]
        gate = jax.nn.sigmoid(_dot(h, wg_ref[:, br * d:(br + 1) * d]))
        term = gate * _dot(src[...], wb_ref[row:row + width, :])
        mix = term if mix is None else mix + term
        row += width
    o_ref[...] = x + _dot(mix.astype(BF16), wo_ref[...])


def _merge(x2, g, oa, ob, oc, wg, wb, wo, tm):
    t, d = x2.shape
    row = lambda w: pl.BlockSpec((tm, w), lambda i: (i, 0))
    full = lambda a: pl.BlockSpec(a.shape, lambda i: (0, 0))
    return pl.pallas_call(
        _merge_kernel,
        grid=(t // tm,),
        in_specs=[row(d), full(g), row(oa.shape[1]), row(ob.shape[1]), row(oc.shape[1]),
                  full(wg), full(wb), full(wo)],
        out_specs=row(d),
        out_shape=jax.ShapeDtypeStruct((t, d), F32),
        compiler_params=pltpu.CompilerParams(
            dimension_semantics=("parallel",), vmem_limit_bytes=VMEM_LIMIT),
        name="merge",
    )(x2, g, oa, ob, oc, wg, wb, wo)


def _ffn_kernel(x_ref, g_ref, wgu_ref, wd_ref, o_ref, *, n_chunks):
    x = x_ref[...]
    d_ff = wd_ref.shape[0]
    ch = d_ff // n_chunks
    h = _rms(x, g_ref[...]).astype(BF16)
    out = x
    for c in range(n_chunks):
        gate = _dot(h, wgu_ref[:, c * ch:(c + 1) * ch])
        up = _dot(h, wgu_ref[:, d_ff + c * ch:d_ff + (c + 1) * ch])
        out = out + _dot((jax.nn.silu(gate) * up).astype(BF16), wd_ref[c * ch:(c + 1) * ch, :])
    o_ref[...] = out


def _ffn(x2, g, wgu, wd, tm):
    t, d = x2.shape
    d_ff = wd.shape[0]
    n_chunks = 2 if d_ff % (2 * LANES) == 0 else 1
    return pl.pallas_call(
        functools.partial(_ffn_kernel, n_chunks=n_chunks),
        grid=(t // tm,),
        in_specs=[
            pl.BlockSpec((tm, d), lambda i: (i, 0)),
            pl.BlockSpec((1, d), lambda i: (0, 0)),
            pl.BlockSpec(wgu.shape, lambda i: (0, 0)),
            pl.BlockSpec(wd.shape, lambda i: (0, 0)),
        ],
        out_specs=pl.BlockSpec((tm, d), lambda i: (i, 0)),
        out_shape=jax.ShapeDtypeStruct((t, d), F32),
        compiler_params=pltpu.CompilerParams(
            dimension_semantics=("parallel",), vmem_limit_bytes=VMEM_LIMIT),
        name="swiglu",
    )(x2, g, wgu, wd)


def _dup_groups(w):
    d = w.shape[0]
    w = w.reshape(d, NSA_KV_GROUPS, 1, HEAD_DIM)
    return jnp.broadcast_to(w, (d, NSA_KV_GROUPS, 2, HEAD_DIM)).reshape(d, NSA_KV_GROUPS * PAIR)


def _half_groups(w):
    d = w.shape[0]
    w = w.reshape(d, NSA_KV_GROUPS, HEAD_DIM)
    return jnp.pad(w, ((0, 0), (0, 0), (0, HEAD_DIM))).reshape(d, NSA_KV_GROUPS * PAIR)


def _pair_gain(g, scale=1.0):
    return jnp.concatenate([g, g]) * scale


def _layer_params(w_in, moba_q_norm, moba_k_norm, nsa_q_norm, nsa_k_norm, nsa_cmp_pos, nsa_cmp_w1, nsa_cmp_w2):
    d = w_in.shape[0]
    sbw, mbw, nsw, kvw = SB_HEADS * HEAD_DIM, MOBA_HEADS * HEAD_DIM, NSA_HEADS * HEAD_DIM, NSA_KV_GROUPS * HEAD_DIM
    sb_q, sb_k, sb_v = (w_in[:, i * sbw:(i + 1) * sbw] for i in range(3))
    o = 3 * sbw
    mb_q, mb_k, mb_v = (w_in[:, o + i * mbw:o + (i + 1) * mbw] for i in range(3))
    o += 3 * mbw
    ns_q = w_in[:, o:o + nsw]
    o += nsw
    kc_w, vc_w, ks_w, vs_w, kw_w, vw_w = (w_in[:, o + i * kvw:o + (i + 1) * kvw] for i in range(6))
    o += 6 * kvw
    gate_w = w_in[:, o:o + N_BRANCHES * NSA_HEADS].reshape(d, N_BRANCHES, NSA_KV_GROUPS, NSA_HPG)
    o += N_BRANCHES * NSA_HEADS
    wg = w_in[:, o:].astype(BF16)
    w = jnp.concatenate([sb_k, mb_k, _dup_groups(ks_w), _dup_groups(kw_w),
                         sb_q, mb_q, ns_q, sb_v, mb_v, _half_groups(vs_w), _half_groups(vw_w)], axis=1).astype(BF16)
    gate_cols = []
    for g in range(NSA_KV_GROUPS):
        cols = gate_w[:, :, g, :].reshape(d, N_BRANCHES * NSA_HPG)
        gate_cols.append(jnp.pad(cols, ((0, 0), (0, LANES - N_BRANCHES * NSA_HPG))))
    wa = jnp.concatenate([kc_w, vc_w] + gate_cols, axis=1).astype(BF16)
    scale = HEAD_DIM ** -0.5
    ones = jnp.ones((LANES,), F32)
    gains = [ones] * 2 + [_pair_gain(moba_k_norm)] * 2 + [_pair_gain(nsa_k_norm[1])] * 2 + [_pair_gain(nsa_k_norm[2])] * 2
    gains += [ones * (scale * LOG2E)] * 2 + [_pair_gain(moba_q_norm, scale * LOG2E)] * 2
    gains += [_pair_gain(nsa_q_norm, scale * LOG2E)] * 4 + [ones] * 8
    gain = jnp.concatenate(gains).reshape(1, (N_K_BLOCKS + N_F_BLOCKS) * LANES)

    half = CMP_BLOCK // 2
    hid = CMP_HIDDEN

    def spread(w1):
        w1 = w1.reshape(2, half, HEAD_DIM, hid)
        cols = []
        for part in range(2):
            for g in range(NSA_KV_GROUPS):
                z = jnp.zeros((half, NSA_KV_GROUPS, HEAD_DIM, hid), F32).at[:, g].set(w1[part])
                cols.append(z.reshape(half * kvw, hid))
        return jnp.concatenate(cols, axis=1).astype(BF16)

    def spread_pos(p):
        p = p.reshape(2, half, 1, HEAD_DIM)
        return jnp.broadcast_to(p, (2, half, NSA_KV_GROUPS, HEAD_DIM)).reshape(2, half * kvw)

    pos = jnp.concatenate([spread_pos(nsa_cmp_pos[0]), spread_pos(nsa_cmp_pos[1])], axis=0)
    w2 = jnp.stack([jnp.concatenate([nsa_cmp_w2[0], nsa_cmp_w2[0]], axis=-1),
                    jnp.pad(nsa_cmp_w2[1], ((0, 0), (0, HEAD_DIM)))]).astype(BF16)
    kn = _pair_gain(nsa_k_norm[0]).reshape(1, LANES)
    return dict(w=w, wa=wa, gain=gain, wg=wg, wk=spread(nsa_cmp_w1[0]), wv=spread(nsa_cmp_w1[1]),
                pos=pos, w2=w2, kn=kn)


def _bias_tables(rel_bias):
    tbl = jnp.pad(rel_bias.T, ((0, 0), (0, LANES - N_BUCKETS)))

    def rows(t, n):
        t = t.reshape(n, -1, LANES)
        return jnp.pad(t, ((0, 0), (0, SUBLANES - t.shape[1]), (0, 0)))

    return rows(tbl[:MOBA_HEADS], MOBA_HEADS // 2), rows(tbl[MOBA_HEADS:], NSA_KV_GROUPS)


def kernel(x, rel_bias, attn_norm, w_in, moba_q_norm, moba_k_norm, nsa_q_norm, nsa_k_norm, nsa_cmp_pos,
           nsa_cmp_w1, nsa_cmp_w2, w_branch, w_out, ffn_norm, w_gate_up, w_down):
    b, s, d = x.shape
    t = b * s
    tm = min(512, t)
    tq_pair = 512
    tq_nsa = 256
    tk_sb = 256
    assert s % tq_pair == 0 and tq_pair % MOBA_BLOCK == 0 and tq_pair % tk_sb == 0 and WINDOW % tq_nsa == 0
    tbl_moba, tbl_nsa = _bias_tables(rel_bias)
    x2 = x.reshape(t, d)
    for layer in range(w_in.shape[0]):
        p = _layer_params(w_in[layer], moba_q_norm[layer], moba_k_norm[layer], nsa_q_norm[layer],
                          nsa_k_norm[layer], nsa_cmp_pos[layer], nsa_cmp_w1[layer], nsa_cmp_w2[layer])
        g_attn = attn_norm[layer].reshape(1, d)
        kmat, feat, kc_in, vc_in, gates_t = _inproj(x2, g_attn, p["w"], p["wa"], p["gain"], tm)
        o_a = _sb_attention(kmat, feat, b, s, tq_pair, tk_sb)
        o_b = _moba_attention(kmat, feat, tbl_moba, b, s, tq_pair)
        chunk = CMP_STRIDE * LANES
        kc, vc = _compress(kc_in.reshape(b, s // CMP_STRIDE, chunk), vc_in.reshape(b, s // CMP_STRIDE, chunk),
                           p["wk"], p["wv"], p["pos"], p["w2"], p["kn"])
        o_c = _nsa_attention(kmat, feat, kc, vc, gates_t, tbl_nsa, b, s, tq_nsa)
        x2 = _merge(x2, g_attn, o_a, o_b, o_c, p["wg"], w_branch[layer].astype(BF16),
                    w_out[layer].astype(BF16), tm)
        x2 = _ffn(x2, ffn_norm[layer].reshape(1, d), w_gate_up[layer].astype(BF16),
                  w_down[layer].astype(BF16), tm)
    return x2.reshape(b, s, d)
```

```python
import functools
import math

import jax
import jax.numpy as jnp
from jax import lax
from jax.experimental import pallas as pl
from jax.experimental.pallas import tpu as pltpu

HEAD_DIM = 64
SB_HEADS = 4
MOBA_HEADS = 4
NSA_HEADS = 8
NSA_KV_GROUPS = 2
NSA_HPG = NSA_HEADS // NSA_KV_GROUPS
N_BRANCHES = 3
MOBA_BLOCK = 256
MOBA_TOPK = 3
CMP_BLOCK = 32
CMP_STRIDE = 16
CMP_HIDDEN = 4 * HEAD_DIM
SLC_BLOCK = 64
SLC_TOPN = 16
WINDOW = 512
N_BUCKETS = 32
REL_MAX_DISTANCE = 128
NORM_EPS = 1e-6
NEG = -1e30
BIG = 1e30
TINY = 1e-30
LOG2E = math.log2(math.e)
MOBA_FAR_STATES = 4
NSA_FAR_GROUP = 4
NSA_SEL_STATES = NSA_FAR_GROUP + 2
SB_UNDERFLOW_LOG2 = 160.0

LANES = 128
SUBLANES = 8
ONES_ROWS = 16
PAIR = 2 * HEAD_DIM
VMEM_LIMIT = 56 * 1024 * 1024

F32 = jnp.float32
BF16 = jnp.bfloat16

K_SB, K_MB, K_NS_SLC, K_NS_WIN = 0, 2, 4, 6
N_K_BLOCKS = 8
F_SB_Q, F_MB_Q, F_NS_Q, F_SB_V, F_MB_V, F_NS_SLC_V, F_NS_WIN_V = 0, 2, 4, 8, 10, 12, 14
N_F_BLOCKS = 16
K_MODES = (0, 0, 2, 2, 2, 2, 2, 2)
F_MODES = (1, 1, 2, 2, 2, 2, 2, 2, 0, 0, 0, 0, 3, 3, 3, 3)


SLC_SHIFT = SLC_BLOCK.bit_length() - 1
MOBA_SHIFT = MOBA_BLOCK.bit_length() - 1
assert 1 << SLC_SHIFT == SLC_BLOCK and 1 << MOBA_SHIFT == MOBA_BLOCK


def _ceil_to(n, m):
    return -(-n // m) * m


def _dot(a, b):
    return jnp.dot(a, b, preferred_element_type=F32)


def _dot_hilo_rhs(a, b):
    hi = b.astype(BF16)
    lo = (b - hi.astype(F32)).astype(BF16)
    return _dot(a, hi) + _dot(a, lo)


def _rms(x, g):
    ms = jnp.mean(x * x, axis=-1, keepdims=True)
    return x * lax.rsqrt(ms + NORM_EPS) * g


def _stack_heads_t(q_t):
    tq = q_t.shape[1]
    lo = lax.broadcasted_iota(jnp.int32, (LANES, tq), 0) < HEAD_DIM
    zero = jnp.zeros((LANES, tq), q_t.dtype)
    parts = []
    for c in range(q_t.shape[0] // LANES):
        blk = q_t[c * LANES:(c + 1) * LANES, :]
        parts.append(jnp.where(lo, blk, zero))
        parts.append(jnp.where(lo, zero, blk))
    return jnp.concatenate(parts, axis=1)


def _t5_bucket(dist):
    n = jnp.maximum(dist, 0)
    max_exact = N_BUCKETS // 2
    nf = jnp.maximum(n, 1).astype(F32)
    large = max_exact + (jnp.log(nf / max_exact) / math.log(REL_MAX_DISTANCE / max_exact)
                         * (N_BUCKETS - max_exact)).astype(jnp.int32)
    large = jnp.minimum(large, N_BUCKETS - 1)
    return jnp.where(n < max_exact, n, large)


def _bias_tile_t(tbl_ref, n_heads, dist):
    nk, tq = dist.shape
    bucket = _t5_bucket(dist)
    cols = []
    for h in range(n_heads):
        t = jnp.broadcast_to(tbl_ref[h:h + 1, :], (nk, LANES))
        far = t[:, N_BUCKETS - 1:N_BUCKETS]
        for c in range(tq // LANES):
            cols.append((jnp.take_along_axis(t, bucket[:, c * LANES:(c + 1) * LANES], axis=1) - far) * LOG2E)
    return jnp.concatenate(cols, axis=1)


def _top_n_mask_t(score, n):
    row = lax.broadcasted_iota(jnp.int32, score.shape, 0).astype(F32)
    sel = jnp.zeros(score.shape, F32)
    s = score
    for _ in range(n):
        m = jnp.max(s, axis=0, keepdims=True)
        idx = jnp.min(jnp.where(s == m, row, float(score.shape[0])), axis=0, keepdims=True)
        pick = row == idx
        sel = jnp.where(pick, 1.0, sel)
        s = jnp.where(pick, -jnp.inf, s)
    return sel


def _post(yb, lo, mode, gain):
    if mode == 2:
        sq = yb * yb
        s_lo = jnp.sum(jnp.where(lo, sq, 0.0), axis=-1, keepdims=True)
        s_hi = jnp.sum(jnp.where(lo, 0.0, sq), axis=-1, keepdims=True)
        yb = yb * lax.rsqrt(jnp.where(lo, s_lo, s_hi) * (1.0 / HEAD_DIM) + NORM_EPS)
    if mode in (1, 2):
        yb = yb * gain
    return yb


def _inproj_kernel(x_ref, g_ref, w_ref, wa_ref, gain_ref, k_ref, f_ref, kc_ref, vc_ref, gate_ref):
    tm = x_ref.shape[0]
    h = _rms(x_ref[...], g_ref[...]).astype(BF16)
    lo = lax.broadcasted_iota(jnp.int32, (tm, LANES), 1) < HEAD_DIM
    for c in range((N_K_BLOCKS + N_F_BLOCKS) // 2):
        y = _dot(h, w_ref[:, c * 2 * LANES:(c + 1) * 2 * LANES])
        for s in range(2):
            blk = 2 * c + s
            gain = gain_ref[:, blk * LANES:(blk + 1) * LANES]
            yb = y[:, s * LANES:(s + 1) * LANES]
            if blk < N_K_BLOCKS:
                k_ref[:, blk * LANES:(blk + 1) * LANES] = _post(yb, lo, K_MODES[blk], gain).astype(BF16)
            else:
                fb = blk - N_K_BLOCKS
                yb = jnp.where(lo, yb, 1.0) if F_MODES[fb] == 3 else _post(yb, lo, F_MODES[fb], gain)
                f_ref[fb * LANES:(fb + 1) * LANES, :] = yb.T.astype(BF16)
    aux = _dot(h, wa_ref[...])
    kc_ref[...] = aux[:, 0:LANES]
    vc_ref[...] = aux[:, LANES:2 * LANES]
    for g in range(NSA_KV_GROUPS):
        gate_ref[g * LANES:(g + 1) * LANES, :] = jax.nn.sigmoid(aux[:, (2 + g) * LANES:(3 + g) * LANES]).T


def _inproj(x2, g, w, wa, gain, tm):
    t, d = x2.shape
    nw = w.shape[1]
    return pl.pallas_call(
        _inproj_kernel,
        grid=(t // tm,),
        in_specs=[
            pl.BlockSpec((tm, d), lambda i: (i, 0)),
            pl.BlockSpec((1, d), lambda i: (0, 0)),
            pl.BlockSpec((d, nw), lambda i: (0, 0)),
            pl.BlockSpec((d, 4 * LANES), lambda i: (0, 0)),
            pl.BlockSpec((1, nw), lambda i: (0, 0)),
        ],
        out_specs=[
            pl.BlockSpec((tm, N_K_BLOCKS * LANES), lambda i: (i, 0)),
            pl.BlockSpec((N_F_BLOCKS * LANES, tm), lambda i: (0, i)),
            pl.BlockSpec((tm, LANES), lambda i: (i, 0)),
            pl.BlockSpec((tm, LANES), lambda i: (i, 0)),
            pl.BlockSpec((NSA_KV_GROUPS * LANES, tm), lambda i: (0, i)),
        ],
        out_shape=[
            jax.ShapeDtypeStruct((t, N_K_BLOCKS * LANES), BF16),
            jax.ShapeDtypeStruct((N_F_BLOCKS * LANES, t), BF16),
            jax.ShapeDtypeStruct((t, LANES), F32),
            jax.ShapeDtypeStruct((t, LANES), F32),
            jax.ShapeDtypeStruct((NSA_KV_GROUPS * LANES, t), F32),
        ],
        compiler_params=pltpu.CompilerParams(
            dimension_semantics=("parallel",), vmem_limit_bytes=VMEM_LIMIT),
        name="inproj",
    )(x2, g, w, wa, gain)


def _sb_kernel(q_ref, k_ref, v_ref, o_ref, r_ref, acc_ref, *, tk):
    tq = q_ref.shape[1]
    cols = 2 * tq
    nsub = tq // tk
    qi = pl.program_id(2)
    qs = _stack_heads_t(q_ref[...])
    j_i = lax.broadcasted_iota(jnp.int32, (tk, cols), 0)
    t_i = lax.broadcasted_iota(jnp.int32, (tk, cols), 1) & (tq - 1)
    after = jnp.where((lax.broadcasted_iota(jnp.int32, (tk, 2 * tk), 1) & (tk - 1))
                      > lax.broadcasted_iota(jnp.int32, (tk, 2 * tk), 0), 1.0, 0.0).astype(BF16)
    r_ref[...] = jnp.zeros(r_ref.shape, F32)
    acc_ref[...] = jnp.zeros(acc_ref.shape, F32)

    def block(kb, strict):
        start = pl.multiple_of(kb * tk, tk)
        z = _dot(k_ref[pl.ds(start, tk), :], qs)
        sp = jnp.maximum(z, 0.0) + jnp.log2(1.0 + jnp.exp2(-jnp.abs(z)))
        stay = sp if strict is None else jnp.where(strict, sp, 0.0)
        hi = stay.astype(BF16)
        lo = (stay - hi.astype(F32)).astype(BF16)
        after_sum = _dot(after, jnp.concatenate([hi, lo], axis=0))
        a = jnp.exp2(z - sp - after_sum - r_ref[...])
        if strict is not None:
            a = jnp.where(strict, a, 0.0)
        acc_ref[...] += _dot(v_ref[:, pl.ds(start, tk)], a.astype(BF16))
        r_ref[...] += jnp.sum(stay, axis=0, keepdims=True)

    for u in range(nsub - 1, -1, -1):
        block(qi * nsub + u, u * tk + j_i < t_i)

    def more(carry):
        j, r_min = carry
        return (j < qi * nsub) & (r_min < SB_UNDERFLOW_LOG2)

    def body(carry):
        j, _ = carry
        block(qi * nsub - 1 - j, None)
        return j + 1, jnp.min(r_ref[...])

    lax.while_loop(more, body, (0, jnp.min(r_ref[...])))
    lo = lax.broadcasted_iota(jnp.int32, (LANES, tq), 0) < HEAD_DIM
    acc = acc_ref[...]
    o_ref[...] = jnp.where(lo, acc[:, :tq], acc[:, tq:]).T.astype(o_ref.dtype)


def _sb_attention(kmat, feat, b, s, tq, tk):
    t = b * s
    nq = s // tq
    return pl.pallas_call(
        functools.partial(_sb_kernel, tk=tk),
        grid=(b, SB_HEADS // 2, nq),
        in_specs=[
            pl.BlockSpec((LANES, tq), lambda bi, p, i: (F_SB_Q + p, bi * nq + i)),
            pl.BlockSpec((s, LANES), lambda bi, p, i: (bi, K_SB + p)),
            pl.BlockSpec((LANES, s), lambda bi, p, i: (F_SB_V + p, bi)),
        ],
        out_specs=pl.BlockSpec((tq, LANES), lambda bi, p, i: (bi * nq + i, p)),
        out_shape=jax.ShapeDtypeStruct((t, SB_HEADS * HEAD_DIM), BF16),
        scratch_shapes=[pltpu.VMEM((1, 2 * tq), F32), pltpu.VMEM((LANES, 2 * tq), F32)],
        compiler_params=pltpu.CompilerParams(
            dimension_semantics=("parallel", "parallel", "arbitrary"), vmem_limit_bytes=VMEM_LIMIT),
        name="stickbreak",
    )(feat, kmat, feat)


def _softmax_init(m_ref, acc_ref):
    m_ref[...] = jnp.full(m_ref.shape, NEG, F32)
    acc_ref[...] = jnp.zeros(acc_ref.shape, F32)


def _softmax_update(m_ref, acc_ref, s, logits, v_t):
    m_old = m_ref[s]
    m_new = jnp.maximum(m_old, jnp.max(logits, axis=0, keepdims=True))
    p = jnp.exp2(logits - m_new)
    acc_ref[s] = jnp.exp2(m_old - m_new) * acc_ref[s] + _dot(v_t, p.astype(BF16))
    m_ref[s] = m_new


def _softmax_merge(m_ref, acc_ref, states):
    m = m_ref[states[0]]
    for s in states[1:]:
        m = jnp.maximum(m, m_ref[s])
    acc = None
    for s in states:
        w = jnp.exp2(m_ref[s] - m)
        acc = w * acc_ref[s] if acc is None else acc + w * acc_ref[s]
    return acc


def _hide(live):
    return jnp.where(live, 0.0, -jnp.inf)


def _moba_kernel(q_ref, k_ref, v_ref, tbl_ref, o_ref, km_ref, bias_ref, sel_ref, m_ref, acc_ref, *, nblk):
    tk = MOBA_BLOCK
    tq = q_ref.shape[1]
    nsub = tq // tk
    cols = 2 * tq
    qi = pl.program_id(2)

    @pl.when(qi == 0)
    def _():
        km_ref[...] = jnp.zeros(km_ref.shape, F32)

        def mean_body(n, carry):
            kb = k_ref[pl.ds(pl.multiple_of(n * tk, tk), tk), :].astype(F32)
            km_ref[pl.ds(n, 1), :] = jnp.mean(kb, axis=0, keepdims=True)
            return carry

        lax.fori_loop(0, nblk, mean_body, 0)
        j = lax.broadcasted_iota(jnp.int32, (tk, tq), 0)
        i = lax.broadcasted_iota(jnp.int32, (tk, tq), 1)
        for e in range(-1, nsub):
            bias_ref[e + 1] = _bias_tile_t(tbl_ref, 2, i - tk * e - j)

    qs = _stack_heads_t(q_ref[...])
    sub_s = lax.broadcasted_iota(jnp.int32, (LANES, cols), 1) & (tq - 1)
    past = lax.broadcasted_iota(jnp.int32, (LANES, cols), 0) < nsub * qi + (sub_s >> MOBA_SHIFT)
    score = jnp.where(past, _dot(km_ref[...].astype(BF16), qs), NEG)
    sel_ref[...] = jnp.where(past & (_top_n_mask_t(score, MOBA_TOPK) > 0.5), 0.0, -jnp.inf)
    _softmax_init(m_ref, acc_ref)
    ones = jnp.ones((ONES_ROWS, tk), BF16)

    def tiles(*specs):
        ready = []
        for state, n, fn in specs:
            start = pl.multiple_of(n * tk, tk)
            ready.append((state, fn(_dot(k_ref[pl.ds(start, tk), :], qs)), start))
        for state, logits, start in ready:
            v_t = jnp.concatenate([v_ref[:, pl.ds(start, tk)], ones], axis=0)
            _softmax_update(m_ref, acc_ref, state, logits, v_t)

    def hidden(n, live=None):
        row = sel_ref[pl.ds(n, 1), :]
        if live is not None:
            row = row + _hide(live)
        return jnp.broadcast_to(row, (tk, cols))

    n_far = jnp.maximum(nsub * qi - 1, 0)

    group = MOBA_FAR_STATES

    def far_tile(i, u):
        n = group * i + u
        if u == 0:
            return (u, n, lambda z: z + hidden(n))
        clamped = jnp.minimum(n, n_far - 1)
        return (u, clamped, lambda z: z + hidden(clamped, live=n < n_far))

    def far_body(i, carry):
        tiles(*[far_tile(i, u) for u in range(group)])
        return carry

    lax.fori_loop(0, (n_far + group - 1) // group, far_body, 0)
    prev = jnp.maximum(nsub * qi - 1, 0)
    j_i = lax.broadcasted_iota(jnp.int32, (tk, cols), 0)
    t_i = lax.broadcasted_iota(jnp.int32, (tk, cols), 1) & (tq - 1)

    def own_tile(e):
        causal_own = ((t_i >> MOBA_SHIFT) == e) & (j_i <= t_i - tk * e)
        return (MOBA_FAR_STATES + 1 + e, nsub * qi + e,
                lambda z: z + bias_ref[e + 1] + jnp.where(causal_own, 0.0, hidden(nsub * qi + e)))

    tiles((MOBA_FAR_STATES, prev, lambda z: z + bias_ref[0] + hidden(prev, live=qi >= 1)),
          *[own_tile(e) for e in range(nsub)])
    acc = _softmax_merge(m_ref, acc_ref, list(range(MOBA_FAR_STATES + 1 + nsub)))
    out = acc[:LANES] / jnp.maximum(acc[LANES:LANES + 1], TINY)
    lo = lax.broadcasted_iota(jnp.int32, (LANES, tq), 0) < HEAD_DIM
    o_ref[...] = jnp.where(lo, out[:, :tq], out[:, tq:]).T.astype(o_ref.dtype)


def _moba_attention(kmat, feat, tbl, b, s, tq):
    t = b * s
    tk = MOBA_BLOCK
    nq = s // tq
    cols = 2 * tq
    n_states = MOBA_FAR_STATES + 1 + tq // tk
    return pl.pallas_call(
        functools.partial(_moba_kernel, nblk=s // tk),
        grid=(b, MOBA_HEADS // 2, nq),
        in_specs=[
            pl.BlockSpec((LANES, tq), lambda bi, p, i: (F_MB_Q + p, bi * nq + i)),
            pl.BlockSpec((s, LANES), lambda bi, p, i: (bi, K_MB + p)),
            pl.BlockSpec((LANES, s), lambda bi, p, i: (F_MB_V + p, bi)),
            pl.BlockSpec((None, SUBLANES, LANES), lambda bi, p, i: (p, 0, 0)),
        ],
        out_specs=pl.BlockSpec((tq, LANES), lambda bi, p, i: (bi * nq + i, p)),
        out_shape=jax.ShapeDtypeStruct((t, MOBA_HEADS * HEAD_DIM), BF16),
        scratch_shapes=[
            pltpu.VMEM((LANES, LANES), F32),
            pltpu.VMEM((tq // tk + 1, tk, cols), F32),
            pltpu.VMEM((LANES, cols), F32),
            pltpu.VMEM((n_states, 1, cols), F32),
            pltpu.VMEM((n_states, LANES + ONES_ROWS, cols), F32),
        ],
        compiler_params=pltpu.CompilerParams(
            dimension_semantics=("arbitrary", "arbitrary", "arbitrary"), vmem_limit_bytes=VMEM_LIMIT),
        name="moba",
    )(feat, kmat, feat, tbl)


def _compress_kernel(rk_ref, rv_ref, wk_ref, wv_ref, pos_ref, w2_ref, kn_ref, kc_ref, vc_ref):
    n = rk_ref.shape[0]
    hid = CMP_HIDDEN
    lo = lax.broadcasted_iota(jnp.int32, (n, LANES), 1) < HEAD_DIM
    for idx, (r_ref, w_ref) in enumerate(((rk_ref, wk_ref), (rv_ref, wv_ref))):
        r = r_ref[...]
        first = _dot((r + pos_ref[2 * idx:2 * idx + 1, :]).astype(BF16), w_ref[:, :2 * hid])
        second = _dot((r + pos_ref[2 * idx + 1:2 * idx + 2, :]).astype(BF16), w_ref[:, 2 * hid:])
        h = jax.nn.gelu(first + pltpu.roll(second, shift=n - 1, axis=0))
        for g in range(NSA_KV_GROUPS):
            y = _dot(h[:, g * hid:(g + 1) * hid].astype(BF16), w2_ref[idx])
            if idx == 0:
                kc_ref[g] = _rms(y, kn_ref[...]).astype(kc_ref.dtype)
            else:
                vc_ref[g] = jnp.where(lo, y, 1.0).T.astype(vc_ref.dtype)


def _compress(rk, rv, wk, wv, pos, w2, kn):
    b, n, width = rk.shape
    return pl.pallas_call(
        _compress_kernel,
        grid=(b,),
        in_specs=[
            pl.BlockSpec((None, n, width), lambda i: (i, 0, 0)),
            pl.BlockSpec((None, n, width), lambda i: (i, 0, 0)),
            pl.BlockSpec(wk.shape, lambda i: (0, 0)),
            pl.BlockSpec(wv.shape, lambda i: (0, 0)),
            pl.BlockSpec(pos.shape, lambda i: (0, 0)),
            pl.BlockSpec(w2.shape, lambda i: (0, 0, 0)),
            pl.BlockSpec(kn.shape, lambda i: (0, 0)),
        ],
        out_specs=[
            pl.BlockSpec((None, NSA_KV_GROUPS, n, LANES), lambda i: (i, 0, 0, 0)),
            pl.BlockSpec((None, NSA_KV_GROUPS, LANES, n), lambda i: (i, 0, 0, 0)),
        ],
        out_shape=[
            jax.ShapeDtypeStruct((b, NSA_KV_GROUPS, n, LANES), BF16),
            jax.ShapeDtypeStruct((b, NSA_KV_GROUPS, LANES, n), BF16),
        ],
        compiler_params=pltpu.CompilerParams(
            dimension_semantics=("parallel",), vmem_limit_bytes=VMEM_LIMIT),
        name="nsa_compress",
    )(rk, rv, wk, wv, pos, w2, kn)


def _nsa_kernel(q_ref, kc_ref, vc_ref, ks_ref, vs_ref, kw_ref, vw_ref, gate_ref, tbl_ref, o_ref,
                bdiag_ref, bnear_ref, ov_ref, sel_ref, m_ref, acc_ref, tot_ref, lc_ref):
    tq = q_ref.shape[1]
    nh = NSA_HPG
    cols = nh * tq
    ncmp = kc_ref.shape[0]
    qi = pl.program_id(2)
    q0 = qi * tq

    @pl.when(qi == 0)
    def _():
        j = lax.broadcasted_iota(jnp.int32, (tq, tq), 0)
        i = lax.broadcasted_iota(jnp.int32, (tq, tq), 1)
        bdiag_ref[...] = _bias_tile_t(tbl_ref, nh, i - j)
        bnear_ref[...] = _bias_tile_t(tbl_ref, nh, tq + i - j)
        ss = lax.broadcasted_iota(jnp.int32, (LANES, ncmp), 0) * SLC_BLOCK
        cs = lax.broadcasted_iota(jnp.int32, (LANES, ncmp), 1) * CMP_STRIDE
        ov_ref[...] = jnp.where((cs < ss + SLC_BLOCK) & (cs + CMP_BLOCK > ss), 1.0, 0.0).astype(BF16)

    qs = _stack_heads_t(q_ref[...])

    def gate(branch):
        return jnp.concatenate([gate_ref[branch * nh + h:branch * nh + h + 1, :] for h in range(nh)], axis=1)

    def result(states):
        acc = _softmax_merge(m_ref, acc_ref, states)
        return acc[:HEAD_DIM] / jnp.maximum(acc[HEAD_DIM:], TINY)

    j_i = lax.broadcasted_iota(jnp.int32, (tq, cols), 0)
    i_i = lax.broadcasted_iota(jnp.int32, (tq, cols), 1) & (tq - 1)
    causal = j_i <= i_i
    per_tile = tq // SLC_BLOCK
    nwin = WINDOW // tq
    n_sel_states = NSA_SEL_STATES
    group = NSA_FAR_GROUP
    _softmax_init(m_ref, acc_ref)

    def tiles(*specs):
        ready = []
        for state, k_ref, v_ref, kb, fn in specs:
            start = pl.multiple_of(kb * tq, tq)
            ready.append((state, fn(_dot(k_ref[pl.ds(start, tq), :], qs)), v_ref, start))
        for state, logits, v_ref, start in ready:
            _softmax_update(m_ref, acc_ref, state, logits, v_ref[:, pl.ds(start, tq)])

    def window_tile(r):
        def fn(z):
            if r == 1:
                z = z + bnear_ref[...]
            if r == nwin:
                z = jnp.where(j_i > i_i, z, -jnp.inf)
            return z + _hide(qi >= r)
        return (n_sel_states + r, kw_ref, vw_ref, jnp.maximum(qi - r, 0), fn)

    tiles(*[window_tile(r) for r in range(nwin, 0, -1)],
          (n_sel_states, kw_ref, vw_ref, qi, lambda z: jnp.where(causal, z + bdiag_ref[...], -jnp.inf)))

    n_i = lax.broadcasted_iota(jnp.int32, (ncmp, cols), 0)
    t_i = lax.broadcasted_iota(jnp.int32, (ncmp, cols), 1) & (tq - 1)
    valid_c = q0 + t_i - (n_i * CMP_STRIDE + CMP_BLOCK - 1) >= 0
    per_tile_c = tq // CMP_STRIDE
    below = _ceil_to(-(-(REL_MAX_DISTANCE + CMP_BLOCK) // CMP_STRIDE), SUBLANES)
    band = below + per_tile_c
    band_start = pl.multiple_of(jnp.maximum(qi * per_tile_c - below, 0), SUBLANES)
    n_b = band_start + lax.broadcasted_iota(jnp.int32, (band, tq), 0)
    dist_b = q0 + lax.broadcasted_iota(jnp.int32, (band, tq), 1) - (n_b * CMP_STRIDE + CMP_BLOCK - 1)
    lc_ref[...] = _dot(kc_ref[...], qs)
    lc_ref[pl.ds(band_start, band), :] += _bias_tile_t(tbl_ref, nh, dist_b)
    logit_c = jnp.where(valid_c, lc_ref[...], -jnp.inf)
    p_c = jnp.exp2(logit_c - jnp.maximum(jnp.max(logit_c, axis=0, keepdims=True), NEG))
    p_c = p_c / jnp.maximum(jnp.sum(p_c, axis=0, keepdims=True), TINY)
    tot_ref[...] = gate(0) * _dot(vc_ref[...], p_c.astype(BF16))[:HEAD_DIM]

    p_sum = p_c[:, 0:tq]
    for h in range(1, nh):
        p_sum = p_sum + p_c[:, h * tq:(h + 1) * tq]
    imp = _dot_hilo_rhs(ov_ref[...], p_sum)
    blk = lax.broadcasted_iota(jnp.int32, (LANES, tq), 0)
    cur = (q0 + lax.broadcasted_iota(jnp.int32, (LANES, tq), 1)) >> SLC_SHIFT
    forced = (blk == 0) | (blk == cur) | (blk == cur - 1)
    score = jnp.where(blk <= cur, jnp.where(forced, BIG, imp), NEG)
    sel = jnp.where((blk <= cur) & (_top_n_mask_t(score, SLC_TOPN) > 0.5), 0.0, -jnp.inf)
    sel_ref[...] = jnp.concatenate([sel] * nh, axis=1)

    def hidden(kb, live=None):
        rows = [sel_ref[pl.ds(kb * per_tile + r, 1), :] for r in range(per_tile)]
        if live is not None:
            rows = [row + _hide(live) for row in rows]
        return jnp.concatenate([jnp.broadcast_to(row, (SLC_BLOCK, cols)) for row in rows], axis=0)

    n_far = jnp.maximum(qi - 1, 0)

    def far_tile(i, u):
        kb = group * i + u
        if u == 0:
            return (u, ks_ref, vs_ref, kb, lambda z: z + hidden(kb))
        clamped = jnp.minimum(kb, n_far - 1)
        return (u, ks_ref, vs_ref, clamped, lambda z: z + hidden(clamped, live=kb < n_far))

    def sel_body(i, carry):
        tiles(*[far_tile(i, u) for u in range(group)])
        return carry

    lax.fori_loop(0, (n_far + group - 1) // group, sel_body, 0)
    near = jnp.maximum(qi - 1, 0)
    tiles((group, ks_ref, vs_ref, near, lambda z: z + bnear_ref[...] + hidden(near, live=qi >= 1)),
          (group + 1, ks_ref, vs_ref, qi, lambda z: jnp.where(causal, z + bdiag_ref[...] + hidden(qi), -jnp.inf)))
    tot = (tot_ref[...] + gate(1) * result(list(range(n_sel_states)))
           + gate(2) * result(list(range(n_sel_states, n_sel_states + nwin + 1))))
    pairs = [jnp.concatenate([tot[:, (2 * c) * tq:(2 * c + 1) * tq], tot[:, (2 * c + 1) * tq:(2 * c + 2) * tq]], axis=0).T
             for c in range(nh // 2)]
    o_ref[...] = jnp.concatenate(pairs, axis=1).astype(o_ref.dtype)


def _nsa_attention(kmat, feat, kc, vc, gates_t, tbl, b, s, tq):
    t = b * s
    nq = s // tq
    ncmp = kc.shape[2]
    cols = NSA_HPG * tq
    n_states = NSA_SEL_STATES + WINDOW // tq + 1
    k_spec = lambda blk: pl.BlockSpec((s, LANES), lambda bi, g, i: (bi, blk + g))
    v_spec = lambda blk: pl.BlockSpec((LANES, s), lambda bi, g, i: (blk + g, bi))
    return pl.pallas_call(
        _nsa_kernel,
        grid=(b, NSA_KV_GROUPS, nq),
        in_specs=[
            pl.BlockSpec((2 * LANES, tq), lambda bi, g, i: (F_NS_Q // 2 + g, bi * nq + i)),
            pl.BlockSpec((None, None, ncmp, LANES), lambda bi, g, i: (bi, g, 0, 0)),
            pl.BlockSpec((None, None, LANES, ncmp), lambda bi, g, i: (bi, g, 0, 0)),
            k_spec(K_NS_SLC), v_spec(F_NS_SLC_V), k_spec(K_NS_WIN), v_spec(F_NS_WIN_V),
            pl.BlockSpec((LANES, tq), lambda bi, g, i: (g, bi * nq + i)),
            pl.BlockSpec((None, SUBLANES, LANES), lambda bi, g, i: (g, 0, 0)),
        ],
        out_specs=pl.BlockSpec((tq, 2 * LANES), lambda bi, g, i: (bi * nq + i, g)),
        out_shape=jax.ShapeDtypeStruct((t, NSA_HEADS * HEAD_DIM), BF16),
        scratch_shapes=[
            pltpu.VMEM((tq, cols), F32),
            pltpu.VMEM((tq, cols), F32),
            pltpu.VMEM((LANES, ncmp), BF16),
            pltpu.VMEM((LANES, cols), F32),
            pltpu.VMEM((n_states, 1, cols), F32),
            pltpu.VMEM((n_states, LANES, cols), F32),
            pltpu.VMEM((HEAD_DIM, cols), F32),
            pltpu.VMEM((ncmp, cols), F32),
        ],
        compiler_params=pltpu.CompilerParams(
            dimension_semantics=("arbitrary", "arbitrary", "arbitrary"), vmem_limit_bytes=VMEM_LIMIT),
        name="nsa",
    )(feat, kc, vc, kmat, feat, kmat, feat, gates_t, tbl)


def _merge_kernel(x_ref, g_ref, oa_ref, ob_ref, oc_ref, wg_ref, wb_ref, wo_ref, o_ref):
    x = x_ref[...]
    d = x.shape[1]
    h = _rms(x, g_ref[...]).astype(BF16)
    mix = None
    row = 0
    for br, src in enumerate((oa_ref, ob_ref, oc_ref)):
        width = src.shape[1]
        gate = jax.nn.sigmoid(_dot(h, wg_ref[:, br * d:(br + 1) * d]))
        term = gate * _dot(src[...], wb_ref[row:row + width, :])
        mix = term if mix is None else mix + term
        row += width
    o_ref[...] = x + _dot(mix.astype(BF16), wo_ref[...])


def _merge(x2, g, oa, ob, oc, wg, wb, wo, tm):
    t, d = x2.shape
    row = lambda w: pl.BlockSpec((tm, w), lambda i: (i, 0))
    full = lambda a: pl.BlockSpec(a.shape, lambda i: (0, 0))
    return pl.pallas_call(
        _merge_kernel,
        grid=(t // tm,),
        in_specs=[row(d), full(g), row(oa.shape[1]), row(ob.shape[1]), row(oc.shape[1]),
                  full(wg), full(wb), full(wo)],
        out_specs=row(d),
        out_shape=jax.ShapeDtypeStruct((t, d), F32),
        compiler_params=pltpu.CompilerParams(
            dimension_semantics=("parallel",), vmem_limit_bytes=VMEM_LIMIT),
        name="merge",
    )(x2, g, oa, ob, oc, wg, wb, wo)


def _ffn_kernel(x_ref, g_ref, wgu_ref, wd_ref, o_ref, *, n_chunks):
    x = x_ref[...]
    d_ff = wd_ref.shape[0]
    ch = d_ff // n_chunks
    h = _rms(x, g_ref[...]).astype(BF16)
    out = x
    for c in range(n_chunks):
        gate = _dot(h, wgu_ref[:, c * ch:(c + 1) * ch])
        up = _dot(h, wgu_ref[:, d_ff + c * ch:d_ff + (c + 1) * ch])
        out = out + _dot((jax.nn.silu(gate) * up).astype(BF16), wd_ref[c * ch:(c + 1) * ch, :])
    o_ref[...] = out


def _ffn(x2, g, wgu, wd, tm):
    t, d = x2.shape
    d_ff = wd.shape[0]
    n_chunks = 2 if d_ff % (2 * LANES) == 0 else 1
    return pl.pallas_call(
        functools.partial(_ffn_kernel, n_chunks=n_chunks),
        grid=(t // tm,),
        in_specs=[
            pl.BlockSpec((tm, d), lambda i: (i, 0)),
            pl.BlockSpec((1, d), lambda i: (0, 0)),
            pl.BlockSpec(wgu.shape, lambda i: (0, 0)),
            pl.BlockSpec(wd.shape, lambda i: (0, 0)),
        ],
        out_specs=pl.BlockSpec((tm, d), lambda i: (i, 0)),
        out_shape=jax.ShapeDtypeStruct((t, d), F32),
        compiler_params=pltpu.CompilerParams(
            dimension_semantics=("parallel",), vmem_limit_bytes=VMEM_LIMIT),
        name="swiglu",
    )(x2, g, wgu, wd)


def _dup_groups(w):
    d = w.shape[0]
    w = w.reshape(d, NSA_KV_GROUPS, 1, HEAD_DIM)
    return jnp.broadcast_to(w, (d, NSA_KV_GROUPS, 2, HEAD_DIM)).reshape(d, NSA_KV_GROUPS * PAIR)


def _half_groups(w):
    d = w.shape[0]
    w = w.reshape(d, NSA_KV_GROUPS, HEAD_DIM)
    return jnp.pad(w, ((0, 0), (0, 0), (0, HEAD_DIM))).reshape(d, NSA_KV_GROUPS * PAIR)


def _pair_gain(g, scale=1.0):
    return jnp.concatenate([g, g]) * scale


def _layer_params(w_in, moba_q_norm, moba_k_norm, nsa_q_norm, nsa_k_norm, nsa_cmp_pos, nsa_cmp_w1, nsa_cmp_w2):
    d = w_in.shape[0]
    sbw, mbw, nsw, kvw = SB_HEADS * HEAD_DIM, MOBA_HEADS * HEAD_DIM, NSA_HEADS * HEAD_DIM, NSA_KV_GROUPS * HEAD_DIM
    sb_q, sb_k, sb_v = (w_in[:, i * sbw:(i + 1) * sbw] for i in range(3))
    o = 3 * sbw
    mb_q, mb_k, mb_v = (w_in[:, o + i * mbw:o + (i + 1) * mbw] for i in range(3))
    o += 3 * mbw
    ns_q = w_in[:, o:o + nsw]
    o += nsw
    kc_w, vc_w, ks_w, vs_w, kw_w, vw_w = (w_in[:, o + i * kvw:o + (i + 1) * kvw] for i in range(6))
    o += 6 * kvw
    gate_w = w_in[:, o:o + N_BRANCHES * NSA_HEADS].reshape(d, N_BRANCHES, NSA_KV_GROUPS, NSA_HPG)
    o += N_BRANCHES * NSA_HEADS
    wg = w_in[:, o:].astype(BF16)
    w = jnp.concatenate([sb_k, mb_k, _dup_groups(ks_w), _dup_groups(kw_w),
                         sb_q, mb_q, ns_q, sb_v, mb_v, _half_groups(vs_w), _half_groups(vw_w)], axis=1).astype(BF16)
    gate_cols = []
    for g in range(NSA_KV_GROUPS):
        cols = gate_w[:, :, g, :].reshape(d, N_BRANCHES * NSA_HPG)
        gate_cols.append(jnp.pad(cols, ((0, 0), (0, LANES - N_BRANCHES * NSA_HPG))))
    wa = jnp.concatenate([kc_w, vc_w] + gate_cols, axis=1).astype(BF16)
    scale = HEAD_DIM ** -0.5
    ones = jnp.ones((LANES,), F32)
    gains = [ones] * 2 + [_pair_gain(moba_k_norm)] * 2 + [_pair_gain(nsa_k_norm[1])] * 2 + [_pair_gain(nsa_k_norm[2])] * 2
    gains += [ones * (scale * LOG2E)] * 2 + [_pair_gain(moba_q_norm, scale * LOG2E)] * 2
    gains += [_pair_gain(nsa_q_norm, scale * LOG2E)] * 4 + [ones] * 8
    gain = jnp.concatenate(gains).reshape(1, (N_K_BLOCKS + N_F_BLOCKS) * LANES)

    half = CMP_BLOCK // 2
    hid = CMP_HIDDEN

    def spread(w1):
        w1 = w1.reshape(2, half, HEAD_DIM, hid)
        cols = []
        for part in range(2):
            for g in range(NSA_KV_GROUPS):
                z = jnp.zeros((half, NSA_KV_GROUPS, HEAD_DIM, hid), F32).at[:, g].set(w1[part])
                cols.append(z.reshape(half * kvw, hid))
        return jnp.concatenate(cols, axis=1).astype(BF16)

    def spread_pos(p):
        p = p.reshape(2, half, 1, HEAD_DIM)
        return jnp.broadcast_to(p, (2, half, NSA_KV_GROUPS, HEAD_DIM)).reshape(2, half * kvw)

    pos = jnp.concatenate([spread_pos(nsa_cmp_pos[0]), spread_pos(nsa_cmp_pos[1])], axis=0)
    w2 = jnp.stack([jnp.concatenate([nsa_cmp_w2[0], nsa_cmp_w2[0]], axis=-1),
                    jnp.pad(nsa_cmp_w2[1], ((0, 0), (0, HEAD_DIM)))]).astype(BF16)
    kn = _pair_gain(nsa_k_norm[0]).reshape(1, LANES)
    return dict(w=w, wa=wa, gain=gain, wg=wg, wk=spread(nsa_cmp_w1[0]), wv=spread(nsa_cmp_w1[1]),
                pos=pos, w2=w2, kn=kn)


def _bias_tables(rel_bias):
    tbl = jnp.pad(rel_bias.T, ((0, 0), (0, LANES - N_BUCKETS)))

    def rows(t, n):
        t = t.reshape(n, -1, LANES)
        return jnp.pad(t, ((0, 0), (0, SUBLANES - t.shape[1]), (0, 0)))

    return rows(tbl[:MOBA_HEADS], MOBA_HEADS // 2), rows(tbl[MOBA_HEADS:], NSA_KV_GROUPS)


def kernel(x, rel_bias, attn_norm, w_in, moba_q_norm, moba_k_norm, nsa_q_norm, nsa_k_norm, nsa_cmp_pos,
           nsa_cmp_w1, nsa_cmp_w2, w_branch, w_out, ffn_norm, w_gate_up, w_down):
    b, s, d = x.shape
    t = b * s
    tm = min(512, t)
    tq_pair = 512
    tq_nsa = 256
    tk_sb = 256
    assert s % tq_pair == 0 and tq_pair % MOBA_BLOCK == 0 and tq_pair % tk_sb == 0 and WINDOW % tq_nsa == 0
    tbl_moba, tbl_nsa = _bias_tables(rel_bias)
    x2 = x.reshape(t, d)
    for layer in range(w_in.shape[0]):
        p = _layer_params(w_in[layer], moba_q_norm[layer], moba_k_norm[layer], nsa_q_norm[layer],
                          nsa_k_norm[layer], nsa_cmp_pos[layer], nsa_cmp_w1[layer], nsa_cmp_w2[layer])
        g_attn = attn_norm[layer].reshape(1, d)
        kmat, feat, kc_in, vc_in, gates_t = _inproj(x2, g_attn, p["w"], p["wa"], p["gain"], tm)
        o_a = _sb_attention(kmat, feat, b, s, tq_pair, tk_sb)
        o_b = _moba_attention(kmat, feat, tbl_moba, b, s, tq_pair)
        chunk = CMP_STRIDE * LANES
        kc, vc = _compress(kc_in.reshape(b, s // CMP_STRIDE, chunk), vc_in.reshape(b, s // CMP_STRIDE, chunk),
                           p["wk"], p["wv"], p["pos"], p["w2"], p["kn"])
        o_c = _nsa_attention(kmat, feat, kc, vc, gates_t, tbl_nsa, b, s, tq_nsa)
        x2 = _merge(x2, g_attn, o_a, o_b, o_c, p["wg"], w_branch[layer].astype(BF16),
                    w_out[layer].astype(BF16), tm)
        x2 = _ffn(x2, ffn_norm[layer].reshape(1, d), w_gate_up[layer].astype(BF16),
                  w_down[layer].astype(BF16), tm)
    return x2.reshape(b, s, d)
```

```python
import functools
import math

import jax
import jax.numpy as jnp
from jax import lax
from jax.experimental import pallas as pl
from jax.experimental.pallas import tpu as pltpu

HEAD_DIM = 64
SB_HEADS = 4
MOBA_HEADS = 4
NSA_HEADS = 8
NSA_KV_GROUPS = 2
NSA_HPG = NSA_HEADS // NSA_KV_GROUPS
N_BRANCHES = 3
MOBA_BLOCK = 256
MOBA_TOPK = 3
CMP_BLOCK = 32
CMP_STRIDE = 16
CMP_HIDDEN = 4 * HEAD_DIM
SLC_BLOCK = 64
SLC_TOPN = 16
WINDOW = 512
N_BUCKETS = 32
REL_MAX_DISTANCE = 128
NORM_EPS = 1e-6
NEG = -1e30
BIG = 1e30
TINY = 1e-30
LOG2E = math.log2(math.e)
MOBA_FAR_STATES = 4
NSA_FAR_GROUP = 4
NSA_SEL_STATES = NSA_FAR_GROUP + 2
SB_UNDERFLOW_LOG2 = 160.0

LANES = 128
SUBLANES = 8
COL_BLOCK = 256
SCORE_LOOKAHEAD = 4
ONES_ROWS = 16
PAIR = 2 * HEAD_DIM
VMEM_LIMIT = 56 * 1024 * 1024

F32 = jnp.float32
BF16 = jnp.bfloat16

K_SB, K_MB, K_NS_SLC, K_NS_WIN = 0, 2, 4, 6
N_K_BLOCKS = 8
F_SB_Q, F_MB_Q, F_NS_Q, F_SB_V, F_MB_V, F_NS_SLC_V, F_NS_WIN_V = 0, 2, 4, 8, 10, 12, 14
N_F_BLOCKS = 16
K_MODES = (0, 0, 2, 2, 2, 2, 2, 2)
F_MODES = (1, 1, 2, 2, 2, 2, 2, 2, 0, 0, 0, 0, 3, 3, 3, 3)


SLC_SHIFT = SLC_BLOCK.bit_length() - 1
MOBA_SHIFT = MOBA_BLOCK.bit_length() - 1
assert 1 << SLC_SHIFT == SLC_BLOCK and 1 << MOBA_SHIFT == MOBA_BLOCK


def _ceil_to(n, m):
    return -(-n // m) * m


def _dot(a, b):
    return jnp.dot(a, b, preferred_element_type=F32)


def _dot_hilo_rhs(a, b):
    hi = b.astype(BF16)
    lo = (b - hi.astype(F32)).astype(BF16)
    return _dot(a, hi) + _dot(a, lo)


def _rms(x, g):
    ms = jnp.mean(x * x, axis=-1, keepdims=True)
    return x * lax.rsqrt(ms + NORM_EPS) * g


def _stack_heads_t(q_t):
    tq = q_t.shape[1]
    lo = lax.broadcasted_iota(jnp.int32, (LANES, tq), 0) < HEAD_DIM
    zero = jnp.zeros((LANES, tq), q_t.dtype)
    parts = []
    for c in range(q_t.shape[0] // LANES):
        blk = q_t[c * LANES:(c + 1) * LANES, :]
        parts.append(jnp.where(lo, blk, zero))
        parts.append(jnp.where(lo, zero, blk))
    return jnp.concatenate(parts, axis=1)


def _t5_bucket(dist):
    n = jnp.maximum(dist, 0)
    max_exact = N_BUCKETS // 2
    nf = jnp.maximum(n, 1).astype(F32)
    large = max_exact + (jnp.log(nf / max_exact) / math.log(REL_MAX_DISTANCE / max_exact)
                         * (N_BUCKETS - max_exact)).astype(jnp.int32)
    large = jnp.minimum(large, N_BUCKETS - 1)
    return jnp.where(n < max_exact, n, large)


def _bias_tile_t(tbl_ref, n_heads, dist):
    nk, tq = dist.shape
    bucket = _t5_bucket(dist)
    cols = []
    for h in range(n_heads):
        t = jnp.broadcast_to(tbl_ref[h:h + 1, :], (nk, LANES))
        far = t[:, N_BUCKETS - 1:N_BUCKETS]
        for c in range(tq // LANES):
            cols.append((jnp.take_along_axis(t, bucket[:, c * LANES:(c + 1) * LANES], axis=1) - far) * LOG2E)
    return jnp.concatenate(cols, axis=1)


def _top_n_mask_t(score, n, between=None):
    row = lax.broadcasted_iota(jnp.int32, score.shape, 0).astype(F32)
    sel = jnp.zeros(score.shape, F32)
    s = score
    for _ in range(n):
        m = jnp.max(s, axis=0, keepdims=True)
        idx = jnp.min(jnp.where(s == m, row, float(score.shape[0])), axis=0, keepdims=True)
        pick = row == idx
        sel = jnp.where(pick, 1.0, sel)
        s = jnp.where(pick, -jnp.inf, s)
        if between is not None:
            between()
    return sel


def _post(yb, lo, mode, gain):
    if mode == 2:
        sq = yb * yb
        s_lo = jnp.sum(jnp.where(lo, sq, 0.0), axis=-1, keepdims=True)
        s_hi = jnp.sum(jnp.where(lo, 0.0, sq), axis=-1, keepdims=True)
        yb = yb * lax.rsqrt(jnp.where(lo, s_lo, s_hi) * (1.0 / HEAD_DIM) + NORM_EPS)
    if mode in (1, 2):
        yb = yb * gain
    return yb


def _inproj_kernel(x_ref, g_ref, w_ref, wa_ref, gain_ref, k_ref, f_ref, kc_ref, vc_ref, gate_ref):
    tm = x_ref.shape[0]
    h = _rms(x_ref[...], g_ref[...]).astype(BF16)
    lo = lax.broadcasted_iota(jnp.int32, (tm, LANES), 1) < HEAD_DIM
    for c in range((N_K_BLOCKS + N_F_BLOCKS) // 2):
        y = _dot(h, w_ref[:, c * 2 * LANES:(c + 1) * 2 * LANES])
        for s in range(2):
            blk = 2 * c + s
            gain = gain_ref[:, blk * LANES:(blk + 1) * LANES]
            yb = y[:, s * LANES:(s + 1) * LANES]
            if blk < N_K_BLOCKS:
                k_ref[:, blk * LANES:(blk + 1) * LANES] = _post(yb, lo, K_MODES[blk], gain).astype(BF16)
            else:
                fb = blk - N_K_BLOCKS
                yb = jnp.where(lo, yb, 1.0) if F_MODES[fb] == 3 else _post(yb, lo, F_MODES[fb], gain)
                f_ref[fb * LANES:(fb + 1) * LANES, :] = yb.T.astype(BF16)
    aux = _dot(h, wa_ref[...])
    kc_ref[...] = aux[:, 0:LANES]
    vc_ref[...] = aux[:, LANES:2 * LANES]
    for g in range(NSA_KV_GROUPS):
        gate_ref[g * LANES:(g + 1) * LANES, :] = jax.nn.sigmoid(aux[:, (2 + g) * LANES:(3 + g) * LANES]).T


def _inproj(x2, g, w, wa, gain, tm):
    t, d = x2.shape
    nw = w.shape[1]
    return pl.pallas_call(
        _inproj_kernel,
        grid=(t // tm,),
        in_specs=[
            pl.BlockSpec((tm, d), lambda i: (i, 0)),
            pl.BlockSpec((1, d), lambda i: (0, 0)),
            pl.BlockSpec((d, nw), lambda i: (0, 0)),
            pl.BlockSpec((d, 4 * LANES), lambda i: (0, 0)),
            pl.BlockSpec((1, nw), lambda i: (0, 0)),
        ],
        out_specs=[
            pl.BlockSpec((tm, N_K_BLOCKS * LANES), lambda i: (i, 0)),
            pl.BlockSpec((N_F_BLOCKS * LANES, tm), lambda i: (0, i)),
            pl.BlockSpec((tm, LANES), lambda i: (i, 0)),
            pl.BlockSpec((tm, LANES), lambda i: (i, 0)),
            pl.BlockSpec((NSA_KV_GROUPS * LANES, tm), lambda i: (0, i)),
        ],
        out_shape=[
            jax.ShapeDtypeStruct((t, N_K_BLOCKS * LANES), BF16),
            jax.ShapeDtypeStruct((N_F_BLOCKS * LANES, t), BF16),
            jax.ShapeDtypeStruct((t, LANES), F32),
            jax.ShapeDtypeStruct((t, LANES), F32),
            jax.ShapeDtypeStruct((NSA_KV_GROUPS * LANES, t), F32),
        ],
        compiler_params=pltpu.CompilerParams(
            dimension_semantics=("parallel",), vmem_limit_bytes=VMEM_LIMIT),
        name="inproj",
    )(x2, g, w, wa, gain)


def _sb_kernel(q_ref, k_ref, v_ref, o_ref, r_ref, acc_ref, *, tk):
    tq = q_ref.shape[1]
    cols = 2 * tq
    nsub = tq // tk
    qi = pl.program_id(2)
    qs = _stack_heads_t(q_ref[...])
    j_i = lax.broadcasted_iota(jnp.int32, (tk, cols), 0)
    t_i = lax.broadcasted_iota(jnp.int32, (tk, cols), 1) & (tq - 1)
    after = jnp.where((lax.broadcasted_iota(jnp.int32, (tk, 2 * tk), 1) & (tk - 1))
                      > lax.broadcasted_iota(jnp.int32, (tk, 2 * tk), 0), 1.0, 0.0).astype(BF16)
    r_ref[...] = jnp.zeros(r_ref.shape, F32)
    acc_ref[...] = jnp.zeros(acc_ref.shape, F32)

    def block(kb, strict):
        start = pl.multiple_of(kb * tk, tk)
        z = _dot(k_ref[pl.ds(start, tk), :], qs)
        sp = jnp.maximum(z, 0.0) + jnp.log2(1.0 + jnp.exp2(-jnp.abs(z)))
        stay = sp if strict is None else jnp.where(strict, sp, 0.0)
        hi = stay.astype(BF16)
        lo = (stay - hi.astype(F32)).astype(BF16)
        after_sum = _dot(after, jnp.concatenate([hi, lo], axis=0))
        a = jnp.exp2(z - sp - after_sum - r_ref[...])
        if strict is not None:
            a = jnp.where(strict, a, 0.0)
        acc_ref[...] += _dot(v_ref[:, pl.ds(start, tk)], a.astype(BF16))
        r_ref[...] += jnp.sum(stay, axis=0, keepdims=True)

    for u in range(nsub - 1, -1, -1):
        block(qi * nsub + u, u * tk + j_i < t_i)

    def more(carry):
        j, r_min = carry
        return (j < qi * nsub) & (r_min < SB_UNDERFLOW_LOG2)

    def body(carry):
        j, _ = carry
        block(qi * nsub - 1 - j, None)
        return j + 1, jnp.min(r_ref[...])

    lax.while_loop(more, body, (0, jnp.min(r_ref[...])))
    lo = lax.broadcasted_iota(jnp.int32, (LANES, tq), 0) < HEAD_DIM
    acc = acc_ref[...]
    o_ref[...] = jnp.where(lo, acc[:, :tq], acc[:, tq:]).T.astype(o_ref.dtype)


def _sb_attention(kmat, feat, b, s, tq, tk):
    t = b * s
    nq = s // tq
    return pl.pallas_call(
        functools.partial(_sb_kernel, tk=tk),
        grid=(b, SB_HEADS // 2, nq),
        in_specs=[
            pl.BlockSpec((LANES, tq), lambda bi, p, i: (F_SB_Q + p, bi * nq + i)),
            pl.BlockSpec((s, LANES), lambda bi, p, i: (bi, K_SB + p)),
            pl.BlockSpec((LANES, s), lambda bi, p, i: (F_SB_V + p, bi)),
        ],
        out_specs=pl.BlockSpec((tq, LANES), lambda bi, p, i: (bi * nq + i, p)),
        out_shape=jax.ShapeDtypeStruct((t, SB_HEADS * HEAD_DIM), BF16),
        scratch_shapes=[pltpu.VMEM((1, 2 * tq), F32), pltpu.VMEM((LANES, 2 * tq), F32)],
        compiler_params=pltpu.CompilerParams(
            dimension_semantics=("parallel", "parallel", "arbitrary"), vmem_limit_bytes=VMEM_LIMIT),
        name="stickbreak",
    )(feat, kmat, feat)


def _softmax_init(m_ref, acc_ref, n_states):
    m_ref[:n_states] = jnp.full((n_states,) + m_ref.shape[1:], NEG, F32)
    acc_ref[:n_states] = jnp.zeros((n_states,) + acc_ref.shape[1:], F32)


def _softmax_set(m_ref, acc_ref, s, logits, v_t):
    m = jnp.maximum(jnp.max(logits, axis=0, keepdims=True), NEG)
    acc_ref[s] = _dot(v_t, jnp.exp2(logits - m).astype(BF16))
    m_ref[s] = m


def _softmax_update(m_ref, acc_ref, s, logits, v_t):
    m_old = m_ref[s]
    m_new = jnp.maximum(m_old, jnp.max(logits, axis=0, keepdims=True))
    p = jnp.exp2(logits - m_new)
    acc_ref[s] = jnp.exp2(m_old - m_new) * acc_ref[s] + _dot(v_t, p.astype(BF16))
    m_ref[s] = m_new


def _softmax_merge(m_ref, acc_ref, states):
    m = m_ref[states[0]]
    for s in states[1:]:
        m = jnp.maximum(m, m_ref[s])
    acc = None
    for s in states:
        w = jnp.exp2(m_ref[s] - m)
        acc = w * acc_ref[s] if acc is None else acc + w * acc_ref[s]
    return acc


def _hide(live):
    return jnp.where(live, 0.0, -jnp.inf)


def _moba_kernel(q_ref, k_ref, v_ref, tbl_ref, o_ref, km_ref, bias_ref, sel_ref, m_ref, acc_ref, *, nblk):
    tk = MOBA_BLOCK
    tq = q_ref.shape[1]
    nsub = tq // tk
    cols = 2 * tq
    qi = pl.program_id(2)

    @pl.when(qi == 0)
    def _():
        km_ref[...] = jnp.zeros(km_ref.shape, F32)

        def mean_body(n, carry):
            kb = k_ref[pl.ds(pl.multiple_of(n * tk, tk), tk), :].astype(F32)
            km_ref[pl.ds(n, 1), :] = jnp.mean(kb, axis=0, keepdims=True)
            return carry

        lax.fori_loop(0, nblk, mean_body, 0)
        j = lax.broadcasted_iota(jnp.int32, (tk, tq), 0)
        i = lax.broadcasted_iota(jnp.int32, (tk, tq), 1)
        for e in range(-1, nsub):
            bias_ref[e + 1] = _bias_tile_t(tbl_ref, 2, i - tk * e - j)

    qs = _stack_heads_t(q_ref[...])
    sub_s = lax.broadcasted_iota(jnp.int32, (LANES, cols), 1) & (tq - 1)
    past = lax.broadcasted_iota(jnp.int32, (LANES, cols), 0) < nsub * qi + (sub_s >> MOBA_SHIFT)
    score = jnp.where(past, _dot(km_ref[...].astype(BF16), qs), NEG)
    sel_ref[...] = jnp.where(past & (_top_n_mask_t(score, MOBA_TOPK) > 0.5), 0.0, -jnp.inf)
    _softmax_init(m_ref, acc_ref, MOBA_FAR_STATES)
    ones = jnp.ones((ONES_ROWS, tk), BF16)

    def tiles(*specs, fresh=False):
        units = []
        for state, n, fn in specs:
            start = pl.multiple_of(n * tk, tk)
            units += [(state, start, fn, slice(c, c + COL_BLOCK)) for c in range(0, cols, COL_BLOCK)]

        def score(unit):
            _, start, fn, cs = unit
            return fn(_dot(k_ref[pl.ds(start, tk), :], qs[:, cs]), cs)

        def fold(unit, logits):
            state, start, _, cs = unit
            v_t = jnp.concatenate([v_ref[:, pl.ds(start, tk)], ones], axis=0)
            (_softmax_set if fresh else _softmax_update)(m_ref, acc_ref, (state, slice(None), cs), logits, v_t)

        pending = []
        for unit in units:
            pending.append((unit, score(unit)))
            if len(pending) > SCORE_LOOKAHEAD:
                fold(*pending.pop(0))
        for item in pending:
            fold(*item)

    def hidden(n, cs, live=None):
        row = sel_ref[pl.ds(n, 1), cs]
        if live is not None:
            row = row + _hide(live)
        return jnp.broadcast_to(row, (tk, COL_BLOCK))

    n_far = jnp.maximum(nsub * qi - 1, 0)

    group = MOBA_FAR_STATES

    def far_tile(i, u):
        n = group * i + u
        if u == 0:
            return (u, n, lambda z, cs: z + hidden(n, cs))
        clamped = jnp.minimum(n, n_far - 1)
        return (u, clamped, lambda z, cs: z + hidden(clamped, cs, live=n < n_far))

    def far_body(i, carry):
        tiles(*[far_tile(i, u) for u in range(group)])
        return carry

    lax.fori_loop(0, (n_far + group - 1) // group, far_body, 0)
    prev = jnp.maximum(nsub * qi - 1, 0)
    j_i = lax.broadcasted_iota(jnp.int32, (tk, cols), 0)
    t_i = lax.broadcasted_iota(jnp.int32, (tk, cols), 1) & (tq - 1)

    def own_tile(e):
        causal_own = ((t_i >> MOBA_SHIFT) == e) & (j_i <= t_i - tk * e)
        return (MOBA_FAR_STATES + 1 + e, nsub * qi + e,
                lambda z, cs: z + bias_ref[e + 1, :, cs] + jnp.where(causal_own[:, cs], 0.0, hidden(nsub * qi + e, cs)))

    tiles((MOBA_FAR_STATES, prev, lambda z, cs: z + bias_ref[0, :, cs] + hidden(prev, cs, live=qi >= 1)),
          *[own_tile(e) for e in range(nsub)], fresh=True)
    acc = _softmax_merge(m_ref, acc_ref, list(range(MOBA_FAR_STATES + 1 + nsub)))
    out = acc[:LANES] / jnp.maximum(acc[LANES:LANES + 1], TINY)
    lo = lax.broadcasted_iota(jnp.int32, (LANES, tq), 0) < HEAD_DIM
    o_ref[...] = jnp.where(lo, out[:, :tq], out[:, tq:]).T.astype(o_ref.dtype)


def _moba_attention(kmat, feat, tbl, b, s, tq):
    t = b * s
    tk = MOBA_BLOCK
    nq = s // tq
    cols = 2 * tq
    n_states = MOBA_FAR_STATES + 1 + tq // tk
    return pl.pallas_call(
        functools.partial(_moba_kernel, nblk=s // tk),
        grid=(b, MOBA_HEADS // 2, nq),
        in_specs=[
            pl.BlockSpec((LANES, tq), lambda bi, p, i: (F_MB_Q + p, bi * nq + i)),
            pl.BlockSpec((s, LANES), lambda bi, p, i: (bi, K_MB + p)),
            pl.BlockSpec((LANES, s), lambda bi, p, i: (F_MB_V + p, bi)),
            pl.BlockSpec((None, SUBLANES, LANES), lambda bi, p, i: (p, 0, 0)),
        ],
        out_specs=pl.BlockSpec((tq, LANES), lambda bi, p, i: (bi * nq + i, p)),
        out_shape=jax.ShapeDtypeStruct((t, MOBA_HEADS * HEAD_DIM), BF16),
        scratch_shapes=[
            pltpu.VMEM((LANES, LANES), F32),
            pltpu.VMEM((tq // tk + 1, tk, cols), F32),
            pltpu.VMEM((LANES, cols), F32),
            pltpu.VMEM((n_states, 1, cols), F32),
            pltpu.VMEM((n_states, LANES + ONES_ROWS, cols), F32),
        ],
        compiler_params=pltpu.CompilerParams(
            dimension_semantics=("arbitrary", "arbitrary", "arbitrary"), vmem_limit_bytes=VMEM_LIMIT),
        name="moba",
    )(feat, kmat, feat, tbl)


def _compress_kernel(rk_ref, rv_ref, wk_ref, wv_ref, pos_ref, w2_ref, kn_ref, kc_ref, vc_ref):
    n = rk_ref.shape[0]
    hid = CMP_HIDDEN
    lo = lax.broadcasted_iota(jnp.int32, (n, LANES), 1) < HEAD_DIM
    for idx, (r_ref, w_ref) in enumerate(((rk_ref, wk_ref), (rv_ref, wv_ref))):
        r = r_ref[...]
        first = _dot((r + pos_ref[2 * idx:2 * idx + 1, :]).astype(BF16), w_ref[:, :2 * hid])
        second = _dot((r + pos_ref[2 * idx + 1:2 * idx + 2, :]).astype(BF16), w_ref[:, 2 * hid:])
        h = jax.nn.gelu(first + pltpu.roll(second, shift=n - 1, axis=0))
        for g in range(NSA_KV_GROUPS):
            y = _dot(h[:, g * hid:(g + 1) * hid].astype(BF16), w2_ref[idx])
            if idx == 0:
                kc_ref[g] = _rms(y, kn_ref[...]).astype(kc_ref.dtype)
            else:
                vc_ref[g] = jnp.where(lo, y, 1.0).T.astype(vc_ref.dtype)


def _compress(rk, rv, wk, wv, pos, w2, kn):
    b, n, width = rk.shape
    return pl.pallas_call(
        _compress_kernel,
        grid=(b,),
        in_specs=[
            pl.BlockSpec((None, n, width), lambda i: (i, 0, 0)),
            pl.BlockSpec((None, n, width), lambda i: (i, 0, 0)),
            pl.BlockSpec(wk.shape, lambda i: (0, 0)),
            pl.BlockSpec(wv.shape, lambda i: (0, 0)),
            pl.BlockSpec(pos.shape, lambda i: (0, 0)),
            pl.BlockSpec(w2.shape, lambda i: (0, 0, 0)),
            pl.BlockSpec(kn.shape, lambda i: (0, 0)),
        ],
        out_specs=[
            pl.BlockSpec((None, NSA_KV_GROUPS, n, LANES), lambda i: (i, 0, 0, 0)),
            pl.BlockSpec((None, NSA_KV_GROUPS, LANES, n), lambda i: (i, 0, 0, 0)),
        ],
        out_shape=[
            jax.ShapeDtypeStruct((b, NSA_KV_GROUPS, n, LANES), BF16),
            jax.ShapeDtypeStruct((b, NSA_KV_GROUPS, LANES, n), BF16),
        ],
        compiler_params=pltpu.CompilerParams(
            dimension_semantics=("parallel",), vmem_limit_bytes=VMEM_LIMIT),
        name="nsa_compress",
    )(rk, rv, wk, wv, pos, w2, kn)


def _nsa_kernel(q_ref, kc_ref, vc_ref, ks_ref, vs_ref, kw_ref, vw_ref, gate_ref, tbl_ref, o_ref,
                bdiag_ref, bnear_ref, ov_ref, sel_ref, m_ref, acc_ref, tot_ref, lc_ref):
    tq = q_ref.shape[1]
    nh = NSA_HPG
    cols = nh * tq
    ncmp = kc_ref.shape[0]
    qi = pl.program_id(2)
    q0 = qi * tq

    @pl.when(qi == 0)
    def _():
        j = lax.broadcasted_iota(jnp.int32, (tq, tq), 0)
        i = lax.broadcasted_iota(jnp.int32, (tq, tq), 1)
        bdiag_ref[...] = _bias_tile_t(tbl_ref, nh, i - j)
        bnear_ref[...] = _bias_tile_t(tbl_ref, nh, tq + i - j)
        ss = lax.broadcasted_iota(jnp.int32, (LANES, ncmp), 0) * SLC_BLOCK
        cs = lax.broadcasted_iota(jnp.int32, (LANES, ncmp), 1) * CMP_STRIDE
        ov_ref[...] = jnp.where((cs < ss + SLC_BLOCK) & (cs + CMP_BLOCK > ss), 1.0, 0.0).astype(BF16)

    qs = _stack_heads_t(q_ref[...])

    def gate(branch):
        return jnp.concatenate([gate_ref[branch * nh + h:branch * nh + h + 1, :] for h in range(nh)], axis=1)

    def result(states):
        acc = _softmax_merge(m_ref, acc_ref, states)
        return acc[:HEAD_DIM] / jnp.maximum(acc[HEAD_DIM:], TINY)

    j_i = lax.broadcasted_iota(jnp.int32, (tq, cols), 0)
    i_i = lax.broadcasted_iota(jnp.int32, (tq, cols), 1) & (tq - 1)
    causal = j_i <= i_i
    per_tile = tq // SLC_BLOCK
    nwin = WINDOW // tq
    n_sel_states = NSA_SEL_STATES
    group = NSA_FAR_GROUP
    _softmax_init(m_ref, acc_ref, group)

    def tile_steps(*specs, fresh=False):
        units = []
        for state, k_ref, v_ref, kb, fn in specs:
            start = pl.multiple_of(kb * tq, tq)
            units += [(state, k_ref, v_ref, start, fn, slice(c, c + COL_BLOCK)) for c in range(0, cols, COL_BLOCK)]

        def score(unit):
            _, k_ref, _, start, fn, cs = unit
            return fn(_dot(k_ref[pl.ds(start, tq), :], qs[:, cs]), cs)

        def fold(unit, logits):
            state, _, v_ref, start, _, cs = unit
            (_softmax_set if fresh else _softmax_update)(
                m_ref, acc_ref, (state, slice(None), cs), logits, v_ref[:, pl.ds(start, tq)])

        pending = []
        for unit in units:
            pending.append((unit, score(unit)))
            if len(pending) > SCORE_LOOKAHEAD:
                fold(*pending.pop(0))
                yield
        for item in pending:
            fold(*item)
            yield

    def tiles(*specs, fresh=False):
        for _ in tile_steps(*specs, fresh=fresh):
            pass

    def window_tile(r):
        def fn(z, cs):
            if r == 1:
                z = z + bnear_ref[:, cs]
            if r == nwin:
                z = jnp.where((j_i > i_i)[:, cs], z, -jnp.inf)
            return z + _hide(qi >= r)
        return (n_sel_states + r, kw_ref, vw_ref, jnp.maximum(qi - r, 0), fn)

    window_steps = tile_steps(
        *[window_tile(r) for r in range(nwin, 0, -1)],
        (n_sel_states, kw_ref, vw_ref, qi, lambda z, cs: jnp.where(causal[:, cs], z + bdiag_ref[:, cs], -jnp.inf)),
        fresh=True)

    n_i = lax.broadcasted_iota(jnp.int32, (ncmp, cols), 0)
    t_i = lax.broadcasted_iota(jnp.int32, (ncmp, cols), 1) & (tq - 1)
    valid_c = q0 + t_i - (n_i * CMP_STRIDE + CMP_BLOCK - 1) >= 0
    per_tile_c = tq // CMP_STRIDE
    below = _ceil_to(-(-(REL_MAX_DISTANCE + CMP_BLOCK) // CMP_STRIDE), SUBLANES)
    band = below + per_tile_c
    band_start = pl.multiple_of(jnp.maximum(qi * per_tile_c - below, 0), SUBLANES)
    n_b = band_start + lax.broadcasted_iota(jnp.int32, (band, tq), 0)
    dist_b = q0 + lax.broadcasted_iota(jnp.int32, (band, tq), 1) - (n_b * CMP_STRIDE + CMP_BLOCK - 1)
    lc_ref[...] = _dot(kc_ref[...], qs)
    lc_ref[pl.ds(band_start, band), :] += _bias_tile_t(tbl_ref, nh, dist_b)
    logit_c = jnp.where(valid_c, lc_ref[...], -jnp.inf)
    p_c = jnp.exp2(logit_c - jnp.maximum(jnp.max(logit_c, axis=0, keepdims=True), NEG))
    p_c = p_c / jnp.maximum(jnp.sum(p_c, axis=0, keepdims=True), TINY)
    tot_ref[...] = gate(0) * _dot(vc_ref[...], p_c.astype(BF16))[:HEAD_DIM]

    p_sum = p_c[:, 0:tq]
    for h in range(1, nh):
        p_sum = p_sum + p_c[:, h * tq:(h + 1) * tq]
    imp = _dot_hilo_rhs(ov_ref[...], p_sum)
    blk = lax.broadcasted_iota(jnp.int32, (LANES, tq), 0)
    cur = (q0 + lax.broadcasted_iota(jnp.int32, (LANES, tq), 1)) >> SLC_SHIFT
    forced = (blk == 0) | (blk == cur) | (blk == cur - 1)
    score = jnp.where(blk <= cur, jnp.where(forced, BIG, imp), NEG)
    top = _top_n_mask_t(score, SLC_TOPN, between=lambda: next(window_steps, None))
    for _ in window_steps:
        pass
    sel = jnp.where((blk <= cur) & (top > 0.5), 0.0, -jnp.inf)
    sel_ref[...] = jnp.concatenate([sel] * nh, axis=1)

    def hidden(kb, cs, live=None):
        rows = [sel_ref[pl.ds(kb * per_tile + r, 1), cs] for r in range(per_tile)]
        if live is not None:
            rows = [row + _hide(live) for row in rows]
        return jnp.concatenate([jnp.broadcast_to(row, (SLC_BLOCK, COL_BLOCK)) for row in rows], axis=0)

    n_far = jnp.maximum(qi - 1, 0)

    def far_tile(i, u):
        kb = group * i + u
        if u == 0:
            return (u, ks_ref, vs_ref, kb, lambda z, cs: z + hidden(kb, cs))
        clamped = jnp.minimum(kb, n_far - 1)
        return (u, ks_ref, vs_ref, clamped, lambda z, cs: z + hidden(clamped, cs, live=kb < n_far))

    def sel_body(i, carry):
        tiles(*[far_tile(i, u) for u in range(group)])
        return carry

    lax.fori_loop(0, (n_far + group - 1) // group, sel_body, 0)
    near = jnp.maximum(qi - 1, 0)
    tiles((group, ks_ref, vs_ref, near, lambda z, cs: z + bnear_ref[:, cs] + hidden(near, cs, live=qi >= 1)),
          (group + 1, ks_ref, vs_ref, qi,
           lambda z, cs: jnp.where(causal[:, cs], z + bdiag_ref[:, cs] + hidden(qi, cs), -jnp.inf)),
          fresh=True)
    tot = (tot_ref[...] + gate(1) * result(list(range(n_sel_states)))
           + gate(2) * result(list(range(n_sel_states, n_sel_states + nwin + 1))))
    pairs = [jnp.concatenate([tot[:, (2 * c) * tq:(2 * c + 1) * tq], tot[:, (2 * c + 1) * tq:(2 * c + 2) * tq]], axis=0).T
             for c in range(nh // 2)]
    o_ref[...] = jnp.concatenate(pairs, axis=1).astype(o_ref.dtype)


def _nsa_attention(kmat, feat, kc, vc, gates_t, tbl, b, s, tq):
    t = b * s
    nq = s // tq
    ncmp = kc.shape[2]
    cols = NSA_HPG * tq
    n_states = NSA_SEL_STATES + WINDOW // tq + 1
    k_spec = lambda blk: pl.BlockSpec((s, LANES), lambda bi, g, i: (bi, blk + g))
    v_spec = lambda blk: pl.BlockSpec((LANES, s), lambda bi, g, i: (blk + g, bi))
    return pl.pallas_call(
        _nsa_kernel,
        grid=(b, NSA_KV_GROUPS, nq),
        in_specs=[
            pl.BlockSpec((2 * LANES, tq), lambda bi, g, i: (F_NS_Q // 2 + g, bi * nq + i)),
            pl.BlockSpec((None, None, ncmp, LANES), lambda bi, g, i: (bi, g, 0, 0)),
            pl.BlockSpec((None, None, LANES, ncmp), lambda bi, g, i: (bi, g, 0, 0)),
            k_spec(K_NS_SLC), v_spec(F_NS_SLC_V), k_spec(K_NS_WIN), v_spec(F_NS_WIN_V),
            pl.BlockSpec((LANES, tq), lambda bi, g, i: (g, bi * nq + i)),
            pl.BlockSpec((None, SUBLANES, LANES), lambda bi, g, i: (g, 0, 0)),
        ],
        out_specs=pl.BlockSpec((tq, 2 * LANES), lambda bi, g, i: (bi * nq + i, g)),
        out_shape=jax.ShapeDtypeStruct((t, NSA_HEADS * HEAD_DIM), BF16),
        scratch_shapes=[
            pltpu.VMEM((tq, cols), F32),
            pltpu.VMEM((tq, cols), F32),
            pltpu.VMEM((LANES, ncmp), BF16),
            pltpu.VMEM((LANES, cols), F32),
            pltpu.VMEM((n_states, 1, cols), F32),
            pltpu.VMEM((n_states, LANES, cols), F32),
            pltpu.VMEM((HEAD_DIM, cols), F32),
            pltpu.VMEM((ncmp, cols), F32),
        ],
        compiler_params=pltpu.CompilerParams(
            dimension_semantics=("arbitrary", "arbitrary", "arbitrary"), vmem_limit_bytes=VMEM_LIMIT),
        name="nsa",
    )(feat, kc, vc, kmat, feat, kmat, feat, gates_t, tbl)


def _merge_kernel(x_ref, g_ref, oa_ref, ob_ref, oc_ref, wg_ref, wb_ref, wo_ref, o_ref):
    x = x_ref[...]
    d = x.shape[1]
    h = _rms(x, g_ref[...]).astype(BF16)
    mix = None
    row = 0
    for br, src in enumerate((oa_ref, ob_ref, oc_ref)):
        width = src.shape[1]
        gate = jax.nn.sigmoid(_dot(h, wg_ref[:, br * d:(br + 1) * d]))
        term = gate * _dot(src[...], wb_ref[row:row + width, :])
        mix = term if mix is None else mix + term
        row += width
    o_ref[...] = x + _dot(mix.astype(BF16), wo_ref[...])


def _merge(x2, g, oa, ob, oc, wg, wb, wo, tm):
    t, d = x2.shape
    row = lambda w: pl.BlockSpec((tm, w), lambda i: (i, 0))
    full = lambda a: pl.BlockSpec(a.shape, lambda i: (0, 0))
    return pl.pallas_call(
        _merge_kernel,
        grid=(t // tm,),
        in_specs=[row(d), full(g), row(oa.shape[1]), row(ob.shape[1]), row(oc.shape[1]),
                  full(wg), full(wb), full(wo)],
        out_specs=row(d),
        out_shape=jax.ShapeDtypeStruct((t, d), F32),
        compiler_params=pltpu.CompilerParams(
            dimension_semantics=("parallel",), vmem_limit_bytes=VMEM_LIMIT),
        name="merge",
    )(x2, g, oa, ob, oc, wg, wb, wo)


def _ffn_kernel(x_ref, g_ref, wgu_ref, wd_ref, o_ref, *, n_chunks):
    x = x_ref[...]
    d_ff = wd_ref.shape[0]
    ch = d_ff // n_chunks
    h = _rms(x, g_ref[...]).astype(BF16)
    out = x
    for c in range(n_chunks):
        gate = _dot(h, wgu_ref[:, c * ch:(c + 1) * ch])
        up = _dot(h, wgu_ref[:, d_ff + c * ch:d_ff + (c + 1) * ch])
        out = out + _dot((jax.nn.silu(gate) * up).astype(BF16), wd_ref[c * ch:(c + 1) * ch, :])
    o_ref[...] = out


def _ffn(x2, g, wgu, wd, tm):
    t, d = x2.shape
    d_ff = wd.shape[0]
    n_chunks = 2 if d_ff % (2 * LANES) == 0 else 1
    return pl.pallas_call(
        functools.partial(_ffn_kernel, n_chunks=n_chunks),
        grid=(t // tm,),
        in_specs=[
            pl.BlockSpec((tm, d), lambda i: (i, 0)),
            pl.BlockSpec((1, d), lambda i: (0, 0)),
            pl.BlockSpec(wgu.shape, lambda i: (0, 0)),
            pl.BlockSpec(wd.shape, lambda i: (0, 0)),
        ],
        out_specs=pl.BlockSpec((tm, d), lambda i: (i, 0)),
        out_shape=jax.ShapeDtypeStruct((t, d), F32),
        compiler_params=pltpu.CompilerParams(
            dimension_semantics=("parallel",), vmem_limit_bytes=VMEM_LIMIT),
        name="swiglu",
    )(x2, g, wgu, wd)


def _dup_groups(w):
    d = w.shape[0]
    w = w.reshape(d, NSA_KV_GROUPS, 1, HEAD_DIM)
    return jnp.broadcast_to(w, (d, NSA_KV_GROUPS, 2, HEAD_DIM)).reshape(d, NSA_KV_GROUPS * PAIR)


def _half_groups(w):
    d = w.shape[0]
    w = w.reshape(d, NSA_KV_GROUPS, HEAD_DIM)
    return jnp.pad(w, ((0, 0), (0, 0), (0, HEAD_DIM))).reshape(d, NSA_KV_GROUPS * PAIR)


def _pair_gain(g, scale=1.0):
    return jnp.concatenate([g, g]) * scale


def _layer_params(w_in, moba_q_norm, moba_k_norm, nsa_q_norm, nsa_k_norm, nsa_cmp_pos, nsa_cmp_w1, nsa_cmp_w2):
    d = w_in.shape[0]
    sbw, mbw, nsw, kvw = SB_HEADS * HEAD_DIM, MOBA_HEADS * HEAD_DIM, NSA_HEADS * HEAD_DIM, NSA_KV_GROUPS * HEAD_DIM
    sb_q, sb_k, sb_v = (w_in[:, i * sbw:(i + 1) * sbw] for i in range(3))
    o = 3 * sbw
    mb_q, mb_k, mb_v = (w_in[:, o + i * mbw:o + (i + 1) * mbw] for i in range(3))
    o += 3 * mbw
    ns_q = w_in[:, o:o + nsw]
    o += nsw
    kc_w, vc_w, ks_w, vs_w, kw_w, vw_w = (w_in[:, o + i * kvw:o + (i + 1) * kvw] for i in range(6))
    o += 6 * kvw
    gate_w = w_in[:, o:o + N_BRANCHES * NSA_HEADS].reshape(d, N_BRANCHES, NSA_KV_GROUPS, NSA_HPG)
    o += N_BRANCHES * NSA_HEADS
    wg = w_in[:, o:].astype(BF16)
    w = jnp.concatenate([sb_k, mb_k, _dup_groups(ks_w), _dup_groups(kw_w),
                         sb_q, mb_q, ns_q, sb_v, mb_v, _half_groups(vs_w), _half_groups(vw_w)], axis=1).astype(BF16)
    gate_cols = []
    for g in range(NSA_KV_GROUPS):
        cols = gate_w[:, :, g, :].reshape(d, N_BRANCHES * NSA_HPG)
        gate_cols.append(jnp.pad(cols, ((0, 0), (0, LANES - N_BRANCHES * NSA_HPG))))
    wa = jnp.concatenate([kc_w, vc_w] + gate_cols, axis=1).astype(BF16)
    scale = HEAD_DIM ** -0.5
    ones = jnp.ones((LANES,), F32)
    gains = [ones] * 2 + [_pair_gain(moba_k_norm)] * 2 + [_pair_gain(nsa_k_norm[1])] * 2 + [_pair_gain(nsa_k_norm[2])] * 2
    gains += [ones * (scale * LOG2E)] * 2 + [_pair_gain(moba_q_norm, scale * LOG2E)] * 2
    gains += [_pair_gain(nsa_q_norm, scale * LOG2E)] * 4 + [ones] * 8
    gain = jnp.concatenate(gains).reshape(1, (N_K_BLOCKS + N_F_BLOCKS) * LANES)

    half = CMP_BLOCK // 2
    hid = CMP_HIDDEN

    def spread(w1):
        w1 = w1.reshape(2, half, HEAD_DIM, hid)
        cols = []
        for part in range(2):
            for g in range(NSA_KV_GROUPS):
                z = jnp.zeros((half, NSA_KV_GROUPS, HEAD_DIM, hid), F32).at[:, g].set(w1[part])
                cols.append(z.reshape(half * kvw, hid))
        return jnp.concatenate(cols, axis=1).astype(BF16)

    def spread_pos(p):
        p = p.reshape(2, half, 1, HEAD_DIM)
        return jnp.broadcast_to(p, (2, half, NSA_KV_GROUPS, HEAD_DIM)).reshape(2, half * kvw)

    pos = jnp.concatenate([spread_pos(nsa_cmp_pos[0]), spread_pos(nsa_cmp_pos[1])], axis=0)
    w2 = jnp.stack([jnp.concatenate([nsa_cmp_w2[0], nsa_cmp_w2[0]], axis=-1),
                    jnp.pad(nsa_cmp_w2[1], ((0, 0), (0, HEAD_DIM)))]).astype(BF16)
    kn = _pair_gain(nsa_k_norm[0]).reshape(1, LANES)
    return dict(w=w, wa=wa, gain=gain, wg=wg, wk=spread(nsa_cmp_w1[0]), wv=spread(nsa_cmp_w1[1]),
                pos=pos, w2=w2, kn=kn)


def _bias_tables(rel_bias):
    tbl = jnp.pad(rel_bias.T, ((0, 0), (0, LANES - N_BUCKETS)))

    def rows(t, n):
        t = t.reshape(n, -1, LANES)
        return jnp.pad(t, ((0, 0), (0, SUBLANES - t.shape[1]), (0, 0)))

    return rows(tbl[:MOBA_HEADS], MOBA_HEADS // 2), rows(tbl[MOBA_HEADS:], NSA_KV_GROUPS)


def kernel(x, rel_bias, attn_norm, w_in, moba_q_norm, moba_k_norm, nsa_q_norm, nsa_k_norm, nsa_cmp_pos,
           nsa_cmp_w1, nsa_cmp_w2, w_branch, w_out, ffn_norm, w_gate_up, w_down):
    b, s, d = x.shape
    t = b * s
    tm = min(512, t)
    tq_pair = 512
    tq_nsa = 256
    tk_sb = 256
    assert s % tq_pair == 0 and tq_pair % MOBA_BLOCK == 0 and tq_pair % tk_sb == 0 and WINDOW % tq_nsa == 0
    tbl_moba, tbl_nsa = _bias_tables(rel_bias)
    x2 = x.reshape(t, d)
    for layer in range(w_in.shape[0]):
        p = _layer_params(w_in[layer], moba_q_norm[layer], moba_k_norm[layer], nsa_q_norm[layer],
                          nsa_k_norm[layer], nsa_cmp_pos[layer], nsa_cmp_w1[layer], nsa_cmp_w2[layer])
        g_attn = attn_norm[layer].reshape(1, d)
        kmat, feat, kc_in, vc_in, gates_t = _inproj(x2, g_attn, p["w"], p["wa"], p["gain"], tm)
        o_a = _sb_attention(kmat, feat, b, s, tq_pair, tk_sb)
        o_b = _moba_attention(kmat, feat, tbl_moba, b, s, tq_pair)
        chunk = CMP_STRIDE * LANES
        kc, vc = _compress(kc_in.reshape(b, s // CMP_STRIDE, chunk), vc_in.reshape(b, s // CMP_STRIDE, chunk),
                           p["wk"], p["wv"], p["pos"], p["w2"], p["kn"])
        o_c = _nsa_attention(kmat, feat, kc, vc, gates_t, tbl_nsa, b, s, tq_nsa)
        x2 = _merge(x2, g_attn, o_a, o_b, o_c, p["wg"], w_branch[layer].astype(BF16),
                    w_out[layer].astype(BF16), tm)
        x2 = _ffn(x2, ffn_norm[layer].reshape(1, d), w_gate_up[layer].astype(BF16),
                  w_down[layer].astype(BF16), tm)
    return x2.reshape(b, s, d)
```

```python
import functools
import math

import jax
import jax.numpy as jnp
from jax import lax
from jax.experimental import pallas as pl
from jax.experimental.pallas import tpu as pltpu

HEAD_DIM = 64
SB_HEADS = 4
MOBA_HEADS = 4
NSA_HEADS = 8
NSA_KV_GROUPS = 2
NSA_HPG = NSA_HEADS // NSA_KV_GROUPS
N_BRANCHES = 3
MOBA_BLOCK = 256
MOBA_TOPK = 3
CMP_BLOCK = 32
CMP_STRIDE = 16
CMP_HIDDEN = 4 * HEAD_DIM
SLC_BLOCK = 64
SLC_TOPN = 16
WINDOW = 512
N_BUCKETS = 32
REL_MAX_DISTANCE = 128
NORM_EPS = 1e-6
NEG = -1e30
BIG = 1e30
TINY = 1e-30
LOG2E = math.log2(math.e)
MOBA_FAR_STATES = 4
NSA_FAR_GROUP = 4
NSA_SEL_STATES = NSA_FAR_GROUP + 2
SB_UNDERFLOW_LOG2 = 160.0

LANES = 128
SUBLANES = 8
COL_BLOCK = 256
SCORE_LOOKAHEAD = 4
CMP_ROW_STEP = 128
NSA_V_ROWS = HEAD_DIM + 16
ONES_ROWS = 16
PAIR = 2 * HEAD_DIM
VMEM_LIMIT = 56 * 1024 * 1024

F32 = jnp.float32
BF16 = jnp.bfloat16

K_SB, K_MB, K_NS_SLC, K_NS_WIN = 0, 2, 4, 6
N_K_BLOCKS = 8
F_SB_Q, F_MB_Q, F_NS_Q, F_SB_V, F_MB_V, F_NS_SLC_V, F_NS_WIN_V = 0, 2, 4, 8, 10, 12, 14
N_F_BLOCKS = 16
K_MODES = (0, 0, 2, 2, 2, 2, 2, 2)
F_MODES = (1, 1, 2, 2, 2, 2, 2, 2, 0, 0, 0, 0, 3, 3, 3, 3)


SLC_SHIFT = SLC_BLOCK.bit_length() - 1
MOBA_SHIFT = MOBA_BLOCK.bit_length() - 1
assert 1 << SLC_SHIFT == SLC_BLOCK and 1 << MOBA_SHIFT == MOBA_BLOCK


def _ceil_to(n, m):
    return -(-n // m) * m


def _dot(a, b):
    return jnp.dot(a, b, preferred_element_type=F32)


def _dot_hilo_rhs(a, b):
    hi = b.astype(BF16)
    lo = (b - hi.astype(F32)).astype(BF16)
    return _dot(a, hi) + _dot(a, lo)


def _rms(x, g):
    ms = jnp.mean(x * x, axis=-1, keepdims=True)
    return x * lax.rsqrt(ms + NORM_EPS) * g


def _stack_heads_t(q_t):
    tq = q_t.shape[1]
    lo = lax.broadcasted_iota(jnp.int32, (LANES, tq), 0) < HEAD_DIM
    zero = jnp.zeros((LANES, tq), q_t.dtype)
    parts = []
    for c in range(q_t.shape[0] // LANES):
        blk = q_t[c * LANES:(c + 1) * LANES, :]
        parts.append(jnp.where(lo, blk, zero))
        parts.append(jnp.where(lo, zero, blk))
    return jnp.concatenate(parts, axis=1)


def _t5_bucket(dist):
    n = jnp.maximum(dist, 0)
    max_exact = N_BUCKETS // 2
    nf = jnp.maximum(n, 1).astype(F32)
    large = max_exact + (jnp.log(nf / max_exact) / math.log(REL_MAX_DISTANCE / max_exact)
                         * (N_BUCKETS - max_exact)).astype(jnp.int32)
    large = jnp.minimum(large, N_BUCKETS - 1)
    return jnp.where(n < max_exact, n, large)


def _bias_tile_t(tbl_ref, n_heads, dist):
    nk, tq = dist.shape
    bucket = _t5_bucket(dist)
    cols = []
    for h in range(n_heads):
        t = jnp.broadcast_to(tbl_ref[h:h + 1, :], (nk, LANES))
        far = t[:, N_BUCKETS - 1:N_BUCKETS]
        for c in range(tq // LANES):
            cols.append((jnp.take_along_axis(t, bucket[:, c * LANES:(c + 1) * LANES], axis=1) - far) * LOG2E)
    return jnp.concatenate(cols, axis=1)


def _top_n_mask_t(score, n, between=None):
    row = lax.broadcasted_iota(jnp.int32, score.shape, 0).astype(F32)
    sel = jnp.zeros(score.shape, F32)
    s = score
    for _ in range(n):
        m = jnp.max(s, axis=0, keepdims=True)
        idx = jnp.min(jnp.where(s == m, row, float(score.shape[0])), axis=0, keepdims=True)
        pick = row == idx
        sel = jnp.where(pick, 1.0, sel)
        s = jnp.where(pick, -jnp.inf, s)
        if between is not None:
            between()
    return sel


def _post(yb, lo, mode, gain):
    if mode == 2:
        sq = yb * yb
        s_lo = jnp.sum(jnp.where(lo, sq, 0.0), axis=-1, keepdims=True)
        s_hi = jnp.sum(jnp.where(lo, 0.0, sq), axis=-1, keepdims=True)
        yb = yb * lax.rsqrt(jnp.where(lo, s_lo, s_hi) * (1.0 / HEAD_DIM) + NORM_EPS)
    if mode in (1, 2):
        yb = yb * gain
    return yb


def _inproj_kernel(x_ref, g_ref, w_ref, wa_ref, gain_ref, k_ref, f_ref, kc_ref, vc_ref, gate_ref):
    tm = x_ref.shape[0]
    h = _rms(x_ref[...], g_ref[...]).astype(BF16)
    lo = lax.broadcasted_iota(jnp.int32, (tm, LANES), 1) < HEAD_DIM
    for c in range((N_K_BLOCKS + N_F_BLOCKS) // 2):
        y = _dot(h, w_ref[:, c * 2 * LANES:(c + 1) * 2 * LANES])
        for s in range(2):
            blk = 2 * c + s
            gain = gain_ref[:, blk * LANES:(blk + 1) * LANES]
            yb = y[:, s * LANES:(s + 1) * LANES]
            if blk < N_K_BLOCKS:
                k_ref[:, blk * LANES:(blk + 1) * LANES] = _post(yb, lo, K_MODES[blk], gain).astype(BF16)
            else:
                fb = blk - N_K_BLOCKS
                yb = jnp.where(lo, yb, 1.0) if F_MODES[fb] == 3 else _post(yb, lo, F_MODES[fb], gain)
                f_ref[fb * LANES:(fb + 1) * LANES, :] = yb.T.astype(BF16)
    aux = _dot(h, wa_ref[...])
    kc_ref[...] = aux[:, 0:LANES]
    vc_ref[...] = aux[:, LANES:2 * LANES]
    for g in range(NSA_KV_GROUPS):
        gate_ref[g * LANES:(g + 1) * LANES, :] = jax.nn.sigmoid(aux[:, (2 + g) * LANES:(3 + g) * LANES]).T


def _inproj(x2, g, w, wa, gain, tm):
    t, d = x2.shape
    nw = w.shape[1]
    return pl.pallas_call(
        _inproj_kernel,
        grid=(t // tm,),
        in_specs=[
            pl.BlockSpec((tm, d), lambda i: (i, 0)),
            pl.BlockSpec((1, d), lambda i: (0, 0)),
            pl.BlockSpec((d, nw), lambda i: (0, 0)),
            pl.BlockSpec((d, 4 * LANES), lambda i: (0, 0)),
            pl.BlockSpec((1, nw), lambda i: (0, 0)),
        ],
        out_specs=[
            pl.BlockSpec((tm, N_K_BLOCKS * LANES), lambda i: (i, 0)),
            pl.BlockSpec((N_F_BLOCKS * LANES, tm), lambda i: (0, i)),
            pl.BlockSpec((tm, LANES), lambda i: (i, 0)),
            pl.BlockSpec((tm, LANES), lambda i: (i, 0)),
            pl.BlockSpec((NSA_KV_GROUPS * LANES, tm), lambda i: (0, i)),
        ],
        out_shape=[
            jax.ShapeDtypeStruct((t, N_K_BLOCKS * LANES), BF16),
            jax.ShapeDtypeStruct((N_F_BLOCKS * LANES, t), BF16),
            jax.ShapeDtypeStruct((t, LANES), F32),
            jax.ShapeDtypeStruct((t, LANES), F32),
            jax.ShapeDtypeStruct((NSA_KV_GROUPS * LANES, t), F32),
        ],
        compiler_params=pltpu.CompilerParams(
            dimension_semantics=("parallel",), vmem_limit_bytes=VMEM_LIMIT),
        name="inproj",
    )(x2, g, w, wa, gain)


def _sb_kernel(q_ref, k_ref, v_ref, o_ref, r_ref, acc_ref, *, tk):
    tq = q_ref.shape[1]
    cols = 2 * tq
    nsub = tq // tk
    qi = pl.program_id(2)
    qs = _stack_heads_t(q_ref[...])
    j_i = lax.broadcasted_iota(jnp.int32, (tk, cols), 0)
    t_i = lax.broadcasted_iota(jnp.int32, (tk, cols), 1) & (tq - 1)
    after = jnp.where((lax.broadcasted_iota(jnp.int32, (tk, 2 * tk), 1) & (tk - 1))
                      > lax.broadcasted_iota(jnp.int32, (tk, 2 * tk), 0), 1.0, 0.0).astype(BF16)
    r_ref[...] = jnp.zeros(r_ref.shape, F32)
    acc_ref[...] = jnp.zeros(acc_ref.shape, F32)

    def block(kb, strict):
        start = pl.multiple_of(kb * tk, tk)
        z = _dot(k_ref[pl.ds(start, tk), :], qs)
        sp = jnp.maximum(z, 0.0) + jnp.log2(1.0 + jnp.exp2(-jnp.abs(z)))
        stay = sp if strict is None else jnp.where(strict, sp, 0.0)
        hi = stay.astype(BF16)
        lo = (stay - hi.astype(F32)).astype(BF16)
        after_sum = _dot(after, jnp.concatenate([hi, lo], axis=0))
        a = jnp.exp2(z - sp - after_sum - r_ref[...])
        if strict is not None:
            a = jnp.where(strict, a, 0.0)
        acc_ref[...] += _dot(v_ref[:, pl.ds(start, tk)], a.astype(BF16))
        r_ref[...] += jnp.sum(stay, axis=0, keepdims=True)

    for u in range(nsub - 1, -1, -1):
        block(qi * nsub + u, u * tk + j_i < t_i)

    def more(carry):
        j, r_min = carry
        return (j < qi * nsub) & (r_min < SB_UNDERFLOW_LOG2)

    def body(carry):
        j, _ = carry
        block(qi * nsub - 1 - j, None)
        return j + 1, jnp.min(r_ref[...])

    lax.while_loop(more, body, (0, jnp.min(r_ref[...])))
    lo = lax.broadcasted_iota(jnp.int32, (LANES, tq), 0) < HEAD_DIM
    acc = acc_ref[...]
    o_ref[...] = jnp.where(lo, acc[:, :tq], acc[:, tq:]).T.astype(o_ref.dtype)


def _sb_attention(kmat, feat, b, s, tq, tk):
    t = b * s
    nq = s // tq
    return pl.pallas_call(
        functools.partial(_sb_kernel, tk=tk),
        grid=(b, SB_HEADS // 2, nq),
        in_specs=[
            pl.BlockSpec((LANES, tq), lambda bi, p, i: (F_SB_Q + p, bi * nq + i)),
            pl.BlockSpec((s, LANES), lambda bi, p, i: (bi, K_SB + p)),
            pl.BlockSpec((LANES, s), lambda bi, p, i: (F_SB_V + p, bi)),
        ],
        out_specs=pl.BlockSpec((tq, LANES), lambda bi, p, i: (bi * nq + i, p)),
        out_shape=jax.ShapeDtypeStruct((t, SB_HEADS * HEAD_DIM), BF16),
        scratch_shapes=[pltpu.VMEM((1, 2 * tq), F32), pltpu.VMEM((LANES, 2 * tq), F32)],
        compiler_params=pltpu.CompilerParams(
            dimension_semantics=("parallel", "parallel", "arbitrary"), vmem_limit_bytes=VMEM_LIMIT),
        name="stickbreak",
    )(feat, kmat, feat)


def _softmax_init(m_ref, acc_ref, n_states):
    m_ref[:n_states] = jnp.full((n_states,) + m_ref.shape[1:], NEG, F32)
    acc_ref[:n_states] = jnp.zeros((n_states,) + acc_ref.shape[1:], F32)


def _softmax_set(m_ref, acc_ref, s, logits, v_t):
    m = jnp.maximum(jnp.max(logits, axis=0, keepdims=True), NEG)
    acc_ref[s] = _dot(v_t, jnp.exp2(logits - m).astype(BF16))
    m_ref[s] = m


def _softmax_update(m_ref, acc_ref, s, logits, v_t):
    m_old = m_ref[s]
    m_new = jnp.maximum(m_old, jnp.max(logits, axis=0, keepdims=True))
    p = jnp.exp2(logits - m_new)
    acc_ref[s] = jnp.exp2(m_old - m_new) * acc_ref[s] + _dot(v_t, p.astype(BF16))
    m_ref[s] = m_new


def _softmax_merge(m_ref, acc_ref, states):
    m = m_ref[states[0]]
    for s in states[1:]:
        m = jnp.maximum(m, m_ref[s])
    acc = None
    for s in states:
        w = jnp.exp2(m_ref[s] - m)
        acc = w * acc_ref[s] if acc is None else acc + w * acc_ref[s]
    return acc


def _hide(live):
    return jnp.where(live, 0.0, -jnp.inf)


def _moba_kernel(q_ref, k_ref, v_ref, tbl_ref, o_ref, km_ref, bias_ref, sel_ref, m_ref, acc_ref, *, nblk):
    tk = MOBA_BLOCK
    tq = q_ref.shape[1]
    nsub = tq // tk
    cols = 2 * tq
    qi = pl.program_id(2)

    @pl.when(qi == 0)
    def _():
        km_ref[...] = jnp.zeros(km_ref.shape, F32)

        def mean_body(n, carry):
            kb = k_ref[pl.ds(pl.multiple_of(n * tk, tk), tk), :].astype(F32)
            km_ref[pl.ds(n, 1), :] = jnp.mean(kb, axis=0, keepdims=True)
            return carry

        lax.fori_loop(0, nblk, mean_body, 0)
        j = lax.broadcasted_iota(jnp.int32, (tk, tq), 0)
        i = lax.broadcasted_iota(jnp.int32, (tk, tq), 1)
        for e in range(-1, nsub):
            bias_ref[e + 1] = _bias_tile_t(tbl_ref, 2, i - tk * e - j)

    qs = _stack_heads_t(q_ref[...])
    sub_s = lax.broadcasted_iota(jnp.int32, (LANES, cols), 1) & (tq - 1)
    past = lax.broadcasted_iota(jnp.int32, (LANES, cols), 0) < nsub * qi + (sub_s >> MOBA_SHIFT)
    score = jnp.where(past, _dot(km_ref[...].astype(BF16), qs), NEG)
    sel_ref[...] = jnp.where(past & (_top_n_mask_t(score, MOBA_TOPK) > 0.5), 0.0, -jnp.inf)
    _softmax_init(m_ref, acc_ref, MOBA_FAR_STATES)
    ones = jnp.ones((ONES_ROWS, tk), BF16)

    def tiles(*specs, fresh=False):
        units = []
        for state, n, fn in specs:
            start = pl.multiple_of(n * tk, tk)
            units += [(state, start, fn, slice(c, c + COL_BLOCK)) for c in range(0, cols, COL_BLOCK)]

        def score(unit):
            _, start, fn, cs = unit
            return fn(_dot(k_ref[pl.ds(start, tk), :], qs[:, cs]), cs)

        def fold(unit, logits):
            state, start, _, cs = unit
            v_t = jnp.concatenate([v_ref[:, pl.ds(start, tk)], ones], axis=0)
            (_softmax_set if fresh else _softmax_update)(m_ref, acc_ref, (state, slice(None), cs), logits, v_t)

        pending = []
        for unit in units:
            pending.append((unit, score(unit)))
            if len(pending) > SCORE_LOOKAHEAD:
                fold(*pending.pop(0))
        for item in pending:
            fold(*item)

    def hidden(n, cs, live=None):
        row = sel_ref[pl.ds(n, 1), cs]
        if live is not None:
            row = row + _hide(live)
        return jnp.broadcast_to(row, (tk, COL_BLOCK))

    n_far = jnp.maximum(nsub * qi - 1, 0)

    group = MOBA_FAR_STATES

    def far_tile(i, u):
        n = group * i + u
        if u == 0:
            return (u, n, lambda z, cs: z + hidden(n, cs))
        clamped = jnp.minimum(n, n_far - 1)
        return (u, clamped, lambda z, cs: z + hidden(clamped, cs, live=n < n_far))

    def far_body(i, carry):
        tiles(*[far_tile(i, u) for u in range(group)])
        return carry

    lax.fori_loop(0, (n_far + group - 1) // group, far_body, 0)
    prev = jnp.maximum(nsub * qi - 1, 0)
    j_i = lax.broadcasted_iota(jnp.int32, (tk, cols), 0)
    t_i = lax.broadcasted_iota(jnp.int32, (tk, cols), 1) & (tq - 1)

    def own_tile(e):
        causal_own = ((t_i >> MOBA_SHIFT) == e) & (j_i <= t_i - tk * e)
        return (MOBA_FAR_STATES + 1 + e, nsub * qi + e,
                lambda z, cs: z + bias_ref[e + 1, :, cs] + jnp.where(causal_own[:, cs], 0.0, hidden(nsub * qi + e, cs)))

    tiles((MOBA_FAR_STATES, prev, lambda z, cs: z + bias_ref[0, :, cs] + hidden(prev, cs, live=qi >= 1)),
          *[own_tile(e) for e in range(nsub)], fresh=True)
    acc = _softmax_merge(m_ref, acc_ref, list(range(MOBA_FAR_STATES + 1 + nsub)))
    out = acc[:LANES] / jnp.maximum(acc[LANES:LANES + 1], TINY)
    lo = lax.broadcasted_iota(jnp.int32, (LANES, tq), 0) < HEAD_DIM
    o_ref[...] = jnp.where(lo, out[:, :tq], out[:, tq:]).T.astype(o_ref.dtype)


def _moba_attention(kmat, feat, tbl, b, s, tq):
    t = b * s
    tk = MOBA_BLOCK
    nq = s // tq
    cols = 2 * tq
    n_states = MOBA_FAR_STATES + 1 + tq // tk
    return pl.pallas_call(
        functools.partial(_moba_kernel, nblk=s // tk),
        grid=(b, MOBA_HEADS // 2, nq),
        in_specs=[
            pl.BlockSpec((LANES, tq), lambda bi, p, i: (F_MB_Q + p, bi * nq + i)),
            pl.BlockSpec((s, LANES), lambda bi, p, i: (bi, K_MB + p)),
            pl.BlockSpec((LANES, s), lambda bi, p, i: (F_MB_V + p, bi)),
            pl.BlockSpec((None, SUBLANES, LANES), lambda bi, p, i: (p, 0, 0)),
        ],
        out_specs=pl.BlockSpec((tq, LANES), lambda bi, p, i: (bi * nq + i, p)),
        out_shape=jax.ShapeDtypeStruct((t, MOBA_HEADS * HEAD_DIM), BF16),
        scratch_shapes=[
            pltpu.VMEM((LANES, LANES), F32),
            pltpu.VMEM((tq // tk + 1, tk, cols), F32),
            pltpu.VMEM((LANES, cols), F32),
            pltpu.VMEM((n_states, 1, cols), F32),
            pltpu.VMEM((n_states, LANES + ONES_ROWS, cols), F32),
        ],
        compiler_params=pltpu.CompilerParams(
            dimension_semantics=("arbitrary", "arbitrary", "arbitrary"), vmem_limit_bytes=VMEM_LIMIT),
        name="moba",
    )(feat, kmat, feat, tbl)


def _compress_kernel(rk_ref, rv_ref, wk_ref, wv_ref, pos_ref, w2_ref, kn_ref, kc_ref, vc_ref):
    n = rk_ref.shape[0]
    hid = CMP_HIDDEN
    lo = lax.broadcasted_iota(jnp.int32, (n, LANES), 1) < HEAD_DIM
    for idx, (r_ref, w_ref) in enumerate(((rk_ref, wk_ref), (rv_ref, wv_ref))):
        r = r_ref[...]
        first = _dot((r + pos_ref[2 * idx:2 * idx + 1, :]).astype(BF16), w_ref[:, :2 * hid])
        second = _dot((r + pos_ref[2 * idx + 1:2 * idx + 2, :]).astype(BF16), w_ref[:, 2 * hid:])
        h = jax.nn.gelu(first + pltpu.roll(second, shift=n - 1, axis=0))
        for g in range(NSA_KV_GROUPS):
            y = _dot(h[:, g * hid:(g + 1) * hid].astype(BF16), w2_ref[idx])
            if idx == 0:
                kc_ref[g] = _rms(y, kn_ref[...]).astype(kc_ref.dtype)
            else:
                vc_ref[g] = jnp.where(lo, y, 1.0).T.astype(vc_ref.dtype)


def _compress(rk, rv, wk, wv, pos, w2, kn):
    b, n, width = rk.shape
    return pl.pallas_call(
        _compress_kernel,
        grid=(b,),
        in_specs=[
            pl.BlockSpec((None, n, width), lambda i: (i, 0, 0)),
            pl.BlockSpec((None, n, width), lambda i: (i, 0, 0)),
            pl.BlockSpec(wk.shape, lambda i: (0, 0)),
            pl.BlockSpec(wv.shape, lambda i: (0, 0)),
            pl.BlockSpec(pos.shape, lambda i: (0, 0)),
            pl.BlockSpec(w2.shape, lambda i: (0, 0, 0)),
            pl.BlockSpec(kn.shape, lambda i: (0, 0)),
        ],
        out_specs=[
            pl.BlockSpec((None, NSA_KV_GROUPS, n, LANES), lambda i: (i, 0, 0, 0)),
            pl.BlockSpec((None, NSA_KV_GROUPS, LANES, n), lambda i: (i, 0, 0, 0)),
        ],
        out_shape=[
            jax.ShapeDtypeStruct((b, NSA_KV_GROUPS, n, LANES), BF16),
            jax.ShapeDtypeStruct((b, NSA_KV_GROUPS, LANES, n), BF16),
        ],
        compiler_params=pltpu.CompilerParams(
            dimension_semantics=("parallel",), vmem_limit_bytes=VMEM_LIMIT),
        name="nsa_compress",
    )(rk, rv, wk, wv, pos, w2, kn)


def _nsa_kernel(q_ref, kc_ref, vc_ref, ks_ref, vs_ref, kw_ref, vw_ref, gate_ref, tbl_ref, o_ref,
                bdiag_ref, bnear_ref, ov_ref, sel_ref, m_ref, acc_ref, tot_ref, lc_ref, imp_ref):
    tq = q_ref.shape[1]
    nh = NSA_HPG
    cols = nh * tq
    ncmp = kc_ref.shape[0]
    qi = pl.program_id(2)
    q0 = qi * tq

    @pl.when(qi == 0)
    def _():
        j = lax.broadcasted_iota(jnp.int32, (tq, tq), 0)
        i = lax.broadcasted_iota(jnp.int32, (tq, tq), 1)
        bdiag_ref[...] = _bias_tile_t(tbl_ref, nh, i - j)
        bnear_ref[...] = _bias_tile_t(tbl_ref, nh, tq + i - j)
        ss = lax.broadcasted_iota(jnp.int32, (LANES, ncmp), 0) * SLC_BLOCK
        cs = lax.broadcasted_iota(jnp.int32, (LANES, ncmp), 1) * CMP_STRIDE
        ov_ref[...] = jnp.where((cs < ss + SLC_BLOCK) & (cs + CMP_BLOCK > ss), 1.0, 0.0).astype(BF16)

    qs = _stack_heads_t(q_ref[...])

    def gate(branch):
        return jnp.concatenate([gate_ref[branch * nh + h:branch * nh + h + 1, :] for h in range(nh)], axis=1)

    def result(states):
        acc = _softmax_merge(m_ref, acc_ref, states)
        return acc[:HEAD_DIM] * (1.0 / jnp.maximum(acc[HEAD_DIM:HEAD_DIM + 1], TINY))

    j_i = lax.broadcasted_iota(jnp.int32, (tq, cols), 0)
    i_i = lax.broadcasted_iota(jnp.int32, (tq, cols), 1) & (tq - 1)
    causal = j_i <= i_i
    per_tile = tq // SLC_BLOCK
    nwin = WINDOW // tq
    n_sel_states = NSA_SEL_STATES
    group = NSA_FAR_GROUP
    _softmax_init(m_ref, acc_ref, group)

    def tile_steps(*specs, fresh=False):
        units = []
        for state, k_ref, v_ref, kb, fn in specs:
            start = pl.multiple_of(kb * tq, tq)
            units += [(state, k_ref, v_ref, start, fn, slice(c, c + COL_BLOCK)) for c in range(0, cols, COL_BLOCK)]

        def score(unit):
            _, k_ref, _, start, fn, cs = unit
            return fn(_dot(k_ref[pl.ds(start, tq), :], qs[:, cs]), cs)

        def fold(unit, logits):
            state, _, v_ref, start, _, cs = unit
            (_softmax_set if fresh else _softmax_update)(
                m_ref, acc_ref, (state, slice(None), cs), logits, v_ref[:NSA_V_ROWS, pl.ds(start, tq)])

        pending = []
        for unit in units:
            pending.append((unit, score(unit)))
            if len(pending) > SCORE_LOOKAHEAD:
                fold(*pending.pop(0))
                yield
        for item in pending:
            fold(*item)
            yield

    def tiles(*specs, fresh=False):
        for _ in tile_steps(*specs, fresh=fresh):
            pass

    def window_tile(r):
        def fn(z, cs):
            if r == 1:
                z = z + bnear_ref[:, cs]
            if r == nwin:
                z = jnp.where((j_i > i_i)[:, cs], z, -jnp.inf)
            return z + _hide(qi >= r)
        return (n_sel_states + r, kw_ref, vw_ref, jnp.maximum(qi - r, 0), fn)

    window_steps = tile_steps(
        *[window_tile(r) for r in range(nwin, 0, -1)],
        (n_sel_states, kw_ref, vw_ref, qi, lambda z, cs: jnp.where(causal[:, cs], z + bdiag_ref[:, cs], -jnp.inf)),
        fresh=True)

    per_tile_c = tq // CMP_STRIDE
    below = _ceil_to(-(-(REL_MAX_DISTANCE + CMP_BLOCK) // CMP_STRIDE), SUBLANES)
    band = below + per_tile_c
    band_start = pl.multiple_of(jnp.maximum(qi * per_tile_c - below, 0), SUBLANES)
    n_b = band_start + lax.broadcasted_iota(jnp.int32, (band, tq), 0)
    dist_b = q0 + lax.broadcasted_iota(jnp.int32, (band, tq), 1) - (n_b * CMP_STRIDE + CMP_BLOCK - 1)
    band_bias = _bias_tile_t(tbl_ref, nh, dist_b)

    def compressed(rows):
        n_i = lax.broadcasted_iota(jnp.int32, (rows, cols), 0)
        t_i = lax.broadcasted_iota(jnp.int32, (rows, cols), 1) & (tq - 1)
        valid_c = q0 + t_i - (n_i * CMP_STRIDE + CMP_BLOCK - 1) >= 0
        lc_ref[:rows] = _dot(kc_ref[:rows, :], qs)
        lc_ref[pl.ds(band_start, band), :] += band_bias
        logit_c = jnp.where(valid_c, lc_ref[:rows], -jnp.inf)
        p_c = jnp.exp2(logit_c - jnp.maximum(jnp.max(logit_c, axis=0, keepdims=True), NEG))
        p_c = p_c / jnp.maximum(jnp.sum(p_c, axis=0, keepdims=True), TINY)
        tot_ref[...] = gate(0) * _dot(vc_ref[:HEAD_DIM, :rows], p_c.astype(BF16))
        p_sum = p_c[:, 0:tq]
        for h in range(1, nh):
            p_sum = p_sum + p_c[:, h * tq:(h + 1) * tq]
        imp_ref[...] = _dot_hilo_rhs(ov_ref[:, :rows], p_sum)

    step = min(CMP_ROW_STEP, ncmp)
    needed = (qi + 1) * per_tile_c
    for rows in range(step, ncmp + 1, step):
        pl.when((needed > rows - step) & (needed <= rows))(functools.partial(compressed, rows))
    imp = imp_ref[...]
    blk = lax.broadcasted_iota(jnp.int32, (LANES, tq), 0)
    cur = (q0 + lax.broadcasted_iota(jnp.int32, (LANES, tq), 1)) >> SLC_SHIFT
    forced = (blk == 0) | (blk == cur) | (blk == cur - 1)
    score = jnp.where(blk <= cur, jnp.where(forced, BIG, imp), NEG)
    top = _top_n_mask_t(score, SLC_TOPN, between=lambda: next(window_steps, None))
    for _ in window_steps:
        pass
    sel = jnp.where((blk <= cur) & (top > 0.5), 0.0, -jnp.inf)
    sel_ref[...] = jnp.concatenate([sel] * nh, axis=1)

    def hidden(kb, cs, live=None):
        rows = [sel_ref[pl.ds(kb * per_tile + r, 1), cs] for r in range(per_tile)]
        if live is not None:
            rows = [row + _hide(live) for row in rows]
        return jnp.concatenate([jnp.broadcast_to(row, (SLC_BLOCK, COL_BLOCK)) for row in rows], axis=0)

    n_far = jnp.maximum(qi - 1, 0)

    def far_tile(i, u):
        kb = group * i + u
        if u == 0:
            return (u, ks_ref, vs_ref, kb, lambda z, cs: z + hidden(kb, cs))
        clamped = jnp.minimum(kb, n_far - 1)
        return (u, ks_ref, vs_ref, clamped, lambda z, cs: z + hidden(clamped, cs, live=kb < n_far))

    def sel_body(i, carry):
        tiles(*[far_tile(i, u) for u in range(group)])
        return carry

    lax.fori_loop(0, (n_far + group - 1) // group, sel_body, 0)
    near = jnp.maximum(qi - 1, 0)
    tiles((group, ks_ref, vs_ref, near, lambda z, cs: z + bnear_ref[:, cs] + hidden(near, cs, live=qi >= 1)),
          (group + 1, ks_ref, vs_ref, qi,
           lambda z, cs: jnp.where(causal[:, cs], z + bdiag_ref[:, cs] + hidden(qi, cs), -jnp.inf)),
          fresh=True)
    tot = (tot_ref[...] + gate(1) * result(list(range(n_sel_states)))
           + gate(2) * result(list(range(n_sel_states, n_sel_states + nwin + 1))))
    pairs = [jnp.concatenate([tot[:, (2 * c) * tq:(2 * c + 1) * tq], tot[:, (2 * c + 1) * tq:(2 * c + 2) * tq]], axis=0).T
             for c in range(nh // 2)]
    o_ref[...] = jnp.concatenate(pairs, axis=1).astype(o_ref.dtype)


def _nsa_attention(kmat, feat, kc, vc, gates_t, tbl, b, s, tq):
    t = b * s
    nq = s // tq
    ncmp = kc.shape[2]
    assert ncmp % min(CMP_ROW_STEP, ncmp) == 0
    cols = NSA_HPG * tq
    n_states = NSA_SEL_STATES + WINDOW // tq + 1
    k_spec = lambda blk: pl.BlockSpec((s, LANES), lambda bi, g, i: (bi, blk + g))
    v_spec = lambda blk: pl.BlockSpec((LANES, s), lambda bi, g, i: (blk + g, bi))
    return pl.pallas_call(
        _nsa_kernel,
        grid=(b, NSA_KV_GROUPS, nq),
        in_specs=[
            pl.BlockSpec((2 * LANES, tq), lambda bi, g, i: (F_NS_Q // 2 + g, bi * nq + i)),
            pl.BlockSpec((None, None, ncmp, LANES), lambda bi, g, i: (bi, g, 0, 0)),
            pl.BlockSpec((None, None, LANES, ncmp), lambda bi, g, i: (bi, g, 0, 0)),
            k_spec(K_NS_SLC), v_spec(F_NS_SLC_V), k_spec(K_NS_WIN), v_spec(F_NS_WIN_V),
            pl.BlockSpec((LANES, tq), lambda bi, g, i: (g, bi * nq + i)),
            pl.BlockSpec((None, SUBLANES, LANES), lambda bi, g, i: (g, 0, 0)),
        ],
        out_specs=pl.BlockSpec((tq, 2 * LANES), lambda bi, g, i: (bi * nq + i, g)),
        out_shape=jax.ShapeDtypeStruct((t, NSA_HEADS * HEAD_DIM), BF16),
        scratch_shapes=[
            pltpu.VMEM((tq, cols), F32),
            pltpu.VMEM((tq, cols), F32),
            pltpu.VMEM((LANES, ncmp), BF16),
            pltpu.VMEM((LANES, cols), F32),
            pltpu.VMEM((n_states, 1, cols), F32),
            pltpu.VMEM((n_states, NSA_V_ROWS, cols), F32),
            pltpu.VMEM((HEAD_DIM, cols), F32),
            pltpu.VMEM((ncmp, cols), F32),
            pltpu.VMEM((LANES, tq), F32),
        ],
        compiler_params=pltpu.CompilerParams(
            dimension_semantics=("arbitrary", "arbitrary", "arbitrary"), vmem_limit_bytes=VMEM_LIMIT),
        name="nsa",
    )(feat, kc, vc, kmat, feat, kmat, feat, gates_t, tbl)


def _merge_kernel(x_ref, g_ref, oa_ref, ob_ref, oc_ref, wg_ref, wb_ref, wo_ref, o_ref):
    x = x_ref[...]
    d = x.shape[1]
    h = _rms(x, g_ref[...]).astype(BF16)
    mix = None
    row = 0
    for br, src in enumerate((oa_ref, ob_ref, oc_ref)):
        width = src.shape[1]
        gate = jax.nn.sigmoid(_dot(h, wg_ref[:, br * d:(br + 1) * d]))
        term = gate * _dot(src[...], wb_ref[row:row + width, :])
        mix = term if mix is None else mix + term
        row += width
    o_ref[...] = x + _dot(mix.astype(BF16), wo_ref[...])


def _merge(x2, g, oa, ob, oc, wg, wb, wo, tm):
    t, d = x2.shape
    row = lambda w: pl.BlockSpec((tm, w), lambda i: (i, 0))
    full = lambda a: pl.BlockSpec(a.shape, lambda i: (0, 0))
    return pl.pallas_call(
        _merge_kernel,
        grid=(t // tm,),
        in_specs=[row(d), full(g), row(oa.shape[1]), row(ob.shape[1]), row(oc.shape[1]),
                  full(wg), full(wb), full(wo)],
        out_specs=row(d),
        out_shape=jax.ShapeDtypeStruct((t, d), F32),
        compiler_params=pltpu.CompilerParams(
            dimension_semantics=("parallel",), vmem_limit_bytes=VMEM_LIMIT),
        name="merge",
    )(x2, g, oa, ob, oc, wg, wb, wo)


def _ffn_kernel(x_ref, g_ref, wgu_ref, wd_ref, o_ref, *, n_chunks):
    x = x_ref[...]
    d_ff = wd_ref.shape[0]
    ch = d_ff // n_chunks
    h = _rms(x, g_ref[...]).astype(BF16)
    out = x
    for c in range(n_chunks):
        gate = _dot(h, wgu_ref[:, c * ch:(c + 1) * ch])
        up = _dot(h, wgu_ref[:, d_ff + c * ch:d_ff + (c + 1) * ch])
        out = out + _dot((jax.nn.silu(gate) * up).astype(BF16), wd_ref[c * ch:(c + 1) * ch, :])
    o_ref[...] = out


def _ffn(x2, g, wgu, wd, tm):
    t, d = x2.shape
    d_ff = wd.shape[0]
    n_chunks = 2 if d_ff % (2 * LANES) == 0 else 1
    return pl.pallas_call(
        functools.partial(_ffn_kernel, n_chunks=n_chunks),
        grid=(t // tm,),
        in_specs=[
            pl.BlockSpec((tm, d), lambda i: (i, 0)),
            pl.BlockSpec((1, d), lambda i: (0, 0)),
            pl.BlockSpec(wgu.shape, lambda i: (0, 0)),
            pl.BlockSpec(wd.shape, lambda i: (0, 0)),
        ],
        out_specs=pl.BlockSpec((tm, d), lambda i: (i, 0)),
        out_shape=jax.ShapeDtypeStruct((t, d), F32),
        compiler_params=pltpu.CompilerParams(
            dimension_semantics=("parallel",), vmem_limit_bytes=VMEM_LIMIT),
        name="swiglu",
    )(x2, g, wgu, wd)


def _dup_groups(w):
    d = w.shape[0]
    w = w.reshape(d, NSA_KV_GROUPS, 1, HEAD_DIM)
    return jnp.broadcast_to(w, (d, NSA_KV_GROUPS, 2, HEAD_DIM)).reshape(d, NSA_KV_GROUPS * PAIR)


def _half_groups(w):
    d = w.shape[0]
    w = w.reshape(d, NSA_KV_GROUPS, HEAD_DIM)
    return jnp.pad(w, ((0, 0), (0, 0), (0, HEAD_DIM))).reshape(d, NSA_KV_GROUPS * PAIR)


def _pair_gain(g, scale=1.0):
    return jnp.concatenate([g, g]) * scale


def _layer_params(w_in, moba_q_norm, moba_k_norm, nsa_q_norm, nsa_k_norm, nsa_cmp_pos, nsa_cmp_w1, nsa_cmp_w2):
    d = w_in.shape[0]
    sbw, mbw, nsw, kvw = SB_HEADS * HEAD_DIM, MOBA_HEADS * HEAD_DIM, NSA_HEADS * HEAD_DIM, NSA_KV_GROUPS * HEAD_DIM
    sb_q, sb_k, sb_v = (w_in[:, i * sbw:(i + 1) * sbw] for i in range(3))
    o = 3 * sbw
    mb_q, mb_k, mb_v = (w_in[:, o + i * mbw:o + (i + 1) * mbw] for i in range(3))
    o += 3 * mbw
    ns_q = w_in[:, o:o + nsw]
    o += nsw
    kc_w, vc_w, ks_w, vs_w, kw_w, vw_w = (w_in[:, o + i * kvw:o + (i + 1) * kvw] for i in range(6))
    o += 6 * kvw
    gate_w = w_in[:, o:o + N_BRANCHES * NSA_HEADS].reshape(d, N_BRANCHES, NSA_KV_GROUPS, NSA_HPG)
    o += N_BRANCHES * NSA_HEADS
    wg = w_in[:, o:].astype(BF16)
    w = jnp.concatenate([sb_k, mb_k, _dup_groups(ks_w), _dup_groups(kw_w),
                         sb_q, mb_q, ns_q, sb_v, mb_v, _half_groups(vs_w), _half_groups(vw_w)], axis=1).astype(BF16)
    gate_cols = []
    for g in range(NSA_KV_GROUPS):
        cols = gate_w[:, :, g, :].reshape(d, N_BRANCHES * NSA_HPG)
        gate_cols.append(jnp.pad(cols, ((0, 0), (0, LANES - N_BRANCHES * NSA_HPG))))
    wa = jnp.concatenate([kc_w, vc_w] + gate_cols, axis=1).astype(BF16)
    scale = HEAD_DIM ** -0.5
    ones = jnp.ones((LANES,), F32)
    gains = [ones] * 2 + [_pair_gain(moba_k_norm)] * 2 + [_pair_gain(nsa_k_norm[1])] * 2 + [_pair_gain(nsa_k_norm[2])] * 2
    gains += [ones * (scale * LOG2E)] * 2 + [_pair_gain(moba_q_norm, scale * LOG2E)] * 2
    gains += [_pair_gain(nsa_q_norm, scale * LOG2E)] * 4 + [ones] * 8
    gain = jnp.concatenate(gains).reshape(1, (N_K_BLOCKS + N_F_BLOCKS) * LANES)

    half = CMP_BLOCK // 2
    hid = CMP_HIDDEN

    def spread(w1):
        w1 = w1.reshape(2, half, HEAD_DIM, hid)
        cols = []
        for part in range(2):
            for g in range(NSA_KV_GROUPS):
                z = jnp.zeros((half, NSA_KV_GROUPS, HEAD_DIM, hid), F32).at[:, g].set(w1[part])
                cols.append(z.reshape(half * kvw, hid))
        return jnp.concatenate(cols, axis=1).astype(BF16)

    def spread_pos(p):
        p = p.reshape(2, half, 1, HEAD_DIM)
        return jnp.broadcast_to(p, (2, half, NSA_KV_GROUPS, HEAD_DIM)).reshape(2, half * kvw)

    pos = jnp.concatenate([spread_pos(nsa_cmp_pos[0]), spread_pos(nsa_cmp_pos[1])], axis=0)
    w2 = jnp.stack([jnp.concatenate([nsa_cmp_w2[0], nsa_cmp_w2[0]], axis=-1),
                    jnp.pad(nsa_cmp_w2[1], ((0, 0), (0, HEAD_DIM)))]).astype(BF16)
    kn = _pair_gain(nsa_k_norm[0]).reshape(1, LANES)
    return dict(w=w, wa=wa, gain=gain, wg=wg, wk=spread(nsa_cmp_w1[0]), wv=spread(nsa_cmp_w1[1]),
                pos=pos, w2=w2, kn=kn)


def _bias_tables(rel_bias):
    tbl = jnp.pad(rel_bias.T, ((0, 0), (0, LANES - N_BUCKETS)))

    def rows(t, n):
        t = t.reshape(n, -1, LANES)
        return jnp.pad(t, ((0, 0), (0, SUBLANES - t.shape[1]), (0, 0)))

    return rows(tbl[:MOBA_HEADS], MOBA_HEADS // 2), rows(tbl[MOBA_HEADS:], NSA_KV_GROUPS)


def kernel(x, rel_bias, attn_norm, w_in, moba_q_norm, moba_k_norm, nsa_q_norm, nsa_k_norm, nsa_cmp_pos,
           nsa_cmp_w1, nsa_cmp_w2, w_branch, w_out, ffn_norm, w_gate_up, w_down):
    b, s, d = x.shape
    t = b * s
    tm = min(512, t)
    tq_moba = 512
    tq_nsa = 256
    tq_sb, tk_sb = 512, 256
    assert s % tq_moba == 0 and tq_moba % MOBA_BLOCK == 0 and s % tq_sb == 0 and WINDOW % tq_nsa == 0
    tbl_moba, tbl_nsa = _bias_tables(rel_bias)
    x2 = x.reshape(t, d)
    for layer in range(w_in.shape[0]):
        p = _layer_params(w_in[layer], moba_q_norm[layer], moba_k_norm[layer], nsa_q_norm[layer],
                          nsa_k_norm[layer], nsa_cmp_pos[layer], nsa_cmp_w1[layer], nsa_cmp_w2[layer])
        g_attn = attn_norm[layer].reshape(1, d)
        kmat, feat, kc_in, vc_in, gates_t = _inproj(x2, g_attn, p["w"], p["wa"], p["gain"], tm)
        o_a = _sb_attention(kmat, feat, b, s, tq_sb, tk_sb)
        o_b = _moba_attention(kmat, feat, tbl_moba, b, s, tq_moba)
        chunk = CMP_STRIDE * LANES
        kc, vc = _compress(kc_in.reshape(b, s // CMP_STRIDE, chunk), vc_in.reshape(b, s // CMP_STRIDE, chunk),
                           p["wk"], p["wv"], p["pos"], p["w2"], p["kn"])
        o_c = _nsa_attention(kmat, feat, kc, vc, gates_t, tbl_nsa, b, s, tq_nsa)
        x2 = _merge(x2, g_attn, o_a, o_b, o_c, p["wg"], w_branch[layer].astype(BF16),
                    w_out[layer].astype(BF16), tm)
        x2 = _ffn(x2, ffn_norm[layer].reshape(1, d), w_gate_up[layer].astype(BF16),
                  w_down[layer].astype(BF16), tm)
    return x2.reshape(b, s, d)
```

```python
import functools
import math

import jax
import jax.numpy as jnp
from jax import lax
from jax.experimental import pallas as pl
from jax.experimental.pallas import tpu as pltpu

HEAD_DIM = 64
SB_HEADS = 4
MOBA_HEADS = 4
NSA_HEADS = 8
NSA_KV_GROUPS = 2
NSA_HPG = NSA_HEADS // NSA_KV_GROUPS
N_BRANCHES = 3
MOBA_BLOCK = 256
MOBA_TOPK = 3
CMP_BLOCK = 32
CMP_STRIDE = 16
CMP_HIDDEN = 4 * HEAD_DIM
SLC_BLOCK = 64
SLC_TOPN = 16
WINDOW = 512
N_BUCKETS = 32
REL_MAX_DISTANCE = 128
NORM_EPS = 1e-6
NEG = -1e30
BIG = 1e30
TINY = 1e-30
LOG2E = math.log2(math.e)
MOBA_FAR_STATES = 4
NSA_FAR_GROUP = 4
NSA_SEL_STATES = NSA_FAR_GROUP + 2
SB_UNDERFLOW_LOG2 = 160.0

LANES = 128
SUBLANES = 8
COL_BLOCK = 256
SCORE_LOOKAHEAD = 4
N_FORCED = 3
CMP_ROW_STEP = 128
NSA_V_ROWS = HEAD_DIM + 16
ONES_ROWS = 16
PAIR = 2 * HEAD_DIM
VMEM_LIMIT = 56 * 1024 * 1024

F32 = jnp.float32
BF16 = jnp.bfloat16

K_SB, K_MB, K_NS_SLC, K_NS_WIN = 0, 2, 4, 6
N_K_BLOCKS = 8
F_SB_Q, F_MB_Q, F_NS_Q, F_SB_V, F_MB_V, F_NS_SLC_V, F_NS_WIN_V = 0, 2, 4, 8, 10, 12, 14
N_F_BLOCKS = 16
K_MODES = (0, 0, 2, 2, 2, 2, 2, 2)
F_MODES = (1, 1, 2, 2, 2, 2, 2, 2, 0, 0, 0, 0, 3, 3, 3, 3)


SLC_SHIFT = SLC_BLOCK.bit_length() - 1
MOBA_SHIFT = MOBA_BLOCK.bit_length() - 1
assert 1 << SLC_SHIFT == SLC_BLOCK and 1 << MOBA_SHIFT == MOBA_BLOCK


def _ceil_to(n, m):
    return -(-n // m) * m


def _dot(a, b):
    return jnp.dot(a, b, preferred_element_type=F32)


def _dot_hilo_rhs(a, b):
    hi = b.astype(BF16)
    lo = (b - hi.astype(F32)).astype(BF16)
    return _dot(a, hi) + _dot(a, lo)


def _rms(x, g):
    ms = jnp.mean(x * x, axis=-1, keepdims=True)
    return x * lax.rsqrt(ms + NORM_EPS) * g


def _stack_heads_t(q_t):
    tq = q_t.shape[1]
    lo = lax.broadcasted_iota(jnp.int32, (LANES, tq), 0) < HEAD_DIM
    zero = jnp.zeros((LANES, tq), q_t.dtype)
    parts = []
    for c in range(q_t.shape[0] // LANES):
        blk = q_t[c * LANES:(c + 1) * LANES, :]
        parts.append(jnp.where(lo, blk, zero))
        parts.append(jnp.where(lo, zero, blk))
    return jnp.concatenate(parts, axis=1)


def _t5_bucket(dist):
    n = jnp.maximum(dist, 0)
    max_exact = N_BUCKETS // 2
    nf = jnp.maximum(n, 1).astype(F32)
    large = max_exact + (jnp.log(nf / max_exact) / math.log(REL_MAX_DISTANCE / max_exact)
                         * (N_BUCKETS - max_exact)).astype(jnp.int32)
    large = jnp.minimum(large, N_BUCKETS - 1)
    return jnp.where(n < max_exact, n, large)


def _bias_tile_t(tbl_ref, n_heads, dist):
    nk, tq = dist.shape
    bucket = _t5_bucket(dist)
    cols = []
    for h in range(n_heads):
        t = jnp.broadcast_to(tbl_ref[h:h + 1, :], (nk, LANES))
        far = t[:, N_BUCKETS - 1:N_BUCKETS]
        for c in range(tq // LANES):
            cols.append((jnp.take_along_axis(t, bucket[:, c * LANES:(c + 1) * LANES], axis=1) - far) * LOG2E)
    return jnp.concatenate(cols, axis=1)


def _top_n_mask_t(score, n, between=None):
    row = lax.broadcasted_iota(jnp.int32, score.shape, 0).astype(F32)
    sel = jnp.zeros(score.shape, F32)
    s = score
    for _ in range(n):
        m = jnp.max(s, axis=0, keepdims=True)
        idx = jnp.min(jnp.where(s == m, row, float(score.shape[0])), axis=0, keepdims=True)
        pick = row == idx
        sel = jnp.where(pick, 1.0, sel)
        s = jnp.where(pick, -jnp.inf, s)
        if between is not None:
            between()
    return sel


def _post(yb, lo, mode, gain):
    if mode == 2:
        sq = yb * yb
        s_lo = jnp.sum(jnp.where(lo, sq, 0.0), axis=-1, keepdims=True)
        s_hi = jnp.sum(jnp.where(lo, 0.0, sq), axis=-1, keepdims=True)
        yb = yb * lax.rsqrt(jnp.where(lo, s_lo, s_hi) * (1.0 / HEAD_DIM) + NORM_EPS)
    if mode in (1, 2):
        yb = yb * gain
    return yb


def _inproj_kernel(x_ref, g_ref, w_ref, wa_ref, gain_ref, k_ref, f_ref, kc_ref, vc_ref, gate_ref):
    tm = x_ref.shape[0]
    h = _rms(x_ref[...], g_ref[...]).astype(BF16)
    lo = lax.broadcasted_iota(jnp.int32, (tm, LANES), 1) < HEAD_DIM
    for c in range((N_K_BLOCKS + N_F_BLOCKS) // 2):
        y = _dot(h, w_ref[:, c * 2 * LANES:(c + 1) * 2 * LANES])
        for s in range(2):
            blk = 2 * c + s
            gain = gain_ref[:, blk * LANES:(blk + 1) * LANES]
            yb = y[:, s * LANES:(s + 1) * LANES]
            if blk < N_K_BLOCKS:
                k_ref[:, blk * LANES:(blk + 1) * LANES] = _post(yb, lo, K_MODES[blk], gain).astype(BF16)
            else:
                fb = blk - N_K_BLOCKS
                yb = jnp.where(lo, yb, 1.0) if F_MODES[fb] == 3 else _post(yb, lo, F_MODES[fb], gain)
                f_ref[fb * LANES:(fb + 1) * LANES, :] = yb.T.astype(BF16)
    aux = _dot(h, wa_ref[...])
    kc_ref[...] = aux[:, 0:LANES]
    vc_ref[...] = aux[:, LANES:2 * LANES]
    for g in range(NSA_KV_GROUPS):
        gate_ref[g * LANES:(g + 1) * LANES, :] = jax.nn.sigmoid(aux[:, (2 + g) * LANES:(3 + g) * LANES]).T


def _inproj(x2, g, w, wa, gain, tm):
    t, d = x2.shape
    nw = w.shape[1]
    return pl.pallas_call(
        _inproj_kernel,
        grid=(t // tm,),
        in_specs=[
            pl.BlockSpec((tm, d), lambda i: (i, 0)),
            pl.BlockSpec((1, d), lambda i: (0, 0)),
            pl.BlockSpec((d, nw), lambda i: (0, 0)),
            pl.BlockSpec((d, 4 * LANES), lambda i: (0, 0)),
            pl.BlockSpec((1, nw), lambda i: (0, 0)),
        ],
        out_specs=[
            pl.BlockSpec((tm, N_K_BLOCKS * LANES), lambda i: (i, 0)),
            pl.BlockSpec((N_F_BLOCKS * LANES, tm), lambda i: (0, i)),
            pl.BlockSpec((tm, LANES), lambda i: (i, 0)),
            pl.BlockSpec((tm, LANES), lambda i: (i, 0)),
            pl.BlockSpec((NSA_KV_GROUPS * LANES, tm), lambda i: (0, i)),
        ],
        out_shape=[
            jax.ShapeDtypeStruct((t, N_K_BLOCKS * LANES), BF16),
            jax.ShapeDtypeStruct((N_F_BLOCKS * LANES, t), BF16),
            jax.ShapeDtypeStruct((t, LANES), F32),
            jax.ShapeDtypeStruct((t, LANES), F32),
            jax.ShapeDtypeStruct((NSA_KV_GROUPS * LANES, t), F32),
        ],
        compiler_params=pltpu.CompilerParams(
            dimension_semantics=("parallel",), vmem_limit_bytes=VMEM_LIMIT),
        name="inproj",
    )(x2, g, w, wa, gain)


def _sb_kernel(q_ref, k_ref, v_ref, o_ref, r_ref, acc_ref, *, tk):
    tq = q_ref.shape[1]
    cols = 2 * tq
    nsub = tq // tk
    qi = pl.program_id(2)
    qs = _stack_heads_t(q_ref[...])
    j_i = lax.broadcasted_iota(jnp.int32, (tk, cols), 0)
    t_i = lax.broadcasted_iota(jnp.int32, (tk, cols), 1) & (tq - 1)
    after = jnp.where((lax.broadcasted_iota(jnp.int32, (tk, 2 * tk), 1) & (tk - 1))
                      > lax.broadcasted_iota(jnp.int32, (tk, 2 * tk), 0), 1.0, 0.0).astype(BF16)
    r_ref[...] = jnp.zeros(r_ref.shape, F32)
    acc_ref[...] = jnp.zeros(acc_ref.shape, F32)

    def block(kb, strict):
        start = pl.multiple_of(kb * tk, tk)
        z = _dot(k_ref[pl.ds(start, tk), :], qs)
        sp = jnp.maximum(z, 0.0) + jnp.log2(1.0 + jnp.exp2(-jnp.abs(z)))
        stay = sp if strict is None else jnp.where(strict, sp, 0.0)
        hi = stay.astype(BF16)
        lo = (stay - hi.astype(F32)).astype(BF16)
        after_sum = _dot(after, jnp.concatenate([hi, lo], axis=0))
        a = jnp.exp2(z - sp - after_sum - r_ref[...])
        if strict is not None:
            a = jnp.where(strict, a, 0.0)
        acc_ref[...] += _dot(v_ref[:, pl.ds(start, tk)], a.astype(BF16))
        r_ref[...] += jnp.sum(stay, axis=0, keepdims=True)

    for u in range(nsub - 1, -1, -1):
        block(qi * nsub + u, u * tk + j_i < t_i)

    def more(carry):
        j, r_min = carry
        return (j < qi * nsub) & (r_min < SB_UNDERFLOW_LOG2)

    def body(carry):
        j, _ = carry
        block(qi * nsub - 1 - j, None)
        return j + 1, jnp.min(r_ref[...])

    lax.while_loop(more, body, (0, jnp.min(r_ref[...])))
    lo = lax.broadcasted_iota(jnp.int32, (LANES, tq), 0) < HEAD_DIM
    acc = acc_ref[...]
    o_ref[...] = jnp.where(lo, acc[:, :tq], acc[:, tq:]).T.astype(o_ref.dtype)


def _sb_attention(kmat, feat, b, s, tq, tk):
    t = b * s
    nq = s // tq
    return pl.pallas_call(
        functools.partial(_sb_kernel, tk=tk),
        grid=(b, SB_HEADS // 2, nq),
        in_specs=[
            pl.BlockSpec((LANES, tq), lambda bi, p, i: (F_SB_Q + p, bi * nq + i)),
            pl.BlockSpec((s, LANES), lambda bi, p, i: (bi, K_SB + p)),
            pl.BlockSpec((LANES, s), lambda bi, p, i: (F_SB_V + p, bi)),
        ],
        out_specs=pl.BlockSpec((tq, LANES), lambda bi, p, i: (bi * nq + i, p)),
        out_shape=jax.ShapeDtypeStruct((t, SB_HEADS * HEAD_DIM), BF16),
        scratch_shapes=[pltpu.VMEM((1, 2 * tq), F32), pltpu.VMEM((LANES, 2 * tq), F32)],
        compiler_params=pltpu.CompilerParams(
            dimension_semantics=("parallel", "parallel", "arbitrary"), vmem_limit_bytes=VMEM_LIMIT),
        name="stickbreak",
    )(feat, kmat, feat)


def _softmax_init(m_ref, acc_ref, n_states):
    m_ref[:n_states] = jnp.full((n_states,) + m_ref.shape[1:], NEG, F32)
    acc_ref[:n_states] = jnp.zeros((n_states,) + acc_ref.shape[1:], F32)


def _softmax_set(m_ref, acc_ref, s, logits, v_t):
    m = jnp.maximum(jnp.max(logits, axis=0, keepdims=True), NEG)
    acc_ref[s] = _dot(v_t, jnp.exp2(logits - m).astype(BF16))
    m_ref[s] = m


def _softmax_update(m_ref, acc_ref, s, logits, v_t):
    m_old = m_ref[s]
    m_new = jnp.maximum(m_old, jnp.max(logits, axis=0, keepdims=True))
    p = jnp.exp2(logits - m_new)
    acc_ref[s] = jnp.exp2(m_old - m_new) * acc_ref[s] + _dot(v_t, p.astype(BF16))
    m_ref[s] = m_new


def _softmax_merge(m_ref, acc_ref, states):
    m = m_ref[states[0]]
    for s in states[1:]:
        m = jnp.maximum(m, m_ref[s])
    acc = None
    for s in states:
        w = jnp.exp2(m_ref[s] - m)
        acc = w * acc_ref[s] if acc is None else acc + w * acc_ref[s]
    return acc


def _hide(live):
    return jnp.where(live, 0.0, -jnp.inf)


def _moba_kernel(q_ref, k_ref, v_ref, tbl_ref, o_ref, km_ref, bias_ref, sel_ref, m_ref, acc_ref, *, nblk):
    tk = MOBA_BLOCK
    tq = q_ref.shape[1]
    nsub = tq // tk
    cols = 2 * tq
    qi = pl.program_id(2)

    @pl.when(qi == 0)
    def _():
        km_ref[...] = jnp.zeros(km_ref.shape, F32)

        def mean_body(n, carry):
            kb = k_ref[pl.ds(pl.multiple_of(n * tk, tk), tk), :].astype(F32)
            km_ref[pl.ds(n, 1), :] = jnp.mean(kb, axis=0, keepdims=True)
            return carry

        lax.fori_loop(0, nblk, mean_body, 0)
        j = lax.broadcasted_iota(jnp.int32, (tk, tq), 0)
        i = lax.broadcasted_iota(jnp.int32, (tk, tq), 1)
        for e in range(-1, nsub):
            bias_ref[e + 1] = _bias_tile_t(tbl_ref, 2, i - tk * e - j)

    qs = _stack_heads_t(q_ref[...])
    sub_s = lax.broadcasted_iota(jnp.int32, (LANES, cols), 1) & (tq - 1)
    past = lax.broadcasted_iota(jnp.int32, (LANES, cols), 0) < nsub * qi + (sub_s >> MOBA_SHIFT)
    score = jnp.where(past, _dot(km_ref[...].astype(BF16), qs), NEG)
    sel_ref[...] = jnp.where(past & (_top_n_mask_t(score, MOBA_TOPK) > 0.5), 0.0, -jnp.inf)
    _softmax_init(m_ref, acc_ref, MOBA_FAR_STATES)
    ones = jnp.ones((ONES_ROWS, tk), BF16)

    def tiles(*specs, fresh=False):
        units = []
        for state, n, fn in specs:
            start = pl.multiple_of(n * tk, tk)
            units += [(state, start, fn, slice(c, c + COL_BLOCK)) for c in range(0, cols, COL_BLOCK)]

        def score(unit):
            _, start, fn, cs = unit
            return fn(_dot(k_ref[pl.ds(start, tk), :], qs[:, cs]), cs)

        def fold(unit, logits):
            state, start, _, cs = unit
            head = cs.start // tq
            v_t = jnp.concatenate([v_ref[head * HEAD_DIM:(head + 1) * HEAD_DIM, pl.ds(start, tk)], ones], axis=0)
            (_softmax_set if fresh else _softmax_update)(m_ref, acc_ref, (state, slice(None), cs), logits, v_t)

        pending = []
        for unit in units:
            pending.append((unit, score(unit)))
            if len(pending) > SCORE_LOOKAHEAD:
                fold(*pending.pop(0))
        for item in pending:
            fold(*item)

    def hidden(n, cs, live=None):
        row = sel_ref[pl.ds(n, 1), cs]
        if live is not None:
            row = row + _hide(live)
        return jnp.broadcast_to(row, (tk, COL_BLOCK))

    n_far = jnp.maximum(nsub * qi - 1, 0)

    group = MOBA_FAR_STATES

    def far_tile(i, u):
        n = group * i + u
        if u == 0:
            return (u, n, lambda z, cs: z + hidden(n, cs))
        clamped = jnp.minimum(n, n_far - 1)
        return (u, clamped, lambda z, cs: z + hidden(clamped, cs, live=n < n_far))

    def far_body(i, carry):
        tiles(*[far_tile(i, u) for u in range(group)])
        return carry

    lax.fori_loop(0, (n_far + group - 1) // group, far_body, 0)
    prev = jnp.maximum(nsub * qi - 1, 0)
    j_i = lax.broadcasted_iota(jnp.int32, (tk, cols), 0)
    t_i = lax.broadcasted_iota(jnp.int32, (tk, cols), 1) & (tq - 1)

    def own_tile(e):
        causal_own = ((t_i >> MOBA_SHIFT) == e) & (j_i <= t_i - tk * e)
        return (MOBA_FAR_STATES + 1 + e, nsub * qi + e,
                lambda z, cs: z + bias_ref[e + 1, :, cs] + jnp.where(causal_own[:, cs], 0.0, hidden(nsub * qi + e, cs)))

    tiles((MOBA_FAR_STATES, prev, lambda z, cs: z + bias_ref[0, :, cs] + hidden(prev, cs, live=qi >= 1)),
          *[own_tile(e) for e in range(nsub)], fresh=True)
    acc = _softmax_merge(m_ref, acc_ref, list(range(MOBA_FAR_STATES + 1 + nsub)))
    out = acc[:HEAD_DIM] * (1.0 / jnp.maximum(acc[HEAD_DIM:HEAD_DIM + 1], TINY))
    o_ref[...] = jnp.concatenate([out[:, :tq], out[:, tq:]], axis=0).T.astype(o_ref.dtype)


def _moba_attention(kmat, feat, tbl, b, s, tq):
    t = b * s
    tk = MOBA_BLOCK
    nq = s // tq
    cols = 2 * tq
    n_states = MOBA_FAR_STATES + 1 + tq // tk
    return pl.pallas_call(
        functools.partial(_moba_kernel, nblk=s // tk),
        grid=(b, MOBA_HEADS // 2, nq),
        in_specs=[
            pl.BlockSpec((LANES, tq), lambda bi, p, i: (F_MB_Q + p, bi * nq + i)),
            pl.BlockSpec((s, LANES), lambda bi, p, i: (bi, K_MB + p)),
            pl.BlockSpec((LANES, s), lambda bi, p, i: (F_MB_V + p, bi)),
            pl.BlockSpec((None, SUBLANES, LANES), lambda bi, p, i: (p, 0, 0)),
        ],
        out_specs=pl.BlockSpec((tq, LANES), lambda bi, p, i: (bi * nq + i, p)),
        out_shape=jax.ShapeDtypeStruct((t, MOBA_HEADS * HEAD_DIM), BF16),
        scratch_shapes=[
            pltpu.VMEM((LANES, LANES), F32),
            pltpu.VMEM((tq // tk + 1, tk, cols), F32),
            pltpu.VMEM((LANES, cols), F32),
            pltpu.VMEM((n_states, 1, cols), F32),
            pltpu.VMEM((n_states, HEAD_DIM + ONES_ROWS, cols), F32),
        ],
        compiler_params=pltpu.CompilerParams(
            dimension_semantics=("arbitrary", "arbitrary", "arbitrary"), vmem_limit_bytes=VMEM_LIMIT),
        name="moba",
    )(feat, kmat, feat, tbl)


def _compress_kernel(rk_ref, rv_ref, wk_ref, wv_ref, pos_ref, w2_ref, kn_ref, kc_ref, vc_ref):
    n = rk_ref.shape[0]
    hid = CMP_HIDDEN
    lo = lax.broadcasted_iota(jnp.int32, (n, LANES), 1) < HEAD_DIM
    for idx, (r_ref, w_ref) in enumerate(((rk_ref, wk_ref), (rv_ref, wv_ref))):
        r = r_ref[...]
        first = _dot((r + pos_ref[2 * idx:2 * idx + 1, :]).astype(BF16), w_ref[:, :2 * hid])
        second = _dot((r + pos_ref[2 * idx + 1:2 * idx + 2, :]).astype(BF16), w_ref[:, 2 * hid:])
        h = jax.nn.gelu(first + pltpu.roll(second, shift=n - 1, axis=0))
        for g in range(NSA_KV_GROUPS):
            y = _dot(h[:, g * hid:(g + 1) * hid].astype(BF16), w2_ref[idx])
            if idx == 0:
                kc_ref[g] = _rms(y, kn_ref[...]).astype(kc_ref.dtype)
            else:
                vc_ref[g] = jnp.where(lo, y, 1.0).T.astype(vc_ref.dtype)


def _compress(rk, rv, wk, wv, pos, w2, kn):
    b, n, width = rk.shape
    return pl.pallas_call(
        _compress_kernel,
        grid=(b,),
        in_specs=[
            pl.BlockSpec((None, n, width), lambda i: (i, 0, 0)),
            pl.BlockSpec((None, n, width), lambda i: (i, 0, 0)),
            pl.BlockSpec(wk.shape, lambda i: (0, 0)),
            pl.BlockSpec(wv.shape, lambda i: (0, 0)),
            pl.BlockSpec(pos.shape, lambda i: (0, 0)),
            pl.BlockSpec(w2.shape, lambda i: (0, 0, 0)),
            pl.BlockSpec(kn.shape, lambda i: (0, 0)),
        ],
        out_specs=[
            pl.BlockSpec((None, NSA_KV_GROUPS, n, LANES), lambda i: (i, 0, 0, 0)),
            pl.BlockSpec((None, NSA_KV_GROUPS, LANES, n), lambda i: (i, 0, 0, 0)),
        ],
        out_shape=[
            jax.ShapeDtypeStruct((b, NSA_KV_GROUPS, n, LANES), BF16),
            jax.ShapeDtypeStruct((b, NSA_KV_GROUPS, LANES, n), BF16),
        ],
        compiler_params=pltpu.CompilerParams(
            dimension_semantics=("parallel",), vmem_limit_bytes=VMEM_LIMIT),
        name="nsa_compress",
    )(rk, rv, wk, wv, pos, w2, kn)


def _nsa_kernel(q_ref, kc_ref, vc_ref, ks_ref, vs_ref, kw_ref, vw_ref, gate_ref, tbl_ref, o_ref,
                bdiag_ref, bnear_ref, ov_ref, sel_ref, m_ref, acc_ref, tot_ref, lc_ref, imp_ref):
    tq = q_ref.shape[1]
    nh = NSA_HPG
    cols = nh * tq
    ncmp = kc_ref.shape[0]
    qi = pl.program_id(2)
    q0 = qi * tq

    @pl.when(qi == 0)
    def _():
        j = lax.broadcasted_iota(jnp.int32, (tq, tq), 0)
        i = lax.broadcasted_iota(jnp.int32, (tq, tq), 1)
        bdiag_ref[...] = _bias_tile_t(tbl_ref, nh, i - j)
        bnear_ref[...] = _bias_tile_t(tbl_ref, nh, tq + i - j)
        ss = lax.broadcasted_iota(jnp.int32, (LANES, ncmp), 0) * SLC_BLOCK
        cs = lax.broadcasted_iota(jnp.int32, (LANES, ncmp), 1) * CMP_STRIDE
        ov_ref[...] = jnp.where((cs < ss + SLC_BLOCK) & (cs + CMP_BLOCK > ss), 1.0, 0.0).astype(BF16)

    qs = _stack_heads_t(q_ref[...])

    def gate(branch):
        return jnp.concatenate([gate_ref[branch * nh + h:branch * nh + h + 1, :] for h in range(nh)], axis=1)

    def result(states):
        acc = _softmax_merge(m_ref, acc_ref, states)
        return acc[:HEAD_DIM] * (1.0 / jnp.maximum(acc[HEAD_DIM:HEAD_DIM + 1], TINY))

    j_i = lax.broadcasted_iota(jnp.int32, (tq, cols), 0)
    i_i = lax.broadcasted_iota(jnp.int32, (tq, cols), 1) & (tq - 1)
    causal = j_i <= i_i
    per_tile = tq // SLC_BLOCK
    nwin = WINDOW // tq
    n_sel_states = NSA_SEL_STATES
    group = NSA_FAR_GROUP
    _softmax_init(m_ref, acc_ref, group)

    def tile_steps(*specs, fresh=False):
        units = []
        for state, k_ref, v_ref, kb, fn in specs:
            start = pl.multiple_of(kb * tq, tq)
            units += [(state, k_ref, v_ref, start, fn, slice(c, c + COL_BLOCK)) for c in range(0, cols, COL_BLOCK)]

        def score(unit):
            _, k_ref, _, start, fn, cs = unit
            return fn(_dot(k_ref[pl.ds(start, tq), :], qs[:, cs]), cs)

        def fold(unit, logits):
            state, _, v_ref, start, _, cs = unit
            (_softmax_set if fresh else _softmax_update)(
                m_ref, acc_ref, (state, slice(None), cs), logits, v_ref[:NSA_V_ROWS, pl.ds(start, tq)])

        pending = []
        for unit in units:
            pending.append((unit, score(unit)))
            if len(pending) > SCORE_LOOKAHEAD:
                fold(*pending.pop(0))
                yield
        for item in pending:
            fold(*item)
            yield

    def tiles(*specs, fresh=False):
        for _ in tile_steps(*specs, fresh=fresh):
            pass

    def window_tile(r):
        def fn(z, cs):
            if r == 1:
                z = z + bnear_ref[:, cs]
            if r == nwin:
                z = jnp.where((j_i > i_i)[:, cs], z, -jnp.inf)
            return z + _hide(qi >= r)
        return (n_sel_states + r, kw_ref, vw_ref, jnp.maximum(qi - r, 0), fn)

    window_steps = tile_steps(
        *[window_tile(r) for r in range(nwin, 0, -1)],
        (n_sel_states, kw_ref, vw_ref, qi, lambda z, cs: jnp.where(causal[:, cs], z + bdiag_ref[:, cs], -jnp.inf)),
        fresh=True)

    per_tile_c = tq // CMP_STRIDE
    below = _ceil_to(-(-(REL_MAX_DISTANCE + CMP_BLOCK) // CMP_STRIDE), SUBLANES)
    band = below + per_tile_c
    band_start = pl.multiple_of(jnp.maximum(qi * per_tile_c - below, 0), SUBLANES)
    n_b = band_start + lax.broadcasted_iota(jnp.int32, (band, tq), 0)
    dist_b = q0 + lax.broadcasted_iota(jnp.int32, (band, tq), 1) - (n_b * CMP_STRIDE + CMP_BLOCK - 1)
    band_bias = _bias_tile_t(tbl_ref, nh, dist_b)

    def compressed(rows):
        n_i = lax.broadcasted_iota(jnp.int32, (rows, cols), 0)
        t_i = lax.broadcasted_iota(jnp.int32, (rows, cols), 1) & (tq - 1)
        valid_c = q0 + t_i - (n_i * CMP_STRIDE + CMP_BLOCK - 1) >= 0
        lc_ref[:rows] = _dot(kc_ref[:rows, :], qs)
        lc_ref[pl.ds(band_start, band), :] += band_bias
        logit_c = jnp.where(valid_c, lc_ref[:rows], -jnp.inf)
        p_c = jnp.exp2(logit_c - jnp.maximum(jnp.max(logit_c, axis=0, keepdims=True), NEG))
        p_c = p_c / jnp.maximum(jnp.sum(p_c, axis=0, keepdims=True), TINY)
        tot_ref[...] = gate(0) * _dot(vc_ref[:HEAD_DIM, :rows], p_c.astype(BF16))
        p_sum = p_c[:, 0:tq]
        for h in range(1, nh):
            p_sum = p_sum + p_c[:, h * tq:(h + 1) * tq]
        imp_ref[...] = _dot_hilo_rhs(ov_ref[:, :rows], p_sum)

    step = min(CMP_ROW_STEP, ncmp)
    needed = (qi + 1) * per_tile_c
    for rows in range(step, ncmp + 1, step):
        pl.when((needed > rows - step) & (needed <= rows))(functools.partial(compressed, rows))
    imp = imp_ref[...]
    blk = lax.broadcasted_iota(jnp.int32, (LANES, tq), 0)
    cur = (q0 + lax.broadcasted_iota(jnp.int32, (LANES, tq), 1)) >> SLC_SHIFT
    forced = (blk == 0) | (blk == cur) | (blk == cur - 1)
    visible = blk <= cur
    score = jnp.where(visible & ~forced, imp, NEG)
    top = _top_n_mask_t(score, SLC_TOPN - N_FORCED, between=lambda: next(window_steps, None))
    for _ in window_steps:
        pass
    sel = jnp.where(visible & (forced | (top > 0.5)), 0.0, -jnp.inf)
    sel_ref[...] = jnp.concatenate([sel] * nh, axis=1)

    def hidden(kb, cs, live=None):
        rows = [sel_ref[pl.ds(kb * per_tile + r, 1), cs] for r in range(per_tile)]
        if live is not None:
            rows = [row + _hide(live) for row in rows]
        return jnp.concatenate([jnp.broadcast_to(row, (SLC_BLOCK, COL_BLOCK)) for row in rows], axis=0)

    n_far = jnp.maximum(qi - 1, 0)

    def far_tile(i, u):
        kb = group * i + u
        if u == 0:
            return (u, ks_ref, vs_ref, kb, lambda z, cs: z + hidden(kb, cs))
        clamped = jnp.minimum(kb, n_far - 1)
        return (u, ks_ref, vs_ref, clamped, lambda z, cs: z + hidden(clamped, cs, live=kb < n_far))

    def sel_body(i, carry):
        tiles(*[far_tile(i, u) for u in range(group)])
        return carry

    lax.fori_loop(0, (n_far + group - 1) // group, sel_body, 0)
    near = jnp.maximum(qi - 1, 0)
    tiles((group, ks_ref, vs_ref, near, lambda z, cs: z + bnear_ref[:, cs] + hidden(near, cs, live=qi >= 1)),
          (group + 1, ks_ref, vs_ref, qi,
           lambda z, cs: jnp.where(causal[:, cs], z + bdiag_ref[:, cs] + hidden(qi, cs), -jnp.inf)),
          fresh=True)
    tot = (tot_ref[...] + gate(1) * result(list(range(n_sel_states)))
           + gate(2) * result(list(range(n_sel_states, n_sel_states + nwin + 1))))
    pairs = [jnp.concatenate([tot[:, (2 * c) * tq:(2 * c + 1) * tq], tot[:, (2 * c + 1) * tq:(2 * c + 2) * tq]], axis=0).T
             for c in range(nh // 2)]
    o_ref[...] = jnp.concatenate(pairs, axis=1).astype(o_ref.dtype)


def _nsa_attention(kmat, feat, kc, vc, gates_t, tbl, b, s, tq):
    t = b * s
    nq = s // tq
    ncmp = kc.shape[2]
    assert ncmp % min(CMP_ROW_STEP, ncmp) == 0
    cols = NSA_HPG * tq
    n_states = NSA_SEL_STATES + WINDOW // tq + 1
    k_spec = lambda blk: pl.BlockSpec((s, LANES), lambda bi, g, i: (bi, blk + g))
    v_spec = lambda blk: pl.BlockSpec((LANES, s), lambda bi, g, i: (blk + g, bi))
    return pl.pallas_call(
        _nsa_kernel,
        grid=(b, NSA_KV_GROUPS, nq),
        in_specs=[
            pl.BlockSpec((2 * LANES, tq), lambda bi, g, i: (F_NS_Q // 2 + g, bi * nq + i)),
            pl.BlockSpec((None, None, ncmp, LANES), lambda bi, g, i: (bi, g, 0, 0)),
            pl.BlockSpec((None, None, LANES, ncmp), lambda bi, g, i: (bi, g, 0, 0)),
            k_spec(K_NS_SLC), v_spec(F_NS_SLC_V), k_spec(K_NS_WIN), v_spec(F_NS_WIN_V),
            pl.BlockSpec((LANES, tq), lambda bi, g, i: (g, bi * nq + i)),
            pl.BlockSpec((None, SUBLANES, LANES), lambda bi, g, i: (g, 0, 0)),
        ],
        out_specs=pl.BlockSpec((tq, 2 * LANES), lambda bi, g, i: (bi * nq + i, g)),
        out_shape=jax.ShapeDtypeStruct((t, NSA_HEADS * HEAD_DIM), BF16),
        scratch_shapes=[
            pltpu.VMEM((tq, cols), F32),
            pltpu.VMEM((tq, cols), F32),
            pltpu.VMEM((LANES, ncmp), BF16),
            pltpu.VMEM((LANES, cols), F32),
            pltpu.VMEM((n_states, 1, cols), F32),
            pltpu.VMEM((n_states, NSA_V_ROWS, cols), F32),
            pltpu.VMEM((HEAD_DIM, cols), F32),
            pltpu.VMEM((ncmp, cols), F32),
            pltpu.VMEM((LANES, tq), F32),
        ],
        compiler_params=pltpu.CompilerParams(
            dimension_semantics=("arbitrary", "arbitrary", "arbitrary"), vmem_limit_bytes=VMEM_LIMIT),
        name="nsa",
    )(feat, kc, vc, kmat, feat, kmat, feat, gates_t, tbl)


def _merge_kernel(x_ref, g_ref, oa_ref, ob_ref, oc_ref, wg_ref, wb_ref, wo_ref, o_ref):
    x = x_ref[...]
    d = x.shape[1]
    h = _rms(x, g_ref[...]).astype(BF16)
    mix = None
    row = 0
    for br, src in enumerate((oa_ref, ob_ref, oc_ref)):
        width = src.shape[1]
        gate = jax.nn.sigmoid(_dot(h, wg_ref[:, br * d:(br + 1) * d]))
        term = gate * _dot(src[...], wb_ref[row:row + width, :])
        mix = term if mix is None else mix + term
        row += width
    o_ref[...] = x + _dot(mix.astype(BF16), wo_ref[...])


def _merge(x2, g, oa, ob, oc, wg, wb, wo, tm):
    t, d = x2.shape
    row = lambda w: pl.BlockSpec((tm, w), lambda i: (i, 0))
    full = lambda a: pl.BlockSpec(a.shape, lambda i: (0, 0))
    return pl.pallas_call(
        _merge_kernel,
        grid=(t // tm,),
        in_specs=[row(d), full(g), row(oa.shape[1]), row(ob.shape[1]), row(oc.shape[1]),
                  full(wg), full(wb), full(wo)],
        out_specs=row(d),
        out_shape=jax.ShapeDtypeStruct((t, d), F32),
        compiler_params=pltpu.CompilerParams(
            dimension_semantics=("parallel",), vmem_limit_bytes=VMEM_LIMIT),
        name="merge",
    )(x2, g, oa, ob, oc, wg, wb, wo)


def _ffn_kernel(x_ref, g_ref, wgu_ref, wd_ref, o_ref, *, n_chunks):
    x = x_ref[...]
    d_ff = wd_ref.shape[0]
    ch = d_ff // n_chunks
    h = _rms(x, g_ref[...]).astype(BF16)
    out = x
    for c in range(n_chunks):
        gate = _dot(h, wgu_ref[:, c * ch:(c + 1) * ch])
        up = _dot(h, wgu_ref[:, d_ff + c * ch:d_ff + (c + 1) * ch])
        out = out + _dot((jax.nn.silu(gate) * up).astype(BF16), wd_ref[c * ch:(c + 1) * ch, :])
    o_ref[...] = out


def _ffn(x2, g, wgu, wd, tm):
    t, d = x2.shape
    d_ff = wd.shape[0]
    n_chunks = 2 if d_ff % (2 * LANES) == 0 else 1
    return pl.pallas_call(
        functools.partial(_ffn_kernel, n_chunks=n_chunks),
        grid=(t // tm,),
        in_specs=[
            pl.BlockSpec((tm, d), lambda i: (i, 0)),
            pl.BlockSpec((1, d), lambda i: (0, 0)),
            pl.BlockSpec(wgu.shape, lambda i: (0, 0)),
            pl.BlockSpec(wd.shape, lambda i: (0, 0)),
        ],
        out_specs=pl.BlockSpec((tm, d), lambda i: (i, 0)),
        out_shape=jax.ShapeDtypeStruct((t, d), F32),
        compiler_params=pltpu.CompilerParams(
            dimension_semantics=("parallel",), vmem_limit_bytes=VMEM_LIMIT),
        name="swiglu",
    )(x2, g, wgu, wd)


def _dup_groups(w):
    d = w.shape[0]
    w = w.reshape(d, NSA_KV_GROUPS, 1, HEAD_DIM)
    return jnp.broadcast_to(w, (d, NSA_KV_GROUPS, 2, HEAD_DIM)).reshape(d, NSA_KV_GROUPS * PAIR)


def _half_groups(w):
    d = w.shape[0]
    w = w.reshape(d, NSA_KV_GROUPS, HEAD_DIM)
    return jnp.pad(w, ((0, 0), (0, 0), (0, HEAD_DIM))).reshape(d, NSA_KV_GROUPS * PAIR)


def _pair_gain(g, scale=1.0):
    return jnp.concatenate([g, g]) * scale


def _layer_params(w_in, moba_q_norm, moba_k_norm, nsa_q_norm, nsa_k_norm, nsa_cmp_pos, nsa_cmp_w1, nsa_cmp_w2):
    d = w_in.shape[0]
    sbw, mbw, nsw, kvw = SB_HEADS * HEAD_DIM, MOBA_HEADS * HEAD_DIM, NSA_HEADS * HEAD_DIM, NSA_KV_GROUPS * HEAD_DIM
    sb_q, sb_k, sb_v = (w_in[:, i * sbw:(i + 1) * sbw] for i in range(3))
    o = 3 * sbw
    mb_q, mb_k, mb_v = (w_in[:, o + i * mbw:o + (i + 1) * mbw] for i in range(3))
    o += 3 * mbw
    ns_q = w_in[:, o:o + nsw]
    o += nsw
    kc_w, vc_w, ks_w, vs_w, kw_w, vw_w = (w_in[:, o + i * kvw:o + (i + 1) * kvw] for i in range(6))
    o += 6 * kvw
    gate_w = w_in[:, o:o + N_BRANCHES * NSA_HEADS].reshape(d, N_BRANCHES, NSA_KV_GROUPS, NSA_HPG)
    o += N_BRANCHES * NSA_HEADS
    wg = w_in[:, o:].astype(BF16)
    w = jnp.concatenate([sb_k, mb_k, _dup_groups(ks_w), _dup_groups(kw_w),
                         sb_q, mb_q, ns_q, sb_v, mb_v, _half_groups(vs_w), _half_groups(vw_w)], axis=1).astype(BF16)
    gate_cols = []
    for g in range(NSA_KV_GROUPS):
        cols = gate_w[:, :, g, :].reshape(d, N_BRANCHES * NSA_HPG)
        gate_cols.append(jnp.pad(cols, ((0, 0), (0, LANES - N_BRANCHES * NSA_HPG))))
    wa = jnp.concatenate([kc_w, vc_w] + gate_cols, axis=1).astype(BF16)
    scale = HEAD_DIM ** -0.5
    ones = jnp.ones((LANES,), F32)
    gains = [ones] * 2 + [_pair_gain(moba_k_norm)] * 2 + [_pair_gain(nsa_k_norm[1])] * 2 + [_pair_gain(nsa_k_norm[2])] * 2
    gains += [ones * (scale * LOG2E)] * 2 + [_pair_gain(moba_q_norm, scale * LOG2E)] * 2
    gains += [_pair_gain(nsa_q_norm, scale * LOG2E)] * 4 + [ones] * 8
    gain = jnp.concatenate(gains).reshape(1, (N_K_BLOCKS + N_F_BLOCKS) * LANES)

    half = CMP_BLOCK // 2
    hid = CMP_HIDDEN

    def spread(w1):
        w1 = w1.reshape(2, half, HEAD_DIM, hid)
        cols = []
        for part in range(2):
            for g in range(NSA_KV_GROUPS):
                z = jnp.zeros((half, NSA_KV_GROUPS, HEAD_DIM, hid), F32).at[:, g].set(w1[part])
                cols.append(z.reshape(half * kvw, hid))
        return jnp.concatenate(cols, axis=1).astype(BF16)

    def spread_pos(p):
        p = p.reshape(2, half, 1, HEAD_DIM)
        return jnp.broadcast_to(p, (2, half, NSA_KV_GROUPS, HEAD_DIM)).reshape(2, half * kvw)

    pos = jnp.concatenate([spread_pos(nsa_cmp_pos[0]), spread_pos(nsa_cmp_pos[1])], axis=0)
    w2 = jnp.stack([jnp.concatenate([nsa_cmp_w2[0], nsa_cmp_w2[0]], axis=-1),
                    jnp.pad(nsa_cmp_w2[1], ((0, 0), (0, HEAD_DIM)))]).astype(BF16)
    kn = _pair_gain(nsa_k_norm[0]).reshape(1, LANES)
    return dict(w=w, wa=wa, gain=gain, wg=wg, wk=spread(nsa_cmp_w1[0]), wv=spread(nsa_cmp_w1[1]),
                pos=pos, w2=w2, kn=kn)


def _bias_tables(rel_bias):
    tbl = jnp.pad(rel_bias.T, ((0, 0), (0, LANES - N_BUCKETS)))

    def rows(t, n):
        t = t.reshape(n, -1, LANES)
        return jnp.pad(t, ((0, 0), (0, SUBLANES - t.shape[1]), (0, 0)))

    return rows(tbl[:MOBA_HEADS], MOBA_HEADS // 2), rows(tbl[MOBA_HEADS:], NSA_KV_GROUPS)


def kernel(x, rel_bias, attn_norm, w_in, moba_q_norm, moba_k_norm, nsa_q_norm, nsa_k_norm, nsa_cmp_pos,
           nsa_cmp_w1, nsa_cmp_w2, w_branch, w_out, ffn_norm, w_gate_up, w_down):
    b, s, d = x.shape
    t = b * s
    tm = min(512, t)
    tq_moba = 512
    tq_nsa = 256
    tq_sb, tk_sb = 512, 256
    assert s % tq_moba == 0 and tq_moba % MOBA_BLOCK == 0 and s % tq_sb == 0 and WINDOW % tq_nsa == 0
    assert SLC_TOPN <= s // SLC_BLOCK <= LANES
    tbl_moba, tbl_nsa = _bias_tables(rel_bias)
    x2 = x.reshape(t, d)
    for layer in range(w_in.shape[0]):
        p = _layer_params(w_in[layer], moba_q_norm[layer], moba_k_norm[layer], nsa_q_norm[layer],
                          nsa_k_norm[layer], nsa_cmp_pos[layer], nsa_cmp_w1[layer], nsa_cmp_w2[layer])
        g_attn = attn_norm[layer].reshape(1, d)
        kmat, feat, kc_in, vc_in, gates_t = _inproj(x2, g_attn, p["w"], p["wa"], p["gain"], tm)
        o_a = _sb_attention(kmat, feat, b, s, tq_sb, tk_sb)
        o_b = _moba_attention(kmat, feat, tbl_moba, b, s, tq_moba)
        chunk = CMP_STRIDE * LANES
        kc, vc = _compress(kc_in.reshape(b, s // CMP_STRIDE, chunk), vc_in.reshape(b, s // CMP_STRIDE, chunk),
                           p["wk"], p["wv"], p["pos"], p["w2"], p["kn"])
        o_c = _nsa_attention(kmat, feat, kc, vc, gates_t, tbl_nsa, b, s, tq_nsa)
        x2 = _merge(x2, g_attn, o_a, o_b, o_c, p["wg"], w_branch[layer].astype(BF16),
                    w_out[layer].astype(BF16), tm)
        x2 = _ffn(x2, ffn_norm[layer].reshape(1, d), w_gate_up[layer].astype(BF16),
                  w_down[layer].astype(BF16), tm)
    return x2.reshape(b, s, d)
```

```python
import functools
import math

import jax
import jax.numpy as jnp
from jax import lax
from jax.experimental import pallas as pl
from jax.experimental.pallas import tpu as pltpu

HEAD_DIM = 64
SB_HEADS = 4
MOBA_HEADS = 4
NSA_HEADS = 8
NSA_KV_GROUPS = 2
NSA_HPG = NSA_HEADS // NSA_KV_GROUPS
N_BRANCHES = 3
MOBA_BLOCK = 256
MOBA_TOPK = 3
CMP_BLOCK = 32
CMP_STRIDE = 16
CMP_HIDDEN = 4 * HEAD_DIM
SLC_BLOCK = 64
SLC_TOPN = 16
WINDOW = 512
N_BUCKETS = 32
REL_MAX_DISTANCE = 128
NORM_EPS = 1e-6
NEG = -1e30
BIG = 1e30
TINY = 1e-30
LOG2E = math.log2(math.e)
MOBA_FAR_STATES = 4
NSA_FAR_GROUP = 4
NSA_SEL_STATES = NSA_FAR_GROUP + 2
SB_UNDERFLOW_LOG2 = 160.0

LANES = 128
SUBLANES = 8
COL_BLOCK = 256
SCORE_LOOKAHEAD = 4
N_FORCED = 3
CMP_ROW_STEP = 128
ONES_ROWS = 16
NSA_V_ROWS = HEAD_DIM + ONES_ROWS
PAIR = 2 * HEAD_DIM
VMEM_LIMIT = 56 * 1024 * 1024

F32 = jnp.float32
BF16 = jnp.bfloat16

K_SB, K_MB, K_NS_SLC, K_NS_WIN = 0, 2, 4, 6
N_K_BLOCKS = 8
F_SB_Q, F_MB_Q, F_NS_Q, F_SB_V, F_MB_V, F_NS_SLC_V, F_NS_WIN_V = 0, 2, 4, 8, 10, 12, 14
N_F_BLOCKS = 16
K_MODES = (0, 0, 2, 2, 2, 2, 2, 2)
F_MODES = (1, 1, 2, 2, 2, 2, 2, 2, 0, 0, 0, 0, 3, 3, 3, 3)


SLC_SHIFT = SLC_BLOCK.bit_length() - 1
MOBA_SHIFT = MOBA_BLOCK.bit_length() - 1
assert 1 << SLC_SHIFT == SLC_BLOCK and 1 << MOBA_SHIFT == MOBA_BLOCK


def _ceil_to(n, m):
    return -(-n // m) * m


def _dot(a, b):
    return jnp.dot(a, b, preferred_element_type=F32)


def _dot_hilo_rhs(a, b):
    hi = b.astype(BF16)
    lo = (b - hi.astype(F32)).astype(BF16)
    return _dot(a, hi) + _dot(a, lo)


def _rms(x, g):
    ms = jnp.mean(x * x, axis=-1, keepdims=True)
    return x * lax.rsqrt(ms + NORM_EPS) * g


def _stack_heads_t(q_t):
    tq = q_t.shape[1]
    lo = lax.broadcasted_iota(jnp.int32, (LANES, tq), 0) < HEAD_DIM
    zero = jnp.zeros((LANES, tq), q_t.dtype)
    parts = []
    for c in range(q_t.shape[0] // LANES):
        blk = q_t[c * LANES:(c + 1) * LANES, :]
        parts.append(jnp.where(lo, blk, zero))
        parts.append(jnp.where(lo, zero, blk))
    return jnp.concatenate(parts, axis=1)


def _t5_bucket(dist):
    n = jnp.maximum(dist, 0)
    max_exact = N_BUCKETS // 2
    nf = jnp.maximum(n, 1).astype(F32)
    large = max_exact + (jnp.log(nf / max_exact) / math.log(REL_MAX_DISTANCE / max_exact)
                         * (N_BUCKETS - max_exact)).astype(jnp.int32)
    large = jnp.minimum(large, N_BUCKETS - 1)
    return jnp.where(n < max_exact, n, large)


def _bias_tile_t(tbl_ref, n_heads, dist):
    nk, tq = dist.shape
    bucket = _t5_bucket(dist)
    cols = []
    for h in range(n_heads):
        t = jnp.broadcast_to(tbl_ref[h:h + 1, :], (nk, LANES))
        far = t[:, N_BUCKETS - 1:N_BUCKETS]
        for c in range(tq // LANES):
            cols.append((jnp.take_along_axis(t, bucket[:, c * LANES:(c + 1) * LANES], axis=1) - far) * LOG2E)
    return jnp.concatenate(cols, axis=1)


def _top_n_mask_t(score, n, between=None):
    row = lax.broadcasted_iota(jnp.int32, score.shape, 0).astype(F32)
    sel = jnp.zeros(score.shape, F32)
    s = score
    for _ in range(n):
        m = jnp.max(s, axis=0, keepdims=True)
        idx = jnp.min(jnp.where(s == m, row, float(score.shape[0])), axis=0, keepdims=True)
        pick = row == idx
        sel = jnp.where(pick, 1.0, sel)
        s = jnp.where(pick, -jnp.inf, s)
        if between is not None:
            between()
    return sel


def _post(yb, lo, mode, gain):
    if mode == 2:
        sq = yb * yb
        s_lo = jnp.sum(jnp.where(lo, sq, 0.0), axis=-1, keepdims=True)
        s_hi = jnp.sum(jnp.where(lo, 0.0, sq), axis=-1, keepdims=True)
        yb = yb * lax.rsqrt(jnp.where(lo, s_lo, s_hi) * (1.0 / HEAD_DIM) + NORM_EPS)
    if mode in (1, 2):
        yb = yb * gain
    return yb


def _inproj_kernel(x_ref, g_ref, w_ref, wa_ref, gain_ref, k_ref, f_ref, kc_ref, vc_ref, gate_ref):
    tm = x_ref.shape[0]
    h = _rms(x_ref[...], g_ref[...]).astype(BF16)
    lo = lax.broadcasted_iota(jnp.int32, (tm, LANES), 1) < HEAD_DIM
    for c in range((N_K_BLOCKS + N_F_BLOCKS) // 2):
        y = _dot(h, w_ref[:, c * 2 * LANES:(c + 1) * 2 * LANES])
        for s in range(2):
            blk = 2 * c + s
            gain = gain_ref[:, blk * LANES:(blk + 1) * LANES]
            yb = y[:, s * LANES:(s + 1) * LANES]
            if blk < N_K_BLOCKS:
                k_ref[:, blk * LANES:(blk + 1) * LANES] = _post(yb, lo, K_MODES[blk], gain).astype(BF16)
            else:
                fb = blk - N_K_BLOCKS
                yb = jnp.where(lo, yb, 1.0) if F_MODES[fb] == 3 else _post(yb, lo, F_MODES[fb], gain)
                f_ref[fb * LANES:(fb + 1) * LANES, :] = yb.T.astype(BF16)
    aux = _dot(h, wa_ref[...])
    kc_ref[...] = aux[:, 0:LANES]
    vc_ref[...] = aux[:, LANES:2 * LANES]
    for g in range(NSA_KV_GROUPS):
        gate_ref[g * LANES:(g + 1) * LANES, :] = jax.nn.sigmoid(aux[:, (2 + g) * LANES:(3 + g) * LANES]).T


def _inproj(x2, g, w, wa, gain, tm):
    t, d = x2.shape
    nw = w.shape[1]
    return pl.pallas_call(
        _inproj_kernel,
        grid=(t // tm,),
        in_specs=[
            pl.BlockSpec((tm, d), lambda i: (i, 0)),
            pl.BlockSpec((1, d), lambda i: (0, 0)),
            pl.BlockSpec((d, nw), lambda i: (0, 0)),
            pl.BlockSpec((d, 4 * LANES), lambda i: (0, 0)),
            pl.BlockSpec((1, nw), lambda i: (0, 0)),
        ],
        out_specs=[
            pl.BlockSpec((tm, N_K_BLOCKS * LANES), lambda i: (i, 0)),
            pl.BlockSpec((N_F_BLOCKS * LANES, tm), lambda i: (0, i)),
            pl.BlockSpec((tm, LANES), lambda i: (i, 0)),
            pl.BlockSpec((tm, LANES), lambda i: (i, 0)),
            pl.BlockSpec((NSA_KV_GROUPS * LANES, tm), lambda i: (0, i)),
        ],
        out_shape=[
            jax.ShapeDtypeStruct((t, N_K_BLOCKS * LANES), BF16),
            jax.ShapeDtypeStruct((N_F_BLOCKS * LANES, t), BF16),
            jax.ShapeDtypeStruct((t, LANES), F32),
            jax.ShapeDtypeStruct((t, LANES), F32),
            jax.ShapeDtypeStruct((NSA_KV_GROUPS * LANES, t), F32),
        ],
        compiler_params=pltpu.CompilerParams(
            dimension_semantics=("parallel",), vmem_limit_bytes=VMEM_LIMIT),
        name="inproj",
    )(x2, g, w, wa, gain)


def _sb_kernel(q_ref, k_ref, v_ref, o_ref, r_ref, acc_ref, *, tk):
    tq = q_ref.shape[1]
    cols = 2 * tq
    nsub = tq // tk
    qi = pl.program_id(2)
    qs = _stack_heads_t(q_ref[...])
    j_i = lax.broadcasted_iota(jnp.int32, (tk, cols), 0)
    t_i = lax.broadcasted_iota(jnp.int32, (tk, cols), 1) & (tq - 1)
    after = jnp.where((lax.broadcasted_iota(jnp.int32, (tk, 2 * tk), 1) & (tk - 1))
                      > lax.broadcasted_iota(jnp.int32, (tk, 2 * tk), 0), 1.0, 0.0).astype(BF16)
    r_ref[...] = jnp.zeros(r_ref.shape, F32)
    acc_ref[...] = jnp.zeros(acc_ref.shape, F32)

    def block(kb, strict):
        start = pl.multiple_of(kb * tk, tk)
        z = _dot(k_ref[pl.ds(start, tk), :], qs)
        sp = jnp.maximum(z, 0.0) + jnp.log2(1.0 + jnp.exp2(-jnp.abs(z)))
        stay = sp if strict is None else jnp.where(strict, sp, 0.0)
        hi = stay.astype(BF16)
        lo = (stay - hi.astype(F32)).astype(BF16)
        after_sum = _dot(after, jnp.concatenate([hi, lo], axis=0))
        a = jnp.exp2(z - sp - after_sum - r_ref[...])
        if strict is not None:
            a = jnp.where(strict, a, 0.0)
        acc_ref[...] += _dot(v_ref[:, pl.ds(start, tk)], a.astype(BF16))
        r_ref[...] += jnp.sum(stay, axis=0, keepdims=True)

    for u in range(nsub - 1, -1, -1):
        block(qi * nsub + u, u * tk + j_i < t_i)

    def more(carry):
        j, r_min = carry
        return (j < qi * nsub) & (r_min < SB_UNDERFLOW_LOG2)

    def body(carry):
        j, _ = carry
        block(qi * nsub - 1 - j, None)
        return j + 1, jnp.min(r_ref[...])

    lax.while_loop(more, body, (0, jnp.min(r_ref[...])))
    lo = lax.broadcasted_iota(jnp.int32, (LANES, tq), 0) < HEAD_DIM
    acc = acc_ref[...]
    o_ref[...] = jnp.where(lo, acc[:, :tq], acc[:, tq:]).T.astype(o_ref.dtype)


def _sb_attention(kmat, feat, b, s, tq, tk):
    t = b * s
    nq = s // tq
    return pl.pallas_call(
        functools.partial(_sb_kernel, tk=tk),
        grid=(b, SB_HEADS // 2, nq),
        in_specs=[
            pl.BlockSpec((LANES, tq), lambda bi, p, i: (F_SB_Q + p, bi * nq + i)),
            pl.BlockSpec((s, LANES), lambda bi, p, i: (bi, K_SB + p)),
            pl.BlockSpec((LANES, s), lambda bi, p, i: (F_SB_V + p, bi)),
        ],
        out_specs=pl.BlockSpec((tq, LANES), lambda bi, p, i: (bi * nq + i, p)),
        out_shape=jax.ShapeDtypeStruct((t, SB_HEADS * HEAD_DIM), BF16),
        scratch_shapes=[pltpu.VMEM((1, 2 * tq), F32), pltpu.VMEM((LANES, 2 * tq), F32)],
        compiler_params=pltpu.CompilerParams(
            dimension_semantics=("parallel", "parallel", "arbitrary"), vmem_limit_bytes=VMEM_LIMIT),
        name="stickbreak",
    )(feat, kmat, feat)


def _softmax_init(m_ref, acc_ref, n_states):
    m_ref[:n_states] = jnp.full((n_states,) + m_ref.shape[1:], NEG, F32)
    acc_ref[:n_states] = jnp.zeros((n_states,) + acc_ref.shape[1:], F32)


def _softmax_set(m_ref, acc_ref, s, logits, v_t):
    m = jnp.maximum(jnp.max(logits, axis=0, keepdims=True), NEG)
    acc_ref[s] = _dot(v_t, jnp.exp2(logits - m).astype(BF16))
    m_ref[s] = m


def _softmax_update(m_ref, acc_ref, s, logits, v_t):
    m_old = m_ref[s]
    m_new = jnp.maximum(m_old, jnp.max(logits, axis=0, keepdims=True))
    p = jnp.exp2(logits - m_new)
    acc_ref[s] = jnp.exp2(m_old - m_new) * acc_ref[s] + _dot(v_t, p.astype(BF16))
    m_ref[s] = m_new


def _softmax_merge(m_ref, acc_ref, states):
    m = m_ref[states[0]]
    for s in states[1:]:
        m = jnp.maximum(m, m_ref[s])
    acc = None
    for s in states:
        w = jnp.exp2(m_ref[s] - m)
        acc = w * acc_ref[s] if acc is None else acc + w * acc_ref[s]
    return acc


def _hide(live):
    return jnp.where(live, 0.0, -jnp.inf)


def _grouped_loop(n, group, body, tail):
    shift = group.bit_length() - 1
    assert 1 << shift == group
    full = n >> shift
    lax.fori_loop(0, full, body, 0)
    first = full << shift
    count = group >> 1
    while count:
        pl.when((n & count) != 0)(functools.partial(tail, first, count))
        first = first + (n & count)
        count >>= 1


def _moba_kernel(q_ref, k_ref, v_ref, tbl_ref, o_ref, km_ref, bias_ref, sel_ref, m_ref, acc_ref, *, nblk):
    tk = MOBA_BLOCK
    tq = q_ref.shape[1]
    nsub = tq // tk
    cols = 2 * tq
    qi = pl.program_id(2)

    @pl.when(qi == 0)
    def _():
        km_ref[...] = jnp.zeros(km_ref.shape, F32)

        def mean_body(n, carry):
            kb = k_ref[pl.ds(pl.multiple_of(n * tk, tk), tk), :].astype(F32)
            km_ref[pl.ds(n, 1), :] = jnp.mean(kb, axis=0, keepdims=True)
            return carry

        lax.fori_loop(0, nblk, mean_body, 0)
        j = lax.broadcasted_iota(jnp.int32, (tk, tq), 0)
        i = lax.broadcasted_iota(jnp.int32, (tk, tq), 1)
        for e in range(-1, nsub):
            bias_ref[e + 1] = _bias_tile_t(tbl_ref, 2, i - tk * e - j)

    qs = _stack_heads_t(q_ref[...])
    sub_s = lax.broadcasted_iota(jnp.int32, (LANES, cols), 1) & (tq - 1)
    past = lax.broadcasted_iota(jnp.int32, (LANES, cols), 0) < nsub * qi + (sub_s >> MOBA_SHIFT)
    score = jnp.where(past, _dot(km_ref[...].astype(BF16), qs), NEG)
    sel_ref[...] = jnp.where(past & (_top_n_mask_t(score, MOBA_TOPK) > 0.5), 0.0, -jnp.inf)
    _softmax_init(m_ref, acc_ref, MOBA_FAR_STATES)
    ones = jnp.ones((ONES_ROWS, tk), BF16)

    def tiles(*specs, fresh=False):
        units = []
        for state, n, fn in specs:
            start = pl.multiple_of(n * tk, tk)
            units += [(state, start, fn, slice(c, c + COL_BLOCK)) for c in range(0, cols, COL_BLOCK)]

        def score(unit):
            _, start, fn, cs = unit
            return fn(_dot(k_ref[pl.ds(start, tk), :], qs[:, cs]), cs)

        def fold(unit, logits):
            state, start, _, cs = unit
            head = cs.start // tq
            v_t = jnp.concatenate([v_ref[head * HEAD_DIM:(head + 1) * HEAD_DIM, pl.ds(start, tk)], ones], axis=0)
            (_softmax_set if fresh else _softmax_update)(m_ref, acc_ref, (state, slice(None), cs), logits, v_t)

        pending = []
        for unit in units:
            pending.append((unit, score(unit)))
            if len(pending) > SCORE_LOOKAHEAD:
                fold(*pending.pop(0))
        for item in pending:
            fold(*item)

    def hidden(n, cs, live=None):
        row = sel_ref[pl.ds(n, 1), cs]
        if live is not None:
            row = row + _hide(live)
        return jnp.broadcast_to(row, (tk, COL_BLOCK))

    n_far = jnp.maximum(nsub * qi - 1, 0)

    group = MOBA_FAR_STATES

    def far_tiles(first, count):
        tiles(*[(u, first + u, lambda z, cs, n=first + u: z + hidden(n, cs)) for u in range(count)])

    def far_body(i, carry):
        far_tiles(group * i, group)
        return carry

    _grouped_loop(n_far, group, far_body, far_tiles)
    prev = jnp.maximum(nsub * qi - 1, 0)
    j_i = lax.broadcasted_iota(jnp.int32, (tk, cols), 0)
    t_i = lax.broadcasted_iota(jnp.int32, (tk, cols), 1) & (tq - 1)

    def own_tile(e):
        causal_own = ((t_i >> MOBA_SHIFT) == e) & (j_i <= t_i - tk * e)
        return (MOBA_FAR_STATES + 1 + e, nsub * qi + e,
                lambda z, cs: z + bias_ref[e + 1, :, cs] + jnp.where(causal_own[:, cs], 0.0, hidden(nsub * qi + e, cs)))

    tiles((MOBA_FAR_STATES, prev, lambda z, cs: z + bias_ref[0, :, cs] + hidden(prev, cs, live=qi >= 1)),
          *[own_tile(e) for e in range(nsub)], fresh=True)
    acc = _softmax_merge(m_ref, acc_ref, list(range(MOBA_FAR_STATES + 1 + nsub)))
    out = acc[:HEAD_DIM] * (1.0 / jnp.maximum(acc[HEAD_DIM:HEAD_DIM + 1], TINY))
    o_ref[...] = jnp.concatenate([out[:, :tq], out[:, tq:]], axis=0).T.astype(o_ref.dtype)


def _moba_attention(kmat, feat, tbl, b, s, tq):
    t = b * s
    tk = MOBA_BLOCK
    nq = s // tq
    cols = 2 * tq
    n_states = MOBA_FAR_STATES + 1 + tq // tk
    return pl.pallas_call(
        functools.partial(_moba_kernel, nblk=s // tk),
        grid=(b, MOBA_HEADS // 2, nq),
        in_specs=[
            pl.BlockSpec((LANES, tq), lambda bi, p, i: (F_MB_Q + p, bi * nq + i)),
            pl.BlockSpec((s, LANES), lambda bi, p, i: (bi, K_MB + p)),
            pl.BlockSpec((LANES, s), lambda bi, p, i: (F_MB_V + p, bi)),
            pl.BlockSpec((None, SUBLANES, LANES), lambda bi, p, i: (p, 0, 0)),
        ],
        out_specs=pl.BlockSpec((tq, LANES), lambda bi, p, i: (bi * nq + i, p)),
        out_shape=jax.ShapeDtypeStruct((t, MOBA_HEADS * HEAD_DIM), BF16),
        scratch_shapes=[
            pltpu.VMEM((LANES, LANES), F32),
            pltpu.VMEM((tq // tk + 1, tk, cols), F32),
            pltpu.VMEM((LANES, cols), F32),
            pltpu.VMEM((n_states, 1, cols), F32),
            pltpu.VMEM((n_states, HEAD_DIM + ONES_ROWS, cols), F32),
        ],
        compiler_params=pltpu.CompilerParams(
            dimension_semantics=("arbitrary", "arbitrary", "arbitrary"), vmem_limit_bytes=VMEM_LIMIT),
        name="moba",
    )(feat, kmat, feat, tbl)


def _compress_kernel(rk_ref, rv_ref, wk_ref, wv_ref, pos_ref, w2_ref, kn_ref, kc_ref, vc_ref):
    n = rk_ref.shape[0]
    hid = CMP_HIDDEN
    lo = lax.broadcasted_iota(jnp.int32, (n, LANES), 1) < HEAD_DIM
    for idx, (r_ref, w_ref) in enumerate(((rk_ref, wk_ref), (rv_ref, wv_ref))):
        r = r_ref[...]
        first = _dot((r + pos_ref[2 * idx:2 * idx + 1, :]).astype(BF16), w_ref[:, :2 * hid])
        second = _dot((r + pos_ref[2 * idx + 1:2 * idx + 2, :]).astype(BF16), w_ref[:, 2 * hid:])
        h = jax.nn.gelu(first + pltpu.roll(second, shift=n - 1, axis=0))
        for g in range(NSA_KV_GROUPS):
            y = _dot(h[:, g * hid:(g + 1) * hid].astype(BF16), w2_ref[idx])
            if idx == 0:
                kc_ref[g] = _rms(y, kn_ref[...]).astype(kc_ref.dtype)
            else:
                vc_ref[g] = jnp.where(lo, y, 1.0).T.astype(vc_ref.dtype)


def _compress(rk, rv, wk, wv, pos, w2, kn):
    b, n, width = rk.shape
    return pl.pallas_call(
        _compress_kernel,
        grid=(b,),
        in_specs=[
            pl.BlockSpec((None, n, width), lambda i: (i, 0, 0)),
            pl.BlockSpec((None, n, width), lambda i: (i, 0, 0)),
            pl.BlockSpec(wk.shape, lambda i: (0, 0)),
            pl.BlockSpec(wv.shape, lambda i: (0, 0)),
            pl.BlockSpec(pos.shape, lambda i: (0, 0)),
            pl.BlockSpec(w2.shape, lambda i: (0, 0, 0)),
            pl.BlockSpec(kn.shape, lambda i: (0, 0)),
        ],
        out_specs=[
            pl.BlockSpec((None, NSA_KV_GROUPS, n, LANES), lambda i: (i, 0, 0, 0)),
            pl.BlockSpec((None, NSA_KV_GROUPS, LANES, n), lambda i: (i, 0, 0, 0)),
        ],
        out_shape=[
            jax.ShapeDtypeStruct((b, NSA_KV_GROUPS, n, LANES), BF16),
            jax.ShapeDtypeStruct((b, NSA_KV_GROUPS, LANES, n), BF16),
        ],
        compiler_params=pltpu.CompilerParams(
            dimension_semantics=("parallel",), vmem_limit_bytes=VMEM_LIMIT),
        name="nsa_compress",
    )(rk, rv, wk, wv, pos, w2, kn)


def _nsa_kernel(q_ref, kc_ref, vc_ref, ks_ref, vs_ref, kw_ref, vw_ref, gate_ref, tbl_ref, o_ref,
                bdiag_ref, bnear_ref, ov_ref, sel_ref, m_ref, acc_ref, tot_ref, lc_ref, imp_ref):
    tq = q_ref.shape[1]
    nh = NSA_HPG
    cols = nh * tq
    ncmp = kc_ref.shape[0]
    qi = pl.program_id(2)
    q0 = qi * tq

    @pl.when(qi == 0)
    def _():
        j = lax.broadcasted_iota(jnp.int32, (tq, tq), 0)
        i = lax.broadcasted_iota(jnp.int32, (tq, tq), 1)
        bdiag_ref[...] = _bias_tile_t(tbl_ref, nh, i - j)
        bnear_ref[...] = _bias_tile_t(tbl_ref, nh, tq + i - j)
        ss = lax.broadcasted_iota(jnp.int32, (LANES, ncmp), 0) * SLC_BLOCK
        cs = lax.broadcasted_iota(jnp.int32, (LANES, ncmp), 1) * CMP_STRIDE
        ov_ref[...] = jnp.where((cs < ss + SLC_BLOCK) & (cs + CMP_BLOCK > ss), 1.0, 0.0).astype(BF16)

    qs = _stack_heads_t(q_ref[...])

    def gate(branch):
        return jnp.concatenate([gate_ref[branch * nh + h:branch * nh + h + 1, :] for h in range(nh)], axis=1)

    def result(states):
        acc = _softmax_merge(m_ref, acc_ref, states)
        return acc[:HEAD_DIM] * (1.0 / jnp.maximum(acc[HEAD_DIM:HEAD_DIM + 1], TINY))

    j_i = lax.broadcasted_iota(jnp.int32, (tq, cols), 0)
    i_i = lax.broadcasted_iota(jnp.int32, (tq, cols), 1) & (tq - 1)
    causal = j_i <= i_i
    per_tile = tq // SLC_BLOCK
    nwin = WINDOW // tq
    n_sel_states = NSA_SEL_STATES
    group = NSA_FAR_GROUP
    _softmax_init(m_ref, acc_ref, group)

    def tile_steps(*specs, fresh=False):
        units = []
        for state, k_ref, v_ref, kb, fn in specs:
            start = pl.multiple_of(kb * tq, tq)
            units += [(state, k_ref, v_ref, start, fn, slice(c, c + COL_BLOCK)) for c in range(0, cols, COL_BLOCK)]

        def score(unit):
            _, k_ref, _, start, fn, cs = unit
            return fn(_dot(k_ref[pl.ds(start, tq), :], qs[:, cs]), cs)

        def fold(unit, logits):
            state, _, v_ref, start, _, cs = unit
            (_softmax_set if fresh else _softmax_update)(
                m_ref, acc_ref, (state, slice(None), cs), logits, v_ref[:NSA_V_ROWS, pl.ds(start, tq)])

        pending = []
        for unit in units:
            pending.append((unit, score(unit)))
            if len(pending) > SCORE_LOOKAHEAD:
                fold(*pending.pop(0))
                yield
        for item in pending:
            fold(*item)
            yield

    def tiles(*specs, fresh=False):
        for _ in tile_steps(*specs, fresh=fresh):
            pass

    def window_tile(r):
        def fn(z, cs):
            if r == 1:
                z = z + bnear_ref[:, cs]
            if r == nwin:
                z = jnp.where((j_i > i_i)[:, cs], z, -jnp.inf)
            return z + _hide(qi >= r)
        return (n_sel_states + r, kw_ref, vw_ref, jnp.maximum(qi - r, 0), fn)

    window_steps = tile_steps(
        *[window_tile(r) for r in range(nwin, 0, -1)],
        (n_sel_states, kw_ref, vw_ref, qi, lambda z, cs: jnp.where(causal[:, cs], z + bdiag_ref[:, cs], -jnp.inf)),
        fresh=True)

    per_tile_c = tq // CMP_STRIDE
    below = _ceil_to(-(-(REL_MAX_DISTANCE + CMP_BLOCK) // CMP_STRIDE), SUBLANES)
    band = below + per_tile_c
    band_start = pl.multiple_of(jnp.maximum(qi * per_tile_c - below, 0), SUBLANES)
    n_b = band_start + lax.broadcasted_iota(jnp.int32, (band, tq), 0)
    dist_b = q0 + lax.broadcasted_iota(jnp.int32, (band, tq), 1) - (n_b * CMP_STRIDE + CMP_BLOCK - 1)
    band_bias = _bias_tile_t(tbl_ref, nh, dist_b)

    def compressed(rows):
        n_i = lax.broadcasted_iota(jnp.int32, (rows, cols), 0)
        t_i = lax.broadcasted_iota(jnp.int32, (rows, cols), 1) & (tq - 1)
        valid_c = q0 + t_i - (n_i * CMP_STRIDE + CMP_BLOCK - 1) >= 0
        lc_ref[:rows] = _dot(kc_ref[:rows, :], qs)
        lc_ref[pl.ds(band_start, band), :] += band_bias
        logit_c = jnp.where(valid_c, lc_ref[:rows], -jnp.inf)
        p_c = jnp.exp2(logit_c - jnp.maximum(jnp.max(logit_c, axis=0, keepdims=True), NEG))
        p_c = p_c / jnp.maximum(jnp.sum(p_c, axis=0, keepdims=True), TINY)
        tot_ref[...] = gate(0) * _dot(vc_ref[:HEAD_DIM, :rows], p_c.astype(BF16))
        p_sum = p_c[:, 0:tq]
        for h in range(1, nh):
            p_sum = p_sum + p_c[:, h * tq:(h + 1) * tq]
        imp_ref[...] = _dot_hilo_rhs(ov_ref[:, :rows], p_sum)

    step = min(CMP_ROW_STEP, ncmp)
    needed = (qi + 1) * per_tile_c
    for rows in range(step, ncmp + 1, step):
        pl.when((needed > rows - step) & (needed <= rows))(functools.partial(compressed, rows))
    imp = imp_ref[...]
    blk = lax.broadcasted_iota(jnp.int32, (LANES, tq), 0)
    cur = (q0 + lax.broadcasted_iota(jnp.int32, (LANES, tq), 1)) >> SLC_SHIFT
    forced = (blk == 0) | (blk == cur) | (blk == cur - 1)
    visible = blk <= cur
    score = jnp.where(visible & ~forced, imp, NEG)
    top = _top_n_mask_t(score, SLC_TOPN - N_FORCED, between=lambda: next(window_steps, None))
    for _ in window_steps:
        pass
    sel = jnp.where(visible & (forced | (top > 0.5)), 0.0, -jnp.inf)
    sel_ref[...] = jnp.concatenate([sel] * nh, axis=1)

    def hidden(kb, cs, live=None):
        rows = [sel_ref[pl.ds(kb * per_tile + r, 1), cs] for r in range(per_tile)]
        if live is not None:
            rows = [row + _hide(live) for row in rows]
        return jnp.concatenate([jnp.broadcast_to(row, (SLC_BLOCK, COL_BLOCK)) for row in rows], axis=0)

    n_far = jnp.maximum(qi - 1, 0)

    def far_tiles(first, count):
        tiles(*[(u, ks_ref, vs_ref, first + u, lambda z, cs, kb=first + u: z + hidden(kb, cs)) for u in range(count)])

    def sel_body(i, carry):
        far_tiles(group * i, group)
        return carry

    _grouped_loop(n_far, group, sel_body, far_tiles)
    near = jnp.maximum(qi - 1, 0)
    tiles((group, ks_ref, vs_ref, near, lambda z, cs: z + bnear_ref[:, cs] + hidden(near, cs, live=qi >= 1)),
          (group + 1, ks_ref, vs_ref, qi,
           lambda z, cs: jnp.where(causal[:, cs], z + bdiag_ref[:, cs] + hidden(qi, cs), -jnp.inf)),
          fresh=True)
    tot = (tot_ref[...] + gate(1) * result(list(range(n_sel_states)))
           + gate(2) * result(list(range(n_sel_states, n_sel_states + nwin + 1))))
    pairs = [jnp.concatenate([tot[:, (2 * c) * tq:(2 * c + 1) * tq], tot[:, (2 * c + 1) * tq:(2 * c + 2) * tq]], axis=0).T
             for c in range(nh // 2)]
    o_ref[...] = jnp.concatenate(pairs, axis=1).astype(o_ref.dtype)


def _nsa_attention(kmat, feat, kc, vc, gates_t, tbl, b, s, tq):
    t = b * s
    nq = s // tq
    ncmp = kc.shape[2]
    assert ncmp % min(CMP_ROW_STEP, ncmp) == 0
    cols = NSA_HPG * tq
    n_states = NSA_SEL_STATES + WINDOW // tq + 1
    k_spec = lambda blk: pl.BlockSpec((s, LANES), lambda bi, g, i: (bi, blk + g))
    v_spec = lambda blk: pl.BlockSpec((LANES, s), lambda bi, g, i: (blk + g, bi))
    return pl.pallas_call(
        _nsa_kernel,
        grid=(b, NSA_KV_GROUPS, nq),
        in_specs=[
            pl.BlockSpec((2 * LANES, tq), lambda bi, g, i: (F_NS_Q // 2 + g, bi * nq + i)),
            pl.BlockSpec((None, None, ncmp, LANES), lambda bi, g, i: (bi, g, 0, 0)),
            pl.BlockSpec((None, None, LANES, ncmp), lambda bi, g, i: (bi, g, 0, 0)),
            k_spec(K_NS_SLC), v_spec(F_NS_SLC_V), k_spec(K_NS_WIN), v_spec(F_NS_WIN_V),
            pl.BlockSpec((LANES, tq), lambda bi, g, i: (g, bi * nq + i)),
            pl.BlockSpec((None, SUBLANES, LANES), lambda bi, g, i: (g, 0, 0)),
        ],
        out_specs=pl.BlockSpec((tq, 2 * LANES), lambda bi, g, i: (bi * nq + i, g)),
        out_shape=jax.ShapeDtypeStruct((t, NSA_HEADS * HEAD_DIM), BF16),
        scratch_shapes=[
            pltpu.VMEM((tq, cols), F32),
            pltpu.VMEM((tq, cols), F32),
            pltpu.VMEM((LANES, ncmp), BF16),
            pltpu.VMEM((LANES, cols), F32),
            pltpu.VMEM((n_states, 1, cols), F32),
            pltpu.VMEM((n_states, NSA_V_ROWS, cols), F32),
            pltpu.VMEM((HEAD_DIM, cols), F32),
            pltpu.VMEM((ncmp, cols), F32),
            pltpu.VMEM((LANES, tq), F32),
        ],
        compiler_params=pltpu.CompilerParams(
            dimension_semantics=("arbitrary", "arbitrary", "arbitrary"), vmem_limit_bytes=VMEM_LIMIT),
        name="nsa",
    )(feat, kc, vc, kmat, feat, kmat, feat, gates_t, tbl)


def _merge_kernel(x_ref, g_ref, oa_ref, ob_ref, oc_ref, wg_ref, wb_ref, wo_ref, o_ref):
    x = x_ref[...]
    d = x.shape[1]
    h = _rms(x, g_ref[...]).astype(BF16)
    mix = None
    row = 0
    for br, src in enumerate((oa_ref, ob_ref, oc_ref)):
        width = src.shape[1]
        gate = jax.nn.sigmoid(_dot(h, wg_ref[:, br * d:(br + 1) * d]))
        term = gate * _dot(src[...], wb_ref[row:row + width, :])
        mix = term if mix is None else mix + term
        row += width
    o_ref[...] = x + _dot(mix.astype(BF16), wo_ref[...])


def _merge(x2, g, oa, ob, oc, wg, wb, wo, tm):
    t, d = x2.shape
    row = lambda w: pl.BlockSpec((tm, w), lambda i: (i, 0))
    full = lambda a: pl.BlockSpec(a.shape, lambda i: (0, 0))
    return pl.pallas_call(
        _merge_kernel,
        grid=(t // tm,),
        in_specs=[row(d), full(g), row(oa.shape[1]), row(ob.shape[1]), row(oc.shape[1]),
                  full(wg), full(wb), full(wo)],
        out_specs=row(d),
        out_shape=jax.ShapeDtypeStruct((t, d), F32),
        compiler_params=pltpu.CompilerParams(
            dimension_semantics=("parallel",), vmem_limit_bytes=VMEM_LIMIT),
        name="merge",
    )(x2, g, oa, ob, oc, wg, wb, wo)


def _ffn_kernel(x_ref, g_ref, wgu_ref, wd_ref, o_ref, *, n_chunks):
    x = x_ref[...]
    d_ff = wd_ref.shape[0]
    ch = d_ff // n_chunks
    h = _rms(x, g_ref[...]).astype(BF16)
    out = x
    for c in range(n_chunks):
        gate = _dot(h, wgu_ref[:, c * ch:(c + 1) * ch])
        up = _dot(h, wgu_ref[:, d_ff + c * ch:d_ff + (c + 1) * ch])
        out = out + _dot((jax.nn.silu(gate) * up).astype(BF16), wd_ref[c * ch:(c + 1) * ch, :])
    o_ref[...] = out


def _ffn(x2, g, wgu, wd, tm):
    t, d = x2.shape
    d_ff = wd.shape[0]
    n_chunks = 2 if d_ff % (2 * LANES) == 0 else 1
    return pl.pallas_call(
        functools.partial(_ffn_kernel, n_chunks=n_chunks),
        grid=(t // tm,),
        in_specs=[
            pl.BlockSpec((tm, d), lambda i: (i, 0)),
            pl.BlockSpec((1, d), lambda i: (0, 0)),
            pl.BlockSpec(wgu.shape, lambda i: (0, 0)),
            pl.BlockSpec(wd.shape, lambda i: (0, 0)),
        ],
        out_specs=pl.BlockSpec((tm, d), lambda i: (i, 0)),
        out_shape=jax.ShapeDtypeStruct((t, d), F32),
        compiler_params=pltpu.CompilerParams(
            dimension_semantics=("parallel",), vmem_limit_bytes=VMEM_LIMIT),
        name="swiglu",
    )(x2, g, wgu, wd)


def _dup_groups(w):
    d = w.shape[0]
    w = w.reshape(d, NSA_KV_GROUPS, 1, HEAD_DIM)
    return jnp.broadcast_to(w, (d, NSA_KV_GROUPS, 2, HEAD_DIM)).reshape(d, NSA_KV_GROUPS * PAIR)


def _half_groups(w):
    d = w.shape[0]
    w = w.reshape(d, NSA_KV_GROUPS, HEAD_DIM)
    return jnp.pad(w, ((0, 0), (0, 0), (0, HEAD_DIM))).reshape(d, NSA_KV_GROUPS * PAIR)


def _pair_gain(g, scale=1.0):
    return jnp.concatenate([g, g]) * scale


def _layer_params(w_in, moba_q_norm, moba_k_norm, nsa_q_norm, nsa_k_norm, nsa_cmp_pos, nsa_cmp_w1, nsa_cmp_w2):
    d = w_in.shape[0]
    sbw, mbw, nsw, kvw = SB_HEADS * HEAD_DIM, MOBA_HEADS * HEAD_DIM, NSA_HEADS * HEAD_DIM, NSA_KV_GROUPS * HEAD_DIM
    sb_q, sb_k, sb_v = (w_in[:, i * sbw:(i + 1) * sbw] for i in range(3))
    o = 3 * sbw
    mb_q, mb_k, mb_v = (w_in[:, o + i * mbw:o + (i + 1) * mbw] for i in range(3))
    o += 3 * mbw
    ns_q = w_in[:, o:o + nsw]
    o += nsw
    kc_w, vc_w, ks_w, vs_w, kw_w, vw_w = (w_in[:, o + i * kvw:o + (i + 1) * kvw] for i in range(6))
    o += 6 * kvw
    gate_w = w_in[:, o:o + N_BRANCHES * NSA_HEADS].reshape(d, N_BRANCHES, NSA_KV_GROUPS, NSA_HPG)
    o += N_BRANCHES * NSA_HEADS
    wg = w_in[:, o:].astype(BF16)
    w = jnp.concatenate([sb_k, mb_k, _dup_groups(ks_w), _dup_groups(kw_w),
                         sb_q, mb_q, ns_q, sb_v, mb_v, _half_groups(vs_w), _half_groups(vw_w)], axis=1).astype(BF16)
    gate_cols = []
    for g in range(NSA_KV_GROUPS):
        cols = gate_w[:, :, g, :].reshape(d, N_BRANCHES * NSA_HPG)
        gate_cols.append(jnp.pad(cols, ((0, 0), (0, LANES - N_BRANCHES * NSA_HPG))))
    wa = jnp.concatenate([kc_w, vc_w] + gate_cols, axis=1).astype(BF16)
    scale = HEAD_DIM ** -0.5
    ones = jnp.ones((LANES,), F32)
    gains = [ones] * 2 + [_pair_gain(moba_k_norm)] * 2 + [_pair_gain(nsa_k_norm[1])] * 2 + [_pair_gain(nsa_k_norm[2])] * 2
    gains += [ones * (scale * LOG2E)] * 2 + [_pair_gain(moba_q_norm, scale * LOG2E)] * 2
    gains += [_pair_gain(nsa_q_norm, scale * LOG2E)] * 4 + [ones] * 8
    gain = jnp.concatenate(gains).reshape(1, (N_K_BLOCKS + N_F_BLOCKS) * LANES)

    half = CMP_BLOCK // 2
    hid = CMP_HIDDEN

    def spread(w1):
        w1 = w1.reshape(2, half, HEAD_DIM, hid)
        cols = []
        for part in range(2):
            for g in range(NSA_KV_GROUPS):
                z = jnp.zeros((half, NSA_KV_GROUPS, HEAD_DIM, hid), F32).at[:, g].set(w1[part])
                cols.append(z.reshape(half * kvw, hid))
        return jnp.concatenate(cols, axis=1).astype(BF16)

    def spread_pos(p):
        p = p.reshape(2, half, 1, HEAD_DIM)
        return jnp.broadcast_to(p, (2, half, NSA_KV_GROUPS, HEAD_DIM)).reshape(2, half * kvw)

    pos = jnp.concatenate([spread_pos(nsa_cmp_pos[0]), spread_pos(nsa_cmp_pos[1])], axis=0)
    w2 = jnp.stack([jnp.concatenate([nsa_cmp_w2[0], nsa_cmp_w2[0]], axis=-1),
                    jnp.pad(nsa_cmp_w2[1], ((0, 0), (0, HEAD_DIM)))]).astype(BF16)
    kn = _pair_gain(nsa_k_norm[0]).reshape(1, LANES)
    return dict(w=w, wa=wa, gain=gain, wg=wg, wk=spread(nsa_cmp_w1[0]), wv=spread(nsa_cmp_w1[1]),
                pos=pos, w2=w2, kn=kn)


def _bias_tables(rel_bias):
    tbl = jnp.pad(rel_bias.T, ((0, 0), (0, LANES - N_BUCKETS)))

    def rows(t, n):
        t = t.reshape(n, -1, LANES)
        return jnp.pad(t, ((0, 0), (0, SUBLANES - t.shape[1]), (0, 0)))

    return rows(tbl[:MOBA_HEADS], MOBA_HEADS // 2), rows(tbl[MOBA_HEADS:], NSA_KV_GROUPS)


def kernel(x, rel_bias, attn_norm, w_in, moba_q_norm, moba_k_norm, nsa_q_norm, nsa_k_norm, nsa_cmp_pos,
           nsa_cmp_w1, nsa_cmp_w2, w_branch, w_out, ffn_norm, w_gate_up, w_down):
    b, s, d = x.shape
    t = b * s
    tm = min(512, t)
    tq_moba = 512
    tq_nsa = 256
    tq_sb, tk_sb = 512, 256
    assert s % tq_moba == 0 and tq_moba % MOBA_BLOCK == 0 and s % tq_sb == 0 and WINDOW % tq_nsa == 0
    assert SLC_TOPN <= s // SLC_BLOCK <= LANES
    tbl_moba, tbl_nsa = _bias_tables(rel_bias)
    x2 = x.reshape(t, d)
    for layer in range(w_in.shape[0]):
        p = _layer_params(w_in[layer], moba_q_norm[layer], moba_k_norm[layer], nsa_q_norm[layer],
                          nsa_k_norm[layer], nsa_cmp_pos[layer], nsa_cmp_w1[layer], nsa_cmp_w2[layer])
        g_attn = attn_norm[layer].reshape(1, d)
        kmat, feat, kc_in, vc_in, gates_t = _inproj(x2, g_attn, p["w"], p["wa"], p["gain"], tm)
        o_a = _sb_attention(kmat, feat, b, s, tq_sb, tk_sb)
        o_b = _moba_attention(kmat, feat, tbl_moba, b, s, tq_moba)
        chunk = CMP_STRIDE * LANES
        kc, vc = _compress(kc_in.reshape(b, s // CMP_STRIDE, chunk), vc_in.reshape(b, s // CMP_STRIDE, chunk),
                           p["wk"], p["wv"], p["pos"], p["w2"], p["kn"])
        o_c = _nsa_attention(kmat, feat, kc, vc, gates_t, tbl_nsa, b, s, tq_nsa)
        x2 = _merge(x2, g_attn, o_a, o_b, o_c, p["wg"], w_branch[layer].astype(BF16),
                    w_out[layer].astype(BF16), tm)
        x2 = _ffn(x2, ffn_norm[layer].reshape(1, d), w_gate_up[layer].astype(BF16),
                  w_down[layer].astype(BF16), tm)
    return x2.reshape(b, s, d)
```

```python
import functools
import math

import jax
import jax.numpy as jnp
from jax import lax
from jax.experimental import pallas as pl
from jax.experimental.pallas import tpu as pltpu

HEAD_DIM = 64
SB_HEADS = 4
MOBA_HEADS = 4
NSA_HEADS = 8
NSA_KV_GROUPS = 2
NSA_HPG = NSA_HEADS // NSA_KV_GROUPS
N_BRANCHES = 3
MOBA_BLOCK = 256
MOBA_TOPK = 3
CMP_BLOCK = 32
CMP_STRIDE = 16
CMP_HIDDEN = 4 * HEAD_DIM
SLC_BLOCK = 64
SLC_TOPN = 16
WINDOW = 512
N_BUCKETS = 32
REL_MAX_DISTANCE = 128
NORM_EPS = 1e-6
NEG = -1e30
BIG = 1e30
TINY = 1e-30
LOG2E = math.log2(math.e)
MOBA_FAR_STATES = 8
NSA_FAR_GROUP = 8
NSA_SEL_STATES = NSA_FAR_GROUP + 2
SB_UNDERFLOW_LOG2 = 160.0

LANES = 128
SUBLANES = 8
COL_BLOCK = 256
SCORE_LOOKAHEAD = 4
N_FORCED = 3
CMP_ROW_STEP = 128
ONES_ROWS = 16
NSA_V_ROWS = HEAD_DIM + ONES_ROWS
PAIR = 2 * HEAD_DIM
VMEM_LIMIT = 56 * 1024 * 1024

F32 = jnp.float32
BF16 = jnp.bfloat16

K_SB, K_MB, K_NS_SLC, K_NS_WIN = 0, 2, 4, 6
N_K_BLOCKS = 8
F_SB_Q, F_MB_Q, F_NS_Q, F_SB_V, F_MB_V, F_NS_SLC_V, F_NS_WIN_V = 0, 2, 4, 8, 10, 12, 14
N_F_BLOCKS = 16
K_MODES = (0, 0, 2, 2, 2, 2, 2, 2)
F_MODES = (1, 1, 2, 2, 2, 2, 2, 2, 0, 0, 0, 0, 3, 3, 3, 3)


SLC_SHIFT = SLC_BLOCK.bit_length() - 1
MOBA_SHIFT = MOBA_BLOCK.bit_length() - 1
assert 1 << SLC_SHIFT == SLC_BLOCK and 1 << MOBA_SHIFT == MOBA_BLOCK


def _ceil_to(n, m):
    return -(-n // m) * m


def _dot(a, b):
    return jnp.dot(a, b, preferred_element_type=F32)


def _dot_hilo_rhs(a, b):
    hi = b.astype(BF16)
    lo = (b - hi.astype(F32)).astype(BF16)
    return _dot(a, hi) + _dot(a, lo)


def _rms(x, g):
    ms = jnp.mean(x * x, axis=-1, keepdims=True)
    return x * lax.rsqrt(ms + NORM_EPS) * g


def _stack_heads_t(q_t):
    tq = q_t.shape[1]
    lo = lax.broadcasted_iota(jnp.int32, (LANES, tq), 0) < HEAD_DIM
    zero = jnp.zeros((LANES, tq), q_t.dtype)
    parts = []
    for c in range(q_t.shape[0] // LANES):
        blk = q_t[c * LANES:(c + 1) * LANES, :]
        parts.append(jnp.where(lo, blk, zero))
        parts.append(jnp.where(lo, zero, blk))
    return jnp.concatenate(parts, axis=1)


def _t5_bucket(dist):
    n = jnp.maximum(dist, 0)
    max_exact = N_BUCKETS // 2
    nf = jnp.maximum(n, 1).astype(F32)
    large = max_exact + (jnp.log(nf / max_exact) / math.log(REL_MAX_DISTANCE / max_exact)
                         * (N_BUCKETS - max_exact)).astype(jnp.int32)
    large = jnp.minimum(large, N_BUCKETS - 1)
    return jnp.where(n < max_exact, n, large)


def _bias_tile_t(tbl_ref, n_heads, dist):
    nk, tq = dist.shape
    bucket = _t5_bucket(dist)
    cols = []
    for h in range(n_heads):
        t = jnp.broadcast_to(tbl_ref[h:h + 1, :], (nk, LANES))
        far = t[:, N_BUCKETS - 1:N_BUCKETS]
        for c in range(tq // LANES):
            cols.append((jnp.take_along_axis(t, bucket[:, c * LANES:(c + 1) * LANES], axis=1) - far) * LOG2E)
    return jnp.concatenate(cols, axis=1)


def _top_n_mask_t(score, n, between=None):
    row = lax.broadcasted_iota(jnp.int32, score.shape, 0).astype(F32)
    sel = jnp.zeros(score.shape, F32)
    s = score
    for _ in range(n):
        m = jnp.max(s, axis=0, keepdims=True)
        idx = jnp.min(jnp.where(s == m, row, float(score.shape[0])), axis=0, keepdims=True)
        pick = row == idx
        sel = jnp.where(pick, 1.0, sel)
        s = jnp.where(pick, -jnp.inf, s)
        if between is not None:
            between()
    return sel


def _post(yb, lo, mode, gain):
    if mode == 2:
        sq = yb * yb
        s_lo = jnp.sum(jnp.where(lo, sq, 0.0), axis=-1, keepdims=True)
        s_hi = jnp.sum(jnp.where(lo, 0.0, sq), axis=-1, keepdims=True)
        yb = yb * lax.rsqrt(jnp.where(lo, s_lo, s_hi) * (1.0 / HEAD_DIM) + NORM_EPS)
    if mode in (1, 2):
        yb = yb * gain
    return yb


def _inproj_kernel(x_ref, g_ref, w_ref, wa_ref, gain_ref, k_ref, f_ref, kc_ref, vc_ref, gate_ref):
    tm = x_ref.shape[0]
    h = _rms(x_ref[...], g_ref[...]).astype(BF16)
    lo = lax.broadcasted_iota(jnp.int32, (tm, LANES), 1) < HEAD_DIM
    for c in range((N_K_BLOCKS + N_F_BLOCKS) // 2):
        y = _dot(h, w_ref[:, c * 2 * LANES:(c + 1) * 2 * LANES])
        for s in range(2):
            blk = 2 * c + s
            gain = gain_ref[:, blk * LANES:(blk + 1) * LANES]
            yb = y[:, s * LANES:(s + 1) * LANES]
            if blk < N_K_BLOCKS:
                k_ref[:, blk * LANES:(blk + 1) * LANES] = _post(yb, lo, K_MODES[blk], gain).astype(BF16)
            else:
                fb = blk - N_K_BLOCKS
                yb = jnp.where(lo, yb, 1.0) if F_MODES[fb] == 3 else _post(yb, lo, F_MODES[fb], gain)
                f_ref[fb * LANES:(fb + 1) * LANES, :] = yb.T.astype(BF16)
    aux = _dot(h, wa_ref[...])
    kc_ref[...] = aux[:, 0:LANES]
    vc_ref[...] = aux[:, LANES:2 * LANES]
    for g in range(NSA_KV_GROUPS):
        gate_ref[g * LANES:(g + 1) * LANES, :] = jax.nn.sigmoid(aux[:, (2 + g) * LANES:(3 + g) * LANES]).T


def _inproj(x2, g, w, wa, gain, tm):
    t, d = x2.shape
    nw = w.shape[1]
    return pl.pallas_call(
        _inproj_kernel,
        grid=(t // tm,),
        in_specs=[
            pl.BlockSpec((tm, d), lambda i: (i, 0)),
            pl.BlockSpec((1, d), lambda i: (0, 0)),
            pl.BlockSpec((d, nw), lambda i: (0, 0)),
            pl.BlockSpec((d, 4 * LANES), lambda i: (0, 0)),
            pl.BlockSpec((1, nw), lambda i: (0, 0)),
        ],
        out_specs=[
            pl.BlockSpec((tm, N_K_BLOCKS * LANES), lambda i: (i, 0)),
            pl.BlockSpec((N_F_BLOCKS * LANES, tm), lambda i: (0, i)),
            pl.BlockSpec((tm, LANES), lambda i: (i, 0)),
            pl.BlockSpec((tm, LANES), lambda i: (i, 0)),
            pl.BlockSpec((NSA_KV_GROUPS * LANES, tm), lambda i: (0, i)),
        ],
        out_shape=[
            jax.ShapeDtypeStruct((t, N_K_BLOCKS * LANES), BF16),
            jax.ShapeDtypeStruct((N_F_BLOCKS * LANES, t), BF16),
            jax.ShapeDtypeStruct((t, LANES), F32),
            jax.ShapeDtypeStruct((t, LANES), F32),
            jax.ShapeDtypeStruct((NSA_KV_GROUPS * LANES, t), F32),
        ],
        compiler_params=pltpu.CompilerParams(
            dimension_semantics=("parallel",), vmem_limit_bytes=VMEM_LIMIT),
        name="inproj",
    )(x2, g, w, wa, gain)


def _sb_kernel(q_ref, k_ref, v_ref, o_ref, r_ref, acc_ref, *, tk):
    tq = q_ref.shape[1]
    cols = 2 * tq
    nsub = tq // tk
    qi = pl.program_id(2)
    qs = _stack_heads_t(q_ref[...])
    j_i = lax.broadcasted_iota(jnp.int32, (tk, cols), 0)
    t_i = lax.broadcasted_iota(jnp.int32, (tk, cols), 1) & (tq - 1)
    after = jnp.where((lax.broadcasted_iota(jnp.int32, (tk, 2 * tk), 1) & (tk - 1))
                      > lax.broadcasted_iota(jnp.int32, (tk, 2 * tk), 0), 1.0, 0.0).astype(BF16)
    r_ref[...] = jnp.zeros(r_ref.shape, F32)
    acc_ref[...] = jnp.zeros(acc_ref.shape, F32)

    def block(kb, strict):
        start = pl.multiple_of(kb * tk, tk)
        z = _dot(k_ref[pl.ds(start, tk), :], qs)
        sp = jnp.maximum(z, 0.0) + jnp.log2(1.0 + jnp.exp2(-jnp.abs(z)))
        stay = sp if strict is None else jnp.where(strict, sp, 0.0)
        hi = stay.astype(BF16)
        lo = (stay - hi.astype(F32)).astype(BF16)
        after_sum = _dot(after, jnp.concatenate([hi, lo], axis=0))
        a = jnp.exp2(z - sp - after_sum - r_ref[...])
        if strict is not None:
            a = jnp.where(strict, a, 0.0)
        acc_ref[...] += _dot(v_ref[:, pl.ds(start, tk)], a.astype(BF16))
        r_ref[...] += jnp.sum(stay, axis=0, keepdims=True)

    for u in range(nsub - 1, -1, -1):
        block(qi * nsub + u, u * tk + j_i < t_i)

    def more(carry):
        j, r_min = carry
        return (j < qi * nsub) & (r_min < SB_UNDERFLOW_LOG2)

    def body(carry):
        j, _ = carry
        block(qi * nsub - 1 - j, None)
        return j + 1, jnp.min(r_ref[...])

    lax.while_loop(more, body, (0, jnp.min(r_ref[...])))
    lo = lax.broadcasted_iota(jnp.int32, (LANES, tq), 0) < HEAD_DIM
    acc = acc_ref[...]
    o_ref[...] = jnp.where(lo, acc[:, :tq], acc[:, tq:]).T.astype(o_ref.dtype)


def _sb_attention(kmat, feat, b, s, tq, tk):
    t = b * s
    nq = s // tq
    return pl.pallas_call(
        functools.partial(_sb_kernel, tk=tk),
        grid=(b, SB_HEADS // 2, nq),
        in_specs=[
            pl.BlockSpec((LANES, tq), lambda bi, p, i: (F_SB_Q + p, bi * nq + i)),
            pl.BlockSpec((s, LANES), lambda bi, p, i: (bi, K_SB + p)),
            pl.BlockSpec((LANES, s), lambda bi, p, i: (F_SB_V + p, bi)),
        ],
        out_specs=pl.BlockSpec((tq, LANES), lambda bi, p, i: (bi * nq + i, p)),
        out_shape=jax.ShapeDtypeStruct((t, SB_HEADS * HEAD_DIM), BF16),
        scratch_shapes=[pltpu.VMEM((1, 2 * tq), F32), pltpu.VMEM((LANES, 2 * tq), F32)],
        compiler_params=pltpu.CompilerParams(
            dimension_semantics=("parallel", "parallel", "arbitrary"), vmem_limit_bytes=VMEM_LIMIT),
        name="stickbreak",
    )(feat, kmat, feat)


def _softmax_init(m_ref, acc_ref, n_states):
    m_ref[:n_states] = jnp.full((n_states,) + m_ref.shape[1:], NEG, F32)
    acc_ref[:n_states] = jnp.zeros((n_states,) + acc_ref.shape[1:], F32)


def _softmax_set(m_ref, acc_ref, s, logits, v_t):
    m = jnp.maximum(jnp.max(logits, axis=0, keepdims=True), NEG)
    acc_ref[s] = _dot(v_t, jnp.exp2(logits - m).astype(BF16))
    m_ref[s] = m


def _softmax_update(m_ref, acc_ref, s, logits, v_t):
    m_old = m_ref[s]
    m_new = jnp.maximum(m_old, jnp.max(logits, axis=0, keepdims=True))
    p = jnp.exp2(logits - m_new)
    acc_ref[s] = jnp.exp2(m_old - m_new) * acc_ref[s] + _dot(v_t, p.astype(BF16))
    m_ref[s] = m_new


def _softmax_merge(m_ref, acc_ref, states):
    m = m_ref[states[0]]
    for s in states[1:]:
        m = jnp.maximum(m, m_ref[s])
    acc = None
    for s in states:
        w = jnp.exp2(m_ref[s] - m)
        acc = w * acc_ref[s] if acc is None else acc + w * acc_ref[s]
    return acc


def _hide(live):
    return jnp.where(live, 0.0, -jnp.inf)


def _grouped_loop(n, group, body, tail):
    shift = group.bit_length() - 1
    assert 1 << shift == group
    full = n >> shift
    lax.fori_loop(0, full, body, 0)
    first = full << shift
    count = group >> 1
    while count:
        pl.when((n & count) != 0)(functools.partial(tail, first, count))
        first = first + (n & count)
        count >>= 1


def _moba_kernel(q_ref, k_ref, v_ref, tbl_ref, o_ref, km_ref, bias_ref, sel_ref, m_ref, acc_ref, *, nblk):
    tk = MOBA_BLOCK
    tq = q_ref.shape[1]
    nsub = tq // tk
    cols = 2 * tq
    qi = pl.program_id(2)

    @pl.when(qi == 0)
    def _():
        km_ref[...] = jnp.zeros(km_ref.shape, F32)

        def mean_body(n, carry):
            kb = k_ref[pl.ds(pl.multiple_of(n * tk, tk), tk), :].astype(F32)
            km_ref[pl.ds(n, 1), :] = jnp.mean(kb, axis=0, keepdims=True)
            return carry

        lax.fori_loop(0, nblk, mean_body, 0)
        j = lax.broadcasted_iota(jnp.int32, (tk, tq), 0)
        i = lax.broadcasted_iota(jnp.int32, (tk, tq), 1)
        for e in range(-1, nsub):
            bias_ref[e + 1] = _bias_tile_t(tbl_ref, 2, i - tk * e - j)

    qs = _stack_heads_t(q_ref[...])
    sub_s = lax.broadcasted_iota(jnp.int32, (LANES, cols), 1) & (tq - 1)
    past = lax.broadcasted_iota(jnp.int32, (LANES, cols), 0) < nsub * qi + (sub_s >> MOBA_SHIFT)
    score = jnp.where(past, _dot(km_ref[...].astype(BF16), qs), NEG)
    sel_ref[...] = jnp.where(past & (_top_n_mask_t(score, MOBA_TOPK) > 0.5), 0.0, -jnp.inf)
    _softmax_init(m_ref, acc_ref, MOBA_FAR_STATES)
    ones = jnp.ones((ONES_ROWS, tk), BF16)

    def tiles(*specs, fresh=False):
        units = []
        for state, n, fn in specs:
            start = pl.multiple_of(n * tk, tk)
            units += [(state, start, fn, slice(c, c + COL_BLOCK)) for c in range(0, cols, COL_BLOCK)]

        def score(unit):
            _, start, fn, cs = unit
            return fn(_dot(k_ref[pl.ds(start, tk), :], qs[:, cs]), cs)

        def fold(unit, logits):
            state, start, _, cs = unit
            head = cs.start // tq
            v_t = jnp.concatenate([v_ref[head * HEAD_DIM:(head + 1) * HEAD_DIM, pl.ds(start, tk)], ones], axis=0)
            (_softmax_set if fresh else _softmax_update)(m_ref, acc_ref, (state, slice(None), cs), logits, v_t)

        pending = []
        for unit in units:
            pending.append((unit, score(unit)))
            if len(pending) > SCORE_LOOKAHEAD:
                fold(*pending.pop(0))
        for item in pending:
            fold(*item)

    def hidden(n, cs, live=None):
        row = sel_ref[pl.ds(n, 1), cs]
        if live is not None:
            row = row + _hide(live)
        return jnp.broadcast_to(row, (tk, COL_BLOCK))

    n_far = jnp.maximum(nsub * qi - 1, 0)

    group = MOBA_FAR_STATES

    def far_tiles(first, count):
        tiles(*[(u, first + u, lambda z, cs, n=first + u: z + hidden(n, cs)) for u in range(count)])

    def far_body(i, carry):
        far_tiles(group * i, group)
        return carry

    _grouped_loop(n_far, group, far_body, far_tiles)
    prev = jnp.maximum(nsub * qi - 1, 0)
    j_i = lax.broadcasted_iota(jnp.int32, (tk, cols), 0)
    t_i = lax.broadcasted_iota(jnp.int32, (tk, cols), 1) & (tq - 1)

    def own_tile(e):
        causal_own = ((t_i >> MOBA_SHIFT) == e) & (j_i <= t_i - tk * e)
        return (MOBA_FAR_STATES + 1 + e, nsub * qi + e,
                lambda z, cs: z + bias_ref[e + 1, :, cs] + jnp.where(causal_own[:, cs], 0.0, hidden(nsub * qi + e, cs)))

    tiles((MOBA_FAR_STATES, prev, lambda z, cs: z + bias_ref[0, :, cs] + hidden(prev, cs, live=qi >= 1)),
          *[own_tile(e) for e in range(nsub)], fresh=True)
    acc = _softmax_merge(m_ref, acc_ref, list(range(MOBA_FAR_STATES + 1 + nsub)))
    out = acc[:HEAD_DIM] * (1.0 / jnp.maximum(acc[HEAD_DIM:HEAD_DIM + 1], TINY))
    o_ref[...] = jnp.concatenate([out[:, :tq], out[:, tq:]], axis=0).T.astype(o_ref.dtype)


def _moba_attention(kmat, feat, tbl, b, s, tq):
    t = b * s
    tk = MOBA_BLOCK
    nq = s // tq
    cols = 2 * tq
    n_states = MOBA_FAR_STATES + 1 + tq // tk
    return pl.pallas_call(
        functools.partial(_moba_kernel, nblk=s // tk),
        grid=(b, MOBA_HEADS // 2, nq),
        in_specs=[
            pl.BlockSpec((LANES, tq), lambda bi, p, i: (F_MB_Q + p, bi * nq + i)),
            pl.BlockSpec((s, LANES), lambda bi, p, i: (bi, K_MB + p)),
            pl.BlockSpec((LANES, s), lambda bi, p, i: (F_MB_V + p, bi)),
            pl.BlockSpec((None, SUBLANES, LANES), lambda bi, p, i: (p, 0, 0)),
        ],
        out_specs=pl.BlockSpec((tq, LANES), lambda bi, p, i: (bi * nq + i, p)),
        out_shape=jax.ShapeDtypeStruct((t, MOBA_HEADS * HEAD_DIM), BF16),
        scratch_shapes=[
            pltpu.VMEM((LANES, LANES), F32),
            pltpu.VMEM((tq // tk + 1, tk, cols), F32),
            pltpu.VMEM((LANES, cols), F32),
            pltpu.VMEM((n_states, 1, cols), F32),
            pltpu.VMEM((n_states, HEAD_DIM + ONES_ROWS, cols), F32),
        ],
        compiler_params=pltpu.CompilerParams(
            dimension_semantics=("arbitrary", "arbitrary", "arbitrary"), vmem_limit_bytes=VMEM_LIMIT),
        name="moba",
    )(feat, kmat, feat, tbl)


def _compress_kernel(rk_ref, rv_ref, wk_ref, wv_ref, pos_ref, w2_ref, kn_ref, kc_ref, vc_ref):
    n = rk_ref.shape[0]
    hid = CMP_HIDDEN
    lo = lax.broadcasted_iota(jnp.int32, (n, LANES), 1) < HEAD_DIM
    for idx, (r_ref, w_ref) in enumerate(((rk_ref, wk_ref), (rv_ref, wv_ref))):
        r = r_ref[...]
        first = _dot((r + pos_ref[2 * idx:2 * idx + 1, :]).astype(BF16), w_ref[:, :2 * hid])
        second = _dot((r + pos_ref[2 * idx + 1:2 * idx + 2, :]).astype(BF16), w_ref[:, 2 * hid:])
        h = jax.nn.gelu(first + pltpu.roll(second, shift=n - 1, axis=0))
        for g in range(NSA_KV_GROUPS):
            y = _dot(h[:, g * hid:(g + 1) * hid].astype(BF16), w2_ref[idx])
            if idx == 0:
                kc_ref[g] = _rms(y, kn_ref[...]).astype(kc_ref.dtype)
            else:
                vc_ref[g] = jnp.where(lo, y, 1.0).T.astype(vc_ref.dtype)


def _compress(rk, rv, wk, wv, pos, w2, kn):
    b, n, width = rk.shape
    return pl.pallas_call(
        _compress_kernel,
        grid=(b,),
        in_specs=[
            pl.BlockSpec((None, n, width), lambda i: (i, 0, 0)),
            pl.BlockSpec((None, n, width), lambda i: (i, 0, 0)),
            pl.BlockSpec(wk.shape, lambda i: (0, 0)),
            pl.BlockSpec(wv.shape, lambda i: (0, 0)),
            pl.BlockSpec(pos.shape, lambda i: (0, 0)),
            pl.BlockSpec(w2.shape, lambda i: (0, 0, 0)),
            pl.BlockSpec(kn.shape, lambda i: (0, 0)),
        ],
        out_specs=[
            pl.BlockSpec((None, NSA_KV_GROUPS, n, LANES), lambda i: (i, 0, 0, 0)),
            pl.BlockSpec((None, NSA_KV_GROUPS, LANES, n), lambda i: (i, 0, 0, 0)),
        ],
        out_shape=[
            jax.ShapeDtypeStruct((b, NSA_KV_GROUPS, n, LANES), BF16),
            jax.ShapeDtypeStruct((b, NSA_KV_GROUPS, LANES, n), BF16),
        ],
        compiler_params=pltpu.CompilerParams(
            dimension_semantics=("parallel",), vmem_limit_bytes=VMEM_LIMIT),
        name="nsa_compress",
    )(rk, rv, wk, wv, pos, w2, kn)


def _nsa_kernel(q_ref, kc_ref, vc_ref, ks_ref, vs_ref, kw_ref, vw_ref, gate_ref, tbl_ref, o_ref,
                bdiag_ref, bnear_ref, ov_ref, sel_ref, m_ref, acc_ref, tot_ref, lc_ref, imp_ref):
    tq = q_ref.shape[1]
    nh = NSA_HPG
    cols = nh * tq
    ncmp = kc_ref.shape[0]
    qi = pl.program_id(2)
    q0 = qi * tq

    @pl.when(qi == 0)
    def _():
        j = lax.broadcasted_iota(jnp.int32, (tq, tq), 0)
        i = lax.broadcasted_iota(jnp.int32, (tq, tq), 1)
        bdiag_ref[...] = _bias_tile_t(tbl_ref, nh, i - j)
        bnear_ref[...] = _bias_tile_t(tbl_ref, nh, tq + i - j)
        ss = lax.broadcasted_iota(jnp.int32, (LANES, ncmp), 0) * SLC_BLOCK
        cs = lax.broadcasted_iota(jnp.int32, (LANES, ncmp), 1) * CMP_STRIDE
        ov_ref[...] = jnp.where((cs < ss + SLC_BLOCK) & (cs + CMP_BLOCK > ss), 1.0, 0.0).astype(BF16)

    qs = _stack_heads_t(q_ref[...])

    def gate(branch):
        return jnp.concatenate([gate_ref[branch * nh + h:branch * nh + h + 1, :] for h in range(nh)], axis=1)

    def result(states):
        acc = _softmax_merge(m_ref, acc_ref, states)
        return acc[:HEAD_DIM] * (1.0 / jnp.maximum(acc[HEAD_DIM:HEAD_DIM + 1], TINY))

    j_i = lax.broadcasted_iota(jnp.int32, (tq, cols), 0)
    i_i = lax.broadcasted_iota(jnp.int32, (tq, cols), 1) & (tq - 1)
    causal = j_i <= i_i
    per_tile = tq // SLC_BLOCK
    nwin = WINDOW // tq
    n_sel_states = NSA_SEL_STATES
    group = NSA_FAR_GROUP
    _softmax_init(m_ref, acc_ref, group)

    def tile_steps(*specs, fresh=False):
        units = []
        for state, k_ref, v_ref, kb, fn in specs:
            start = pl.multiple_of(kb * tq, tq)
            units += [(state, k_ref, v_ref, start, fn, slice(c, c + COL_BLOCK)) for c in range(0, cols, COL_BLOCK)]

        def score(unit):
            _, k_ref, _, start, fn, cs = unit
            return fn(_dot(k_ref[pl.ds(start, tq), :], qs[:, cs]), cs)

        def fold(unit, logits):
            state, _, v_ref, start, _, cs = unit
            (_softmax_set if fresh else _softmax_update)(
                m_ref, acc_ref, (state, slice(None), cs), logits, v_ref[:NSA_V_ROWS, pl.ds(start, tq)])

        pending = []
        for unit in units:
            pending.append((unit, score(unit)))
            if len(pending) > SCORE_LOOKAHEAD:
                fold(*pending.pop(0))
                yield
        for item in pending:
            fold(*item)
            yield

    def tiles(*specs, fresh=False):
        for _ in tile_steps(*specs, fresh=fresh):
            pass

    def window_tile(r):
        def fn(z, cs):
            if r == 1:
                z = z + bnear_ref[:, cs]
            if r == nwin:
                z = jnp.where((j_i > i_i)[:, cs], z, -jnp.inf)
            return z + _hide(qi >= r)
        return (n_sel_states + r, kw_ref, vw_ref, jnp.maximum(qi - r, 0), fn)

    window_steps = tile_steps(
        *[window_tile(r) for r in range(nwin, 0, -1)],
        (n_sel_states, kw_ref, vw_ref, qi, lambda z, cs: jnp.where(causal[:, cs], z + bdiag_ref[:, cs], -jnp.inf)),
        fresh=True)

    per_tile_c = tq // CMP_STRIDE
    below = _ceil_to(-(-(REL_MAX_DISTANCE + CMP_BLOCK) // CMP_STRIDE), SUBLANES)
    band = below + per_tile_c
    band_start = pl.multiple_of(jnp.maximum(qi * per_tile_c - below, 0), SUBLANES)
    n_b = band_start + lax.broadcasted_iota(jnp.int32, (band, tq), 0)
    dist_b = q0 + lax.broadcasted_iota(jnp.int32, (band, tq), 1) - (n_b * CMP_STRIDE + CMP_BLOCK - 1)
    band_bias = _bias_tile_t(tbl_ref, nh, dist_b)

    def compressed(rows):
        n_i = lax.broadcasted_iota(jnp.int32, (rows, cols), 0)
        t_i = lax.broadcasted_iota(jnp.int32, (rows, cols), 1) & (tq - 1)
        valid_c = q0 + t_i - (n_i * CMP_STRIDE + CMP_BLOCK - 1) >= 0
        lc_ref[:rows] = _dot(kc_ref[:rows, :], qs)
        lc_ref[pl.ds(band_start, band), :] += band_bias
        logit_c = jnp.where(valid_c, lc_ref[:rows], -jnp.inf)
        p_c = jnp.exp2(logit_c - jnp.maximum(jnp.max(logit_c, axis=0, keepdims=True), NEG))
        p_c = p_c / jnp.maximum(jnp.sum(p_c, axis=0, keepdims=True), TINY)
        tot_ref[...] = gate(0) * _dot(vc_ref[:HEAD_DIM, :rows], p_c.astype(BF16))
        p_sum = p_c[:, 0:tq]
        for h in range(1, nh):
            p_sum = p_sum + p_c[:, h * tq:(h + 1) * tq]
        imp_ref[...] = _dot_hilo_rhs(ov_ref[:, :rows], p_sum)

    step = min(CMP_ROW_STEP, ncmp)
    needed = (qi + 1) * per_tile_c
    for rows in range(step, ncmp + 1, step):
        pl.when((needed > rows - step) & (needed <= rows))(functools.partial(compressed, rows))
    imp = imp_ref[...]
    blk = lax.broadcasted_iota(jnp.int32, (LANES, tq), 0)
    cur = (q0 + lax.broadcasted_iota(jnp.int32, (LANES, tq), 1)) >> SLC_SHIFT
    forced = (blk == 0) | (blk == cur) | (blk == cur - 1)
    visible = blk <= cur
    score = jnp.where(visible & ~forced, imp, NEG)
    top = _top_n_mask_t(score, SLC_TOPN - N_FORCED, between=lambda: next(window_steps, None))
    for _ in window_steps:
        pass
    sel = jnp.where(visible & (forced | (top > 0.5)), 0.0, -jnp.inf)
    sel_ref[...] = jnp.concatenate([sel] * nh, axis=1)

    def hidden(kb, cs, live=None):
        rows = [sel_ref[pl.ds(kb * per_tile + r, 1), cs] for r in range(per_tile)]
        if live is not None:
            rows = [row + _hide(live) for row in rows]
        return jnp.concatenate([jnp.broadcast_to(row, (SLC_BLOCK, COL_BLOCK)) for row in rows], axis=0)

    n_far = jnp.maximum(qi - 1, 0)

    def far_tiles(first, count):
        tiles(*[(u, ks_ref, vs_ref, first + u, lambda z, cs, kb=first + u: z + hidden(kb, cs)) for u in range(count)])

    def sel_body(i, carry):
        far_tiles(group * i, group)
        return carry

    _grouped_loop(n_far, group, sel_body, far_tiles)
    near = jnp.maximum(qi - 1, 0)
    tiles((group, ks_ref, vs_ref, near, lambda z, cs: z + bnear_ref[:, cs] + hidden(near, cs, live=qi >= 1)),
          (group + 1, ks_ref, vs_ref, qi,
           lambda z, cs: jnp.where(causal[:, cs], z + bdiag_ref[:, cs] + hidden(qi, cs), -jnp.inf)),
          fresh=True)
    tot = (tot_ref[...] + gate(1) * result(list(range(n_sel_states)))
           + gate(2) * result(list(range(n_sel_states, n_sel_states + nwin + 1))))
    pairs = [jnp.concatenate([tot[:, (2 * c) * tq:(2 * c + 1) * tq], tot[:, (2 * c + 1) * tq:(2 * c + 2) * tq]], axis=0).T
             for c in range(nh // 2)]
    o_ref[...] = jnp.concatenate(pairs, axis=1).astype(o_ref.dtype)


def _nsa_attention(kmat, feat, kc, vc, gates_t, tbl, b, s, tq):
    t = b * s
    nq = s // tq
    ncmp = kc.shape[2]
    assert ncmp % min(CMP_ROW_STEP, ncmp) == 0
    cols = NSA_HPG * tq
    n_states = NSA_SEL_STATES + WINDOW // tq + 1
    k_spec = lambda blk: pl.BlockSpec((s, LANES), lambda bi, g, i: (bi, blk + g))
    v_spec = lambda blk: pl.BlockSpec((LANES, s), lambda bi, g, i: (blk + g, bi))
    return pl.pallas_call(
        _nsa_kernel,
        grid=(b, NSA_KV_GROUPS, nq),
        in_specs=[
            pl.BlockSpec((2 * LANES, tq), lambda bi, g, i: (F_NS_Q // 2 + g, bi * nq + i)),
            pl.BlockSpec((None, None, ncmp, LANES), lambda bi, g, i: (bi, g, 0, 0)),
            pl.BlockSpec((None, None, LANES, ncmp), lambda bi, g, i: (bi, g, 0, 0)),
            k_spec(K_NS_SLC), v_spec(F_NS_SLC_V), k_spec(K_NS_WIN), v_spec(F_NS_WIN_V),
            pl.BlockSpec((LANES, tq), lambda bi, g, i: (g, bi * nq + i)),
            pl.BlockSpec((None, SUBLANES, LANES), lambda bi, g, i: (g, 0, 0)),
        ],
        out_specs=pl.BlockSpec((tq, 2 * LANES), lambda bi, g, i: (bi * nq + i, g)),
        out_shape=jax.ShapeDtypeStruct((t, NSA_HEADS * HEAD_DIM), BF16),
        scratch_shapes=[
            pltpu.VMEM((tq, cols), F32),
            pltpu.VMEM((tq, cols), F32),
            pltpu.VMEM((LANES, ncmp), BF16),
            pltpu.VMEM((LANES, cols), F32),
            pltpu.VMEM((n_states, 1, cols), F32),
            pltpu.VMEM((n_states, NSA_V_ROWS, cols), F32),
            pltpu.VMEM((HEAD_DIM, cols), F32),
            pltpu.VMEM((ncmp, cols), F32),
            pltpu.VMEM((LANES, tq), F32),
        ],
        compiler_params=pltpu.CompilerParams(
            dimension_semantics=("arbitrary", "arbitrary", "arbitrary"), vmem_limit_bytes=VMEM_LIMIT),
        name="nsa",
    )(feat, kc, vc, kmat, feat, kmat, feat, gates_t, tbl)


def _merge_kernel(x_ref, g_ref, oa_ref, ob_ref, oc_ref, wg_ref, wb_ref, wo_ref, o_ref):
    x = x_ref[...]
    d = x.shape[1]
    h = _rms(x, g_ref[...]).astype(BF16)
    mix = None
    row = 0
    for br, src in enumerate((oa_ref, ob_ref, oc_ref)):
        width = src.shape[1]
        gate = jax.nn.sigmoid(_dot(h, wg_ref[:, br * d:(br + 1) * d]))
        term = gate * _dot(src[...], wb_ref[row:row + width, :])
        mix = term if mix is None else mix + term
        row += width
    o_ref[...] = x + _dot(mix.astype(BF16), wo_ref[...])


def _merge(x2, g, oa, ob, oc, wg, wb, wo, tm):
    t, d = x2.shape
    row = lambda w: pl.BlockSpec((tm, w), lambda i: (i, 0))
    full = lambda a: pl.BlockSpec(a.shape, lambda i: (0, 0))
    return pl.pallas_call(
        _merge_kernel,
        grid=(t // tm,),
        in_specs=[row(d), full(g), row(oa.shape[1]), row(ob.shape[1]), row(oc.shape[1]),
                  full(wg), full(wb), full(wo)],
        out_specs=row(d),
        out_shape=jax.ShapeDtypeStruct((t, d), F32),
        compiler_params=pltpu.CompilerParams(
            dimension_semantics=("parallel",), vmem_limit_bytes=VMEM_LIMIT),
        name="merge",
    )(x2, g, oa, ob, oc, wg, wb, wo)


def _ffn_kernel(x_ref, g_ref, wgu_ref, wd_ref, o_ref, *, n_chunks):
    x = x_ref[...]
    d_ff = wd_ref.shape[0]
    ch = d_ff // n_chunks
    h = _rms(x, g_ref[...]).astype(BF16)
    out = x
    for c in range(n_chunks):
        gate = _dot(h, wgu_ref[:, c * ch:(c + 1) * ch])
        up = _dot(h, wgu_ref[:, d_ff + c * ch:d_ff + (c + 1) * ch])
        out = out + _dot((jax.nn.silu(gate) * up).astype(BF16), wd_ref[c * ch:(c + 1) * ch, :])
    o_ref[...] = out


def _ffn(x2, g, wgu, wd, tm):
    t, d = x2.shape
    d_ff = wd.shape[0]
    n_chunks = 2 if d_ff % (2 * LANES) == 0 else 1
    return pl.pallas_call(
        functools.partial(_ffn_kernel, n_chunks=n_chunks),
        grid=(t // tm,),
        in_specs=[
            pl.BlockSpec((tm, d), lambda i: (i, 0)),
            pl.BlockSpec((1, d), lambda i: (0, 0)),
            pl.BlockSpec(wgu.shape, lambda i: (0, 0)),
            pl.BlockSpec(wd.shape, lambda i: (0, 0)),
        ],
        out_specs=pl.BlockSpec((tm, d), lambda i: (i, 0)),
        out_shape=jax.ShapeDtypeStruct((t, d), F32),
        compiler_params=pltpu.CompilerParams(
            dimension_semantics=("parallel",), vmem_limit_bytes=VMEM_LIMIT),
        name="swiglu",
    )(x2, g, wgu, wd)


def _dup_groups(w):
    d = w.shape[0]
    w = w.reshape(d, NSA_KV_GROUPS, 1, HEAD_DIM)
    return jnp.broadcast_to(w, (d, NSA_KV_GROUPS, 2, HEAD_DIM)).reshape(d, NSA_KV_GROUPS * PAIR)


def _half_groups(w):
    d = w.shape[0]
    w = w.reshape(d, NSA_KV_GROUPS, HEAD_DIM)
    return jnp.pad(w, ((0, 0), (0, 0), (0, HEAD_DIM))).reshape(d, NSA_KV_GROUPS * PAIR)


def _pair_gain(g, scale=1.0):
    return jnp.concatenate([g, g]) * scale


def _layer_params(w_in, moba_q_norm, moba_k_norm, nsa_q_norm, nsa_k_norm, nsa_cmp_pos, nsa_cmp_w1, nsa_cmp_w2):
    d = w_in.shape[0]
    sbw, mbw, nsw, kvw = SB_HEADS * HEAD_DIM, MOBA_HEADS * HEAD_DIM, NSA_HEADS * HEAD_DIM, NSA_KV_GROUPS * HEAD_DIM
    sb_q, sb_k, sb_v = (w_in[:, i * sbw:(i + 1) * sbw] for i in range(3))
    o = 3 * sbw
    mb_q, mb_k, mb_v = (w_in[:, o + i * mbw:o + (i + 1) * mbw] for i in range(3))
    o += 3 * mbw
    ns_q = w_in[:, o:o + nsw]
    o += nsw
    kc_w, vc_w, ks_w, vs_w, kw_w, vw_w = (w_in[:, o + i * kvw:o + (i + 1) * kvw] for i in range(6))
    o += 6 * kvw
    gate_w = w_in[:, o:o + N_BRANCHES * NSA_HEADS].reshape(d, N_BRANCHES, NSA_KV_GROUPS, NSA_HPG)
    o += N_BRANCHES * NSA_HEADS
    wg = w_in[:, o:].astype(BF16)
    w = jnp.concatenate([sb_k, mb_k, _dup_groups(ks_w), _dup_groups(kw_w),
                         sb_q, mb_q, ns_q, sb_v, mb_v, _half_groups(vs_w), _half_groups(vw_w)], axis=1).astype(BF16)
    gate_cols = []
    for g in range(NSA_KV_GROUPS):
        cols = gate_w[:, :, g, :].reshape(d, N_BRANCHES * NSA_HPG)
        gate_cols.append(jnp.pad(cols, ((0, 0), (0, LANES - N_BRANCHES * NSA_HPG))))
    wa = jnp.concatenate([kc_w, vc_w] + gate_cols, axis=1).astype(BF16)
    scale = HEAD_DIM ** -0.5
    ones = jnp.ones((LANES,), F32)
    gains = [ones] * 2 + [_pair_gain(moba_k_norm)] * 2 + [_pair_gain(nsa_k_norm[1])] * 2 + [_pair_gain(nsa_k_norm[2])] * 2
    gains += [ones * (scale * LOG2E)] * 2 + [_pair_gain(moba_q_norm, scale * LOG2E)] * 2
    gains += [_pair_gain(nsa_q_norm, scale * LOG2E)] * 4 + [ones] * 8
    gain = jnp.concatenate(gains).reshape(1, (N_K_BLOCKS + N_F_BLOCKS) * LANES)

    half = CMP_BLOCK // 2
    hid = CMP_HIDDEN

    def spread(w1):
        w1 = w1.reshape(2, half, HEAD_DIM, hid)
        cols = []
        for part in range(2):
            for g in range(NSA_KV_GROUPS):
                z = jnp.zeros((half, NSA_KV_GROUPS, HEAD_DIM, hid), F32).at[:, g].set(w1[part])
                cols.append(z.reshape(half * kvw, hid))
        return jnp.concatenate(cols, axis=1).astype(BF16)

    def spread_pos(p):
        p = p.reshape(2, half, 1, HEAD_DIM)
        return jnp.broadcast_to(p, (2, half, NSA_KV_GROUPS, HEAD_DIM)).reshape(2, half * kvw)

    pos = jnp.concatenate([spread_pos(nsa_cmp_pos[0]), spread_pos(nsa_cmp_pos[1])], axis=0)
    w2 = jnp.stack([jnp.concatenate([nsa_cmp_w2[0], nsa_cmp_w2[0]], axis=-1),
                    jnp.pad(nsa_cmp_w2[1], ((0, 0), (0, HEAD_DIM)))]).astype(BF16)
    kn = _pair_gain(nsa_k_norm[0]).reshape(1, LANES)
    return dict(w=w, wa=wa, gain=gain, wg=wg, wk=spread(nsa_cmp_w1[0]), wv=spread(nsa_cmp_w1[1]),
                pos=pos, w2=w2, kn=kn)


def _bias_tables(rel_bias):
    tbl = jnp.pad(rel_bias.T, ((0, 0), (0, LANES - N_BUCKETS)))

    def rows(t, n):
        t = t.reshape(n, -1, LANES)
        return jnp.pad(t, ((0, 0), (0, SUBLANES - t.shape[1]), (0, 0)))

    return rows(tbl[:MOBA_HEADS], MOBA_HEADS // 2), rows(tbl[MOBA_HEADS:], NSA_KV_GROUPS)


def kernel(x, rel_bias, attn_norm, w_in, moba_q_norm, moba_k_norm, nsa_q_norm, nsa_k_norm, nsa_cmp_pos,
           nsa_cmp_w1, nsa_cmp_w2, w_branch, w_out, ffn_norm, w_gate_up, w_down):
    b, s, d = x.shape
    t = b * s
    tm = min(512, t)
    tq_moba = 512
    tq_nsa = 256
    tq_sb, tk_sb = 512, 256
    assert s % tq_moba == 0 and tq_moba % MOBA_BLOCK == 0 and s % tq_sb == 0 and WINDOW % tq_nsa == 0
    assert SLC_TOPN <= s // SLC_BLOCK <= LANES
    tbl_moba, tbl_nsa = _bias_tables(rel_bias)
    x2 = x.reshape(t, d)
    for layer in range(w_in.shape[0]):
        p = _layer_params(w_in[layer], moba_q_norm[layer], moba_k_norm[layer], nsa_q_norm[layer],
                          nsa_k_norm[layer], nsa_cmp_pos[layer], nsa_cmp_w1[layer], nsa_cmp_w2[layer])
        g_attn = attn_norm[layer].reshape(1, d)
        kmat, feat, kc_in, vc_in, gates_t = _inproj(x2, g_attn, p["w"], p["wa"], p["gain"], tm)
        o_a = _sb_attention(kmat, feat, b, s, tq_sb, tk_sb)
        o_b = _moba_attention(kmat, feat, tbl_moba, b, s, tq_moba)
        chunk = CMP_STRIDE * LANES
        kc, vc = _compress(kc_in.reshape(b, s // CMP_STRIDE, chunk), vc_in.reshape(b, s // CMP_STRIDE, chunk),
                           p["wk"], p["wv"], p["pos"], p["w2"], p["kn"])
        o_c = _nsa_attention(kmat, feat, kc, vc, gates_t, tbl_nsa, b, s, tq_nsa)
        x2 = _merge(x2, g_attn, o_a, o_b, o_c, p["wg"], w_branch[layer].astype(BF16),
                    w_out[layer].astype(BF16), tm)
        x2 = _ffn(x2, ffn_norm[layer].reshape(1, d), w_gate_up[layer].astype(BF16),
                  w_down[layer].astype(BF16), tm)
    return x2.reshape(b, s, d)
```

```python
import functools
import math

import jax
import jax.numpy as jnp
from jax import lax
from jax.experimental import pallas as pl
from jax.experimental.pallas import tpu as pltpu

HEAD_DIM = 64
SB_HEADS = 4
MOBA_HEADS = 4
NSA_HEADS = 8
NSA_KV_GROUPS = 2
NSA_HPG = NSA_HEADS // NSA_KV_GROUPS
N_BRANCHES = 3
MOBA_BLOCK = 256
MOBA_TOPK = 3
CMP_BLOCK = 32
CMP_STRIDE = 16
CMP_HIDDEN = 4 * HEAD_DIM
SLC_BLOCK = 64
SLC_TOPN = 16
WINDOW = 512
N_BUCKETS = 32
REL_MAX_DISTANCE = 128
NORM_EPS = 1e-6
NEG = -1e30
BIG = 1e30
TINY = 1e-30
LOG2E = math.log2(math.e)
MOBA_FAR_STATES = 16
NSA_FAR_GROUP = 16
NSA_SEL_STATES = NSA_FAR_GROUP + 2
SB_UNDERFLOW_LOG2 = 160.0

LANES = 128
SUBLANES = 8
COL_BLOCK = 256
SCORE_LOOKAHEAD = 4
N_FORCED = 3
CMP_ROW_STEP = 128
ONES_ROWS = 16
NSA_V_ROWS = HEAD_DIM + ONES_ROWS
PAIR = 2 * HEAD_DIM
VMEM_LIMIT = 56 * 1024 * 1024

F32 = jnp.float32
BF16 = jnp.bfloat16

K_SB, K_MB, K_NS_SLC, K_NS_WIN = 0, 2, 4, 6
N_K_BLOCKS = 8
F_SB_Q, F_MB_Q, F_NS_Q, F_SB_V, F_MB_V, F_NS_SLC_V, F_NS_WIN_V = 0, 2, 4, 8, 10, 12, 14
N_F_BLOCKS = 16
K_MODES = (0, 0, 2, 2, 2, 2, 2, 2)
F_MODES = (1, 1, 2, 2, 2, 2, 2, 2, 0, 0, 0, 0, 3, 3, 3, 3)


SLC_SHIFT = SLC_BLOCK.bit_length() - 1
MOBA_SHIFT = MOBA_BLOCK.bit_length() - 1
assert 1 << SLC_SHIFT == SLC_BLOCK and 1 << MOBA_SHIFT == MOBA_BLOCK


def _ceil_to(n, m):
    return -(-n // m) * m


def _dot(a, b):
    return jnp.dot(a, b, preferred_element_type=F32)


def _dot_hilo_rhs(a, b):
    hi = b.astype(BF16)
    lo = (b - hi.astype(F32)).astype(BF16)
    return _dot(a, hi) + _dot(a, lo)


def _rms(x, g):
    ms = jnp.mean(x * x, axis=-1, keepdims=True)
    return x * lax.rsqrt(ms + NORM_EPS) * g


def _stack_heads_t(q_t):
    tq = q_t.shape[1]
    lo = lax.broadcasted_iota(jnp.int32, (LANES, tq), 0) < HEAD_DIM
    zero = jnp.zeros((LANES, tq), q_t.dtype)
    parts = []
    for c in range(q_t.shape[0] // LANES):
        blk = q_t[c * LANES:(c + 1) * LANES, :]
        parts.append(jnp.where(lo, blk, zero))
        parts.append(jnp.where(lo, zero, blk))
    return jnp.concatenate(parts, axis=1)


def _t5_bucket(dist):
    n = jnp.maximum(dist, 0)
    max_exact = N_BUCKETS // 2
    nf = jnp.maximum(n, 1).astype(F32)
    large = max_exact + (jnp.log(nf / max_exact) / math.log(REL_MAX_DISTANCE / max_exact)
                         * (N_BUCKETS - max_exact)).astype(jnp.int32)
    large = jnp.minimum(large, N_BUCKETS - 1)
    return jnp.where(n < max_exact, n, large)


def _bias_tile_t(tbl_ref, n_heads, dist):
    nk, tq = dist.shape
    bucket = _t5_bucket(dist)
    cols = []
    for h in range(n_heads):
        t = jnp.broadcast_to(tbl_ref[h:h + 1, :], (nk, LANES))
        far = t[:, N_BUCKETS - 1:N_BUCKETS]
        for c in range(tq // LANES):
            cols.append((jnp.take_along_axis(t, bucket[:, c * LANES:(c + 1) * LANES], axis=1) - far) * LOG2E)
    return jnp.concatenate(cols, axis=1)


def _top_n_mask_t(score, n, between=None):
    row = lax.broadcasted_iota(jnp.int32, score.shape, 0).astype(F32)
    sel = jnp.zeros(score.shape, F32)
    s = score
    for _ in range(n):
        m = jnp.max(s, axis=0, keepdims=True)
        idx = jnp.min(jnp.where(s == m, row, float(score.shape[0])), axis=0, keepdims=True)
        pick = row == idx
        sel = jnp.where(pick, 1.0, sel)
        s = jnp.where(pick, -jnp.inf, s)
        if between is not None:
            between()
    return sel


def _post(yb, lo, mode, gain):
    if mode == 2:
        sq = yb * yb
        s_lo = jnp.sum(jnp.where(lo, sq, 0.0), axis=-1, keepdims=True)
        s_hi = jnp.sum(jnp.where(lo, 0.0, sq), axis=-1, keepdims=True)
        yb = yb * lax.rsqrt(jnp.where(lo, s_lo, s_hi) * (1.0 / HEAD_DIM) + NORM_EPS)
    if mode in (1, 2):
        yb = yb * gain
    return yb


def _inproj_kernel(x_ref, g_ref, w_ref, wa_ref, gain_ref, k_ref, f_ref, kc_ref, vc_ref, gate_ref):
    tm = x_ref.shape[0]
    h = _rms(x_ref[...], g_ref[...]).astype(BF16)
    lo = lax.broadcasted_iota(jnp.int32, (tm, LANES), 1) < HEAD_DIM
    for c in range((N_K_BLOCKS + N_F_BLOCKS) // 2):
        y = _dot(h, w_ref[:, c * 2 * LANES:(c + 1) * 2 * LANES])
        for s in range(2):
            blk = 2 * c + s
            gain = gain_ref[:, blk * LANES:(blk + 1) * LANES]
            yb = y[:, s * LANES:(s + 1) * LANES]
            if blk < N_K_BLOCKS:
                k_ref[:, blk * LANES:(blk + 1) * LANES] = _post(yb, lo, K_MODES[blk], gain).astype(BF16)
            else:
                fb = blk - N_K_BLOCKS
                yb = jnp.where(lo, yb, 1.0) if F_MODES[fb] == 3 else _post(yb, lo, F_MODES[fb], gain)
                f_ref[fb * LANES:(fb + 1) * LANES, :] = yb.T.astype(BF16)
    aux = _dot(h, wa_ref[...])
    kc_ref[...] = aux[:, 0:LANES]
    vc_ref[...] = aux[:, LANES:2 * LANES]
    for g in range(NSA_KV_GROUPS):
        gate_ref[g * LANES:(g + 1) * LANES, :] = jax.nn.sigmoid(aux[:, (2 + g) * LANES:(3 + g) * LANES]).T


def _inproj(x2, g, w, wa, gain, tm):
    t, d = x2.shape
    nw = w.shape[1]
    return pl.pallas_call(
        _inproj_kernel,
        grid=(t // tm,),
        in_specs=[
            pl.BlockSpec((tm, d), lambda i: (i, 0)),
            pl.BlockSpec((1, d), lambda i: (0, 0)),
            pl.BlockSpec((d, nw), lambda i: (0, 0)),
            pl.BlockSpec((d, 4 * LANES), lambda i: (0, 0)),
            pl.BlockSpec((1, nw), lambda i: (0, 0)),
        ],
        out_specs=[
            pl.BlockSpec((tm, N_K_BLOCKS * LANES), lambda i: (i, 0)),
            pl.BlockSpec((N_F_BLOCKS * LANES, tm), lambda i: (0, i)),
            pl.BlockSpec((tm, LANES), lambda i: (i, 0)),
            pl.BlockSpec((tm, LANES), lambda i: (i, 0)),
            pl.BlockSpec((NSA_KV_GROUPS * LANES, tm), lambda i: (0, i)),
        ],
        out_shape=[
            jax.ShapeDtypeStruct((t, N_K_BLOCKS * LANES), BF16),
            jax.ShapeDtypeStruct((N_F_BLOCKS * LANES, t), BF16),
            jax.ShapeDtypeStruct((t, LANES), F32),
            jax.ShapeDtypeStruct((t, LANES), F32),
            jax.ShapeDtypeStruct((NSA_KV_GROUPS * LANES, t), F32),
        ],
        compiler_params=pltpu.CompilerParams(
            dimension_semantics=("parallel",), vmem_limit_bytes=VMEM_LIMIT),
        name="inproj",
    )(x2, g, w, wa, gain)


def _sb_kernel(q_ref, k_ref, v_ref, o_ref, r_ref, acc_ref, *, tk):
    tq = q_ref.shape[1]
    cols = 2 * tq
    nsub = tq // tk
    qi = pl.program_id(2)
    qs = _stack_heads_t(q_ref[...])
    j_i = lax.broadcasted_iota(jnp.int32, (tk, cols), 0)
    t_i = lax.broadcasted_iota(jnp.int32, (tk, cols), 1) & (tq - 1)
    after = jnp.where((lax.broadcasted_iota(jnp.int32, (tk, 2 * tk), 1) & (tk - 1))
                      > lax.broadcasted_iota(jnp.int32, (tk, 2 * tk), 0), 1.0, 0.0).astype(BF16)
    r_ref[...] = jnp.zeros(r_ref.shape, F32)
    acc_ref[...] = jnp.zeros(acc_ref.shape, F32)

    def block(kb, strict):
        start = pl.multiple_of(kb * tk, tk)
        z = _dot(k_ref[pl.ds(start, tk), :], qs)
        sp = jnp.maximum(z, 0.0) + jnp.log2(1.0 + jnp.exp2(-jnp.abs(z)))
        stay = sp if strict is None else jnp.where(strict, sp, 0.0)
        hi = stay.astype(BF16)
        lo = (stay - hi.astype(F32)).astype(BF16)
        after_sum = _dot(after, jnp.concatenate([hi, lo], axis=0))
        a = jnp.exp2(z - sp - after_sum - r_ref[...])
        if strict is not None:
            a = jnp.where(strict, a, 0.0)
        acc_ref[...] += _dot(v_ref[:, pl.ds(start, tk)], a.astype(BF16))
        r_ref[...] += jnp.sum(stay, axis=0, keepdims=True)

    for u in range(nsub - 1, -1, -1):
        block(qi * nsub + u, u * tk + j_i < t_i)

    def more(carry):
        j, r_min = carry
        return (j < qi * nsub) & (r_min < SB_UNDERFLOW_LOG2)

    def body(carry):
        j, _ = carry
        block(qi * nsub - 1 - j, None)
        return j + 1, jnp.min(r_ref[...])

    lax.while_loop(more, body, (0, jnp.min(r_ref[...])))
    lo = lax.broadcasted_iota(jnp.int32, (LANES, tq), 0) < HEAD_DIM
    acc = acc_ref[...]
    o_ref[...] = jnp.where(lo, acc[:, :tq], acc[:, tq:]).T.astype(o_ref.dtype)


def _sb_attention(kmat, feat, b, s, tq, tk):
    t = b * s
    nq = s // tq
    return pl.pallas_call(
        functools.partial(_sb_kernel, tk=tk),
        grid=(b, SB_HEADS // 2, nq),
        in_specs=[
            pl.BlockSpec((LANES, tq), lambda bi, p, i: (F_SB_Q + p, bi * nq + i)),
            pl.BlockSpec((s, LANES), lambda bi, p, i: (bi, K_SB + p)),
            pl.BlockSpec((LANES, s), lambda bi, p, i: (F_SB_V + p, bi)),
        ],
        out_specs=pl.BlockSpec((tq, LANES), lambda bi, p, i: (bi * nq + i, p)),
        out_shape=jax.ShapeDtypeStruct((t, SB_HEADS * HEAD_DIM), BF16),
        scratch_shapes=[pltpu.VMEM((1, 2 * tq), F32), pltpu.VMEM((LANES, 2 * tq), F32)],
        compiler_params=pltpu.CompilerParams(
            dimension_semantics=("parallel", "parallel", "arbitrary"), vmem_limit_bytes=VMEM_LIMIT),
        name="stickbreak",
    )(feat, kmat, feat)


def _softmax_init(m_ref, acc_ref, n_states):
    m_ref[:n_states] = jnp.full((n_states,) + m_ref.shape[1:], NEG, F32)
    acc_ref[:n_states] = jnp.zeros((n_states,) + acc_ref.shape[1:], F32)


def _softmax_set(m_ref, acc_ref, s, logits, v_t):
    m = jnp.maximum(jnp.max(logits, axis=0, keepdims=True), NEG)
    acc_ref[s] = _dot(v_t, jnp.exp2(logits - m).astype(BF16))
    m_ref[s] = m


def _softmax_update(m_ref, acc_ref, s, logits, v_t):
    m_old = m_ref[s]
    m_new = jnp.maximum(m_old, jnp.max(logits, axis=0, keepdims=True))
    p = jnp.exp2(logits - m_new)
    acc_ref[s] = jnp.exp2(m_old - m_new) * acc_ref[s] + _dot(v_t, p.astype(BF16))
    m_ref[s] = m_new


def _softmax_merge(m_ref, acc_ref, states):
    m = m_ref[states[0]]
    for s in states[1:]:
        m = jnp.maximum(m, m_ref[s])
    acc = None
    for s in states:
        w = jnp.exp2(m_ref[s] - m)
        acc = w * acc_ref[s] if acc is None else acc + w * acc_ref[s]
    return acc


def _hide(live):
    return jnp.where(live, 0.0, -jnp.inf)


def _grouped_loop(n, group, body, tail):
    shift = group.bit_length() - 1
    assert 1 << shift == group
    full = n >> shift
    lax.fori_loop(0, full, body, 0)
    first = full << shift
    count = group >> 1
    while count:
        pl.when((n & count) != 0)(functools.partial(tail, first, count))
        first = first + (n & count)
        count >>= 1


def _moba_kernel(q_ref, k_ref, v_ref, tbl_ref, o_ref, km_ref, bias_ref, sel_ref, m_ref, acc_ref, *, nblk):
    tk = MOBA_BLOCK
    tq = q_ref.shape[1]
    nsub = tq // tk
    cols = 2 * tq
    qi = pl.program_id(2)

    @pl.when(qi == 0)
    def _():
        km_ref[...] = jnp.zeros(km_ref.shape, F32)

        def mean_body(n, carry):
            kb = k_ref[pl.ds(pl.multiple_of(n * tk, tk), tk), :].astype(F32)
            km_ref[pl.ds(n, 1), :] = jnp.mean(kb, axis=0, keepdims=True)
            return carry

        lax.fori_loop(0, nblk, mean_body, 0)
        j = lax.broadcasted_iota(jnp.int32, (tk, tq), 0)
        i = lax.broadcasted_iota(jnp.int32, (tk, tq), 1)
        for e in range(-1, nsub):
            bias_ref[e + 1] = _bias_tile_t(tbl_ref, 2, i - tk * e - j)

    qs = _stack_heads_t(q_ref[...])
    sub_s = lax.broadcasted_iota(jnp.int32, (LANES, cols), 1) & (tq - 1)
    past = lax.broadcasted_iota(jnp.int32, (LANES, cols), 0) < nsub * qi + (sub_s >> MOBA_SHIFT)
    score = jnp.where(past, _dot(km_ref[...].astype(BF16), qs), NEG)
    sel_ref[...] = jnp.where(past & (_top_n_mask_t(score, MOBA_TOPK) > 0.5), 0.0, -jnp.inf)
    _softmax_init(m_ref, acc_ref, MOBA_FAR_STATES)
    ones = jnp.ones((ONES_ROWS, tk), BF16)

    def tiles(*specs, fresh=False):
        units = []
        for state, n, fn in specs:
            start = pl.multiple_of(n * tk, tk)
            units += [(state, start, fn, slice(c, c + COL_BLOCK)) for c in range(0, cols, COL_BLOCK)]

        def score(unit):
            _, start, fn, cs = unit
            return fn(_dot(k_ref[pl.ds(start, tk), :], qs[:, cs]), cs)

        def fold(unit, logits):
            state, start, _, cs = unit
            head = cs.start // tq
            v_t = jnp.concatenate([v_ref[head * HEAD_DIM:(head + 1) * HEAD_DIM, pl.ds(start, tk)], ones], axis=0)
            (_softmax_set if fresh else _softmax_update)(m_ref, acc_ref, (state, slice(None), cs), logits, v_t)

        pending = []
        for unit in units:
            pending.append((unit, score(unit)))
            if len(pending) > SCORE_LOOKAHEAD:
                fold(*pending.pop(0))
        for item in pending:
            fold(*item)

    def hidden(n, cs, live=None):
        row = sel_ref[pl.ds(n, 1), cs]
        if live is not None:
            row = row + _hide(live)
        return jnp.broadcast_to(row, (tk, COL_BLOCK))

    n_far = jnp.maximum(nsub * qi - 1, 0)

    group = MOBA_FAR_STATES

    def far_tiles(first, count):
        tiles(*[(u, first + u, lambda z, cs, n=first + u: z + hidden(n, cs)) for u in range(count)])

    def far_body(i, carry):
        far_tiles(group * i, group)
        return carry

    _grouped_loop(n_far, group, far_body, far_tiles)
    prev = jnp.maximum(nsub * qi - 1, 0)
    j_i = lax.broadcasted_iota(jnp.int32, (tk, cols), 0)
    t_i = lax.broadcasted_iota(jnp.int32, (tk, cols), 1) & (tq - 1)

    def own_tile(e):
        causal_own = ((t_i >> MOBA_SHIFT) == e) & (j_i <= t_i - tk * e)
        return (MOBA_FAR_STATES + 1 + e, nsub * qi + e,
                lambda z, cs: z + bias_ref[e + 1, :, cs] + jnp.where(causal_own[:, cs], 0.0, hidden(nsub * qi + e, cs)))

    tiles((MOBA_FAR_STATES, prev, lambda z, cs: z + bias_ref[0, :, cs] + hidden(prev, cs, live=qi >= 1)),
          *[own_tile(e) for e in range(nsub)], fresh=True)
    acc = _softmax_merge(m_ref, acc_ref, list(range(MOBA_FAR_STATES + 1 + nsub)))
    out = acc[:HEAD_DIM] * (1.0 / jnp.maximum(acc[HEAD_DIM:HEAD_DIM + 1], TINY))
    o_ref[...] = jnp.concatenate([out[:, :tq], out[:, tq:]], axis=0).T.astype(o_ref.dtype)


def _moba_attention(kmat, feat, tbl, b, s, tq):
    t = b * s
    tk = MOBA_BLOCK
    nq = s // tq
    cols = 2 * tq
    n_states = MOBA_FAR_STATES + 1 + tq // tk
    return pl.pallas_call(
        functools.partial(_moba_kernel, nblk=s // tk),
        grid=(b, MOBA_HEADS // 2, nq),
        in_specs=[
            pl.BlockSpec((LANES, tq), lambda bi, p, i: (F_MB_Q + p, bi * nq + i)),
            pl.BlockSpec((s, LANES), lambda bi, p, i: (bi, K_MB + p)),
            pl.BlockSpec((LANES, s), lambda bi, p, i: (F_MB_V + p, bi)),
            pl.BlockSpec((None, SUBLANES, LANES), lambda bi, p, i: (p, 0, 0)),
        ],
        out_specs=pl.BlockSpec((tq, LANES), lambda bi, p, i: (bi * nq + i, p)),
        out_shape=jax.ShapeDtypeStruct((t, MOBA_HEADS * HEAD_DIM), BF16),
        scratch_shapes=[
            pltpu.VMEM((LANES, LANES), F32),
            pltpu.VMEM((tq // tk + 1, tk, cols), F32),
            pltpu.VMEM((LANES, cols), F32),
            pltpu.VMEM((n_states, 1, cols), F32),
            pltpu.VMEM((n_states, HEAD_DIM + ONES_ROWS, cols), F32),
        ],
        compiler_params=pltpu.CompilerParams(
            dimension_semantics=("arbitrary", "arbitrary", "arbitrary"), vmem_limit_bytes=VMEM_LIMIT),
        name="moba",
    )(feat, kmat, feat, tbl)


def _compress_kernel(rk_ref, rv_ref, wk_ref, wv_ref, pos_ref, w2_ref, kn_ref, kc_ref, vc_ref):
    n = rk_ref.shape[0]
    hid = CMP_HIDDEN
    lo = lax.broadcasted_iota(jnp.int32, (n, LANES), 1) < HEAD_DIM
    for idx, (r_ref, w_ref) in enumerate(((rk_ref, wk_ref), (rv_ref, wv_ref))):
        r = r_ref[...]
        first = _dot((r + pos_ref[2 * idx:2 * idx + 1, :]).astype(BF16), w_ref[:, :2 * hid])
        second = _dot((r + pos_ref[2 * idx + 1:2 * idx + 2, :]).astype(BF16), w_ref[:, 2 * hid:])
        h = jax.nn.gelu(first + pltpu.roll(second, shift=n - 1, axis=0))
        for g in range(NSA_KV_GROUPS):
            y = _dot(h[:, g * hid:(g + 1) * hid].astype(BF16), w2_ref[idx])
            if idx == 0:
                kc_ref[g] = _rms(y, kn_ref[...]).astype(kc_ref.dtype)
            else:
                vc_ref[g] = jnp.where(lo, y, 1.0).T.astype(vc_ref.dtype)


def _compress(rk, rv, wk, wv, pos, w2, kn):
    b, n, width = rk.shape
    return pl.pallas_call(
        _compress_kernel,
        grid=(b,),
        in_specs=[
            pl.BlockSpec((None, n, width), lambda i: (i, 0, 0)),
            pl.BlockSpec((None, n, width), lambda i: (i, 0, 0)),
            pl.BlockSpec(wk.shape, lambda i: (0, 0)),
            pl.BlockSpec(wv.shape, lambda i: (0, 0)),
            pl.BlockSpec(pos.shape, lambda i: (0, 0)),
            pl.BlockSpec(w2.shape, lambda i: (0, 0, 0)),
            pl.BlockSpec(kn.shape, lambda i: (0, 0)),
        ],
        out_specs=[
            pl.BlockSpec((None, NSA_KV_GROUPS, n, LANES), lambda i: (i, 0, 0, 0)),
            pl.BlockSpec((None, NSA_KV_GROUPS, LANES, n), lambda i: (i, 0, 0, 0)),
        ],
        out_shape=[
            jax.ShapeDtypeStruct((b, NSA_KV_GROUPS, n, LANES), BF16),
            jax.ShapeDtypeStruct((b, NSA_KV_GROUPS, LANES, n), BF16),
        ],
        compiler_params=pltpu.CompilerParams(
            dimension_semantics=("parallel",), vmem_limit_bytes=VMEM_LIMIT),
        name="nsa_compress",
    )(rk, rv, wk, wv, pos, w2, kn)


def _nsa_kernel(q_ref, kc_ref, vc_ref, ks_ref, vs_ref, kw_ref, vw_ref, gate_ref, tbl_ref, o_ref,
                bdiag_ref, bnear_ref, ov_ref, sel_ref, m_ref, acc_ref, tot_ref, lc_ref, imp_ref):
    tq = q_ref.shape[1]
    nh = NSA_HPG
    cols = nh * tq
    ncmp = kc_ref.shape[0]
    qi = pl.program_id(2)
    q0 = qi * tq

    @pl.when(qi == 0)
    def _():
        j = lax.broadcasted_iota(jnp.int32, (tq, tq), 0)
        i = lax.broadcasted_iota(jnp.int32, (tq, tq), 1)
        bdiag_ref[...] = _bias_tile_t(tbl_ref, nh, i - j)
        bnear_ref[...] = _bias_tile_t(tbl_ref, nh, tq + i - j)
        ss = lax.broadcasted_iota(jnp.int32, (LANES, ncmp), 0) * SLC_BLOCK
        cs = lax.broadcasted_iota(jnp.int32, (LANES, ncmp), 1) * CMP_STRIDE
        ov_ref[...] = jnp.where((cs < ss + SLC_BLOCK) & (cs + CMP_BLOCK > ss), 1.0, 0.0).astype(BF16)

    qs = _stack_heads_t(q_ref[...])

    def gate(branch):
        return jnp.concatenate([gate_ref[branch * nh + h:branch * nh + h + 1, :] for h in range(nh)], axis=1)

    def result(states):
        acc = _softmax_merge(m_ref, acc_ref, states)
        return acc[:HEAD_DIM] * (1.0 / jnp.maximum(acc[HEAD_DIM:HEAD_DIM + 1], TINY))

    j_i = lax.broadcasted_iota(jnp.int32, (tq, cols), 0)
    i_i = lax.broadcasted_iota(jnp.int32, (tq, cols), 1) & (tq - 1)
    causal = j_i <= i_i
    per_tile = tq // SLC_BLOCK
    nwin = WINDOW // tq
    n_sel_states = NSA_SEL_STATES
    group = NSA_FAR_GROUP
    _softmax_init(m_ref, acc_ref, group)

    def tile_steps(*specs, fresh=False):
        units = []
        for state, k_ref, v_ref, kb, fn in specs:
            start = pl.multiple_of(kb * tq, tq)
            units += [(state, k_ref, v_ref, start, fn, slice(c, c + COL_BLOCK)) for c in range(0, cols, COL_BLOCK)]

        def score(unit):
            _, k_ref, _, start, fn, cs = unit
            return fn(_dot(k_ref[pl.ds(start, tq), :], qs[:, cs]), cs)

        def fold(unit, logits):
            state, _, v_ref, start, _, cs = unit
            (_softmax_set if fresh else _softmax_update)(
                m_ref, acc_ref, (state, slice(None), cs), logits, v_ref[:NSA_V_ROWS, pl.ds(start, tq)])

        pending = []
        for unit in units:
            pending.append((unit, score(unit)))
            if len(pending) > SCORE_LOOKAHEAD:
                fold(*pending.pop(0))
                yield
        for item in pending:
            fold(*item)
            yield

    def tiles(*specs, fresh=False):
        for _ in tile_steps(*specs, fresh=fresh):
            pass

    def window_tile(r):
        def fn(z, cs):
            if r == 1:
                z = z + bnear_ref[:, cs]
            if r == nwin:
                z = jnp.where((j_i > i_i)[:, cs], z, -jnp.inf)
            return z + _hide(qi >= r)
        return (n_sel_states + r, kw_ref, vw_ref, jnp.maximum(qi - r, 0), fn)

    window_steps = tile_steps(
        *[window_tile(r) for r in range(nwin, 0, -1)],
        (n_sel_states, kw_ref, vw_ref, qi, lambda z, cs: jnp.where(causal[:, cs], z + bdiag_ref[:, cs], -jnp.inf)),
        fresh=True)

    per_tile_c = tq // CMP_STRIDE
    below = _ceil_to(-(-(REL_MAX_DISTANCE + CMP_BLOCK) // CMP_STRIDE), SUBLANES)
    band = below + per_tile_c
    band_start = pl.multiple_of(jnp.maximum(qi * per_tile_c - below, 0), SUBLANES)
    n_b = band_start + lax.broadcasted_iota(jnp.int32, (band, tq), 0)
    dist_b = q0 + lax.broadcasted_iota(jnp.int32, (band, tq), 1) - (n_b * CMP_STRIDE + CMP_BLOCK - 1)
    band_bias = _bias_tile_t(tbl_ref, nh, dist_b)

    def compressed(rows):
        n_i = lax.broadcasted_iota(jnp.int32, (rows, cols), 0)
        t_i = lax.broadcasted_iota(jnp.int32, (rows, cols), 1) & (tq - 1)
        valid_c = q0 + t_i - (n_i * CMP_STRIDE + CMP_BLOCK - 1) >= 0
        lc_ref[:rows] = _dot(kc_ref[:rows, :], qs)
        lc_ref[pl.ds(band_start, band), :] += band_bias
        logit_c = jnp.where(valid_c, lc_ref[:rows], -jnp.inf)
        p_c = jnp.exp2(logit_c - jnp.maximum(jnp.max(logit_c, axis=0, keepdims=True), NEG))
        p_c = p_c / jnp.maximum(jnp.sum(p_c, axis=0, keepdims=True), TINY)
        tot_ref[...] = gate(0) * _dot(vc_ref[:HEAD_DIM, :rows], p_c.astype(BF16))
        p_sum = p_c[:, 0:tq]
        for h in range(1, nh):
            p_sum = p_sum + p_c[:, h * tq:(h + 1) * tq]
        imp_ref[...] = _dot_hilo_rhs(ov_ref[:, :rows], p_sum)

    step = min(CMP_ROW_STEP, ncmp)
    needed = (qi + 1) * per_tile_c
    for rows in range(step, ncmp + 1, step):
        pl.when((needed > rows - step) & (needed <= rows))(functools.partial(compressed, rows))
    imp = imp_ref[...]
    blk = lax.broadcasted_iota(jnp.int32, (LANES, tq), 0)
    cur = (q0 + lax.broadcasted_iota(jnp.int32, (LANES, tq), 1)) >> SLC_SHIFT
    forced = (blk == 0) | (blk == cur) | (blk == cur - 1)
    visible = blk <= cur
    score = jnp.where(visible & ~forced, imp, NEG)
    top = _top_n_mask_t(score, SLC_TOPN - N_FORCED, between=lambda: next(window_steps, None))
    for _ in window_steps:
        pass
    sel = jnp.where(visible & (forced | (top > 0.5)), 0.0, -jnp.inf)
    sel_ref[...] = jnp.concatenate([sel] * nh, axis=1)

    def hidden(kb, cs, live=None):
        rows = [sel_ref[pl.ds(kb * per_tile + r, 1), cs] for r in range(per_tile)]
        if live is not None:
            rows = [row + _hide(live) for row in rows]
        return jnp.concatenate([jnp.broadcast_to(row, (SLC_BLOCK, COL_BLOCK)) for row in rows], axis=0)

    n_far = jnp.maximum(qi - 1, 0)

    def far_tiles(first, count):
        tiles(*[(u, ks_ref, vs_ref, first + u, lambda z, cs, kb=first + u: z + hidden(kb, cs)) for u in range(count)])

    def sel_body(i, carry):
        far_tiles(group * i, group)
        return carry

    _grouped_loop(n_far, group, sel_body, far_tiles)
    near = jnp.maximum(qi - 1, 0)
    tiles((group, ks_ref, vs_ref, near, lambda z, cs: z + bnear_ref[:, cs] + hidden(near, cs, live=qi >= 1)),
          (group + 1, ks_ref, vs_ref, qi,
           lambda z, cs: jnp.where(causal[:, cs], z + bdiag_ref[:, cs] + hidden(qi, cs), -jnp.inf)),
          fresh=True)
    tot = (tot_ref[...] + gate(1) * result(list(range(n_sel_states)))
           + gate(2) * result(list(range(n_sel_states, n_sel_states + nwin + 1))))
    pairs = [jnp.concatenate([tot[:, (2 * c) * tq:(2 * c + 1) * tq], tot[:, (2 * c + 1) * tq:(2 * c + 2) * tq]], axis=0).T
             for c in range(nh // 2)]
    o_ref[...] = jnp.concatenate(pairs, axis=1).astype(o_ref.dtype)


def _nsa_attention(kmat, feat, kc, vc, gates_t, tbl, b, s, tq):
    t = b * s
    nq = s // tq
    ncmp = kc.shape[2]
    assert ncmp % min(CMP_ROW_STEP, ncmp) == 0
    cols = NSA_HPG * tq
    n_states = NSA_SEL_STATES + WINDOW // tq + 1
    k_spec = lambda blk: pl.BlockSpec((s, LANES), lambda bi, g, i: (bi, blk + g))
    v_spec = lambda blk: pl.BlockSpec((LANES, s), lambda bi, g, i: (blk + g, bi))
    return pl.pallas_call(
        _nsa_kernel,
        grid=(b, NSA_KV_GROUPS, nq),
        in_specs=[
            pl.BlockSpec((2 * LANES, tq), lambda bi, g, i: (F_NS_Q // 2 + g, bi * nq + i)),
            pl.BlockSpec((None, None, ncmp, LANES), lambda bi, g, i: (bi, g, 0, 0)),
            pl.BlockSpec((None, None, LANES, ncmp), lambda bi, g, i: (bi, g, 0, 0)),
            k_spec(K_NS_SLC), v_spec(F_NS_SLC_V), k_spec(K_NS_WIN), v_spec(F_NS_WIN_V),
            pl.BlockSpec((LANES, tq), lambda bi, g, i: (g, bi * nq + i)),
            pl.BlockSpec((None, SUBLANES, LANES), lambda bi, g, i: (g, 0, 0)),
        ],
        out_specs=pl.BlockSpec((tq, 2 * LANES), lambda bi, g, i: (bi * nq + i, g)),
        out_shape=jax.ShapeDtypeStruct((t, NSA_HEADS * HEAD_DIM), BF16),
        scratch_shapes=[
            pltpu.VMEM((tq, cols), F32),
            pltpu.VMEM((tq, cols), F32),
            pltpu.VMEM((LANES, ncmp), BF16),
            pltpu.VMEM((LANES, cols), F32),
            pltpu.VMEM((n_states, 1, cols), F32),
            pltpu.VMEM((n_states, NSA_V_ROWS, cols), F32),
            pltpu.VMEM((HEAD_DIM, cols), F32),
            pltpu.VMEM((ncmp, cols), F32),
            pltpu.VMEM((LANES, tq), F32),
        ],
        compiler_params=pltpu.CompilerParams(
            dimension_semantics=("arbitrary", "arbitrary", "arbitrary"), vmem_limit_bytes=VMEM_LIMIT),
        name="nsa",
    )(feat, kc, vc, kmat, feat, kmat, feat, gates_t, tbl)


def _merge_kernel(x_ref, g_ref, oa_ref, ob_ref, oc_ref, wg_ref, wb_ref, wo_ref, o_ref):
    x = x_ref[...]
    d = x.shape[1]
    h = _rms(x, g_ref[...]).astype(BF16)
    mix = None
    row = 0
    for br, src in enumerate((oa_ref, ob_ref, oc_ref)):
        width = src.shape[1]
        gate = jax.nn.sigmoid(_dot(h, wg_ref[:, br * d:(br + 1) * d]))
        term = gate * _dot(src[...], wb_ref[row:row + width, :])
        mix = term if mix is None else mix + term
        row += width
    o_ref[...] = x + _dot(mix.astype(BF16), wo_ref[...])


def _merge(x2, g, oa, ob, oc, wg, wb, wo, tm):
    t, d = x2.shape
    row = lambda w: pl.BlockSpec((tm, w), lambda i: (i, 0))
    full = lambda a: pl.BlockSpec(a.shape, lambda i: (0, 0))
    return pl.pallas_call(
        _merge_kernel,
        grid=(t // tm,),
        in_specs=[row(d), full(g), row(oa.shape[1]), row(ob.shape[1]), row(oc.shape[1]),
                  full(wg), full(wb), full(wo)],
        out_specs=row(d),
        out_shape=jax.ShapeDtypeStruct((t, d), F32),
        compiler_params=pltpu.CompilerParams(
            dimension_semantics=("parallel",), vmem_limit_bytes=VMEM_LIMIT),
        name="merge",
    )(x2, g, oa, ob, oc, wg, wb, wo)


def _ffn_kernel(x_ref, g_ref, wgu_ref, wd_ref, o_ref, *, n_chunks):
    x = x_ref[...]
    d_ff = wd_ref.shape[0]
    ch = d_ff // n_chunks
    h = _rms(x, g_ref[...]).astype(BF16)
    out = x
    for c in range(n_chunks):
        gate = _dot(h, wgu_ref[:, c * ch:(c + 1) * ch])
        up = _dot(h, wgu_ref[:, d_ff + c * ch:d_ff + (c + 1) * ch])
        out = out + _dot((jax.nn.silu(gate) * up).astype(BF16), wd_ref[c * ch:(c + 1) * ch, :])
    o_ref[...] = out


def _ffn(x2, g, wgu, wd, tm):
    t, d = x2.shape
    d_ff = wd.shape[0]
    n_chunks = 2 if d_ff % (2 * LANES) == 0 else 1
    return pl.pallas_call(
        functools.partial(_ffn_kernel, n_chunks=n_chunks),
        grid=(t // tm,),
        in_specs=[
            pl.BlockSpec((tm, d), lambda i: (i, 0)),
            pl.BlockSpec((1, d), lambda i: (0, 0)),
            pl.BlockSpec(wgu.shape, lambda i: (0, 0)),
            pl.BlockSpec(wd.shape, lambda i: (0, 0)),
        ],
        out_specs=pl.BlockSpec((tm, d), lambda i: (i, 0)),
        out_shape=jax.ShapeDtypeStruct((t, d), F32),
        compiler_params=pltpu.CompilerParams(
            dimension_semantics=("parallel",), vmem_limit_bytes=VMEM_LIMIT),
        name="swiglu",
    )(x2, g, wgu, wd)


def _dup_groups(w):
    d = w.shape[0]
    w = w.reshape(d, NSA_KV_GROUPS, 1, HEAD_DIM)
    return jnp.broadcast_to(w, (d, NSA_KV_GROUPS, 2, HEAD_DIM)).reshape(d, NSA_KV_GROUPS * PAIR)


def _half_groups(w):
    d = w.shape[0]
    w = w.reshape(d, NSA_KV_GROUPS, HEAD_DIM)
    return jnp.pad(w, ((0, 0), (0, 0), (0, HEAD_DIM))).reshape(d, NSA_KV_GROUPS * PAIR)


def _pair_gain(g, scale=1.0):
    return jnp.concatenate([g, g]) * scale


def _layer_params(w_in, moba_q_norm, moba_k_norm, nsa_q_norm, nsa_k_norm, nsa_cmp_pos, nsa_cmp_w1, nsa_cmp_w2):
    d = w_in.shape[0]
    sbw, mbw, nsw, kvw = SB_HEADS * HEAD_DIM, MOBA_HEADS * HEAD_DIM, NSA_HEADS * HEAD_DIM, NSA_KV_GROUPS * HEAD_DIM
    sb_q, sb_k, sb_v = (w_in[:, i * sbw:(i + 1) * sbw] for i in range(3))
    o = 3 * sbw
    mb_q, mb_k, mb_v = (w_in[:, o + i * mbw:o + (i + 1) * mbw] for i in range(3))
    o += 3 * mbw
    ns_q = w_in[:, o:o + nsw]
    o += nsw
    kc_w, vc_w, ks_w, vs_w, kw_w, vw_w = (w_in[:, o + i * kvw:o + (i + 1) * kvw] for i in range(6))
    o += 6 * kvw
    gate_w = w_in[:, o:o + N_BRANCHES * NSA_HEADS].reshape(d, N_BRANCHES, NSA_KV_GROUPS, NSA_HPG)
    o += N_BRANCHES * NSA_HEADS
    wg = w_in[:, o:].astype(BF16)
    w = jnp.concatenate([sb_k, mb_k, _dup_groups(ks_w), _dup_groups(kw_w),
                         sb_q, mb_q, ns_q, sb_v, mb_v, _half_groups(vs_w), _half_groups(vw_w)], axis=1).astype(BF16)
    gate_cols = []
    for g in range(NSA_KV_GROUPS):
        cols = gate_w[:, :, g, :].reshape(d, N_BRANCHES * NSA_HPG)
        gate_cols.append(jnp.pad(cols, ((0, 0), (0, LANES - N_BRANCHES * NSA_HPG))))
    wa = jnp.concatenate([kc_w, vc_w] + gate_cols, axis=1).astype(BF16)
    scale = HEAD_DIM ** -0.5
    ones = jnp.ones((LANES,), F32)
    gains = [ones] * 2 + [_pair_gain(moba_k_norm)] * 2 + [_pair_gain(nsa_k_norm[1])] * 2 + [_pair_gain(nsa_k_norm[2])] * 2
    gains += [ones * (scale * LOG2E)] * 2 + [_pair_gain(moba_q_norm, scale * LOG2E)] * 2
    gains += [_pair_gain(nsa_q_norm, scale * LOG2E)] * 4 + [ones] * 8
    gain = jnp.concatenate(gains).reshape(1, (N_K_BLOCKS + N_F_BLOCKS) * LANES)

    half = CMP_BLOCK // 2
    hid = CMP_HIDDEN

    def spread(w1):
        w1 = w1.reshape(2, half, HEAD_DIM, hid)
        cols = []
        for part in range(2):
            for g in range(NSA_KV_GROUPS):
                z = jnp.zeros((half, NSA_KV_GROUPS, HEAD_DIM, hid), F32).at[:, g].set(w1[part])
                cols.append(z.reshape(half * kvw, hid))
        return jnp.concatenate(cols, axis=1).astype(BF16)

    def spread_pos(p):
        p = p.reshape(2, half, 1, HEAD_DIM)
        return jnp.broadcast_to(p, (2, half, NSA_KV_GROUPS, HEAD_DIM)).reshape(2, half * kvw)

    pos = jnp.concatenate([spread_pos(nsa_cmp_pos[0]), spread_pos(nsa_cmp_pos[1])], axis=0)
    w2 = jnp.stack([jnp.concatenate([nsa_cmp_w2[0], nsa_cmp_w2[0]], axis=-1),
                    jnp.pad(nsa_cmp_w2[1], ((0, 0), (0, HEAD_DIM)))]).astype(BF16)
    kn = _pair_gain(nsa_k_norm[0]).reshape(1, LANES)
    return dict(w=w, wa=wa, gain=gain, wg=wg, wk=spread(nsa_cmp_w1[0]), wv=spread(nsa_cmp_w1[1]),
                pos=pos, w2=w2, kn=kn)


def _bias_tables(rel_bias):
    tbl = jnp.pad(rel_bias.T, ((0, 0), (0, LANES - N_BUCKETS)))

    def rows(t, n):
        t = t.reshape(n, -1, LANES)
        return jnp.pad(t, ((0, 0), (0, SUBLANES - t.shape[1]), (0, 0)))

    return rows(tbl[:MOBA_HEADS], MOBA_HEADS // 2), rows(tbl[MOBA_HEADS:], NSA_KV_GROUPS)


def kernel(x, rel_bias, attn_norm, w_in, moba_q_norm, moba_k_norm, nsa_q_norm, nsa_k_norm, nsa_cmp_pos,
           nsa_cmp_w1, nsa_cmp_w2, w_branch, w_out, ffn_norm, w_gate_up, w_down):
    b, s, d = x.shape
    t = b * s
    tm = min(512, t)
    tq_moba = 512
    tq_nsa = 256
    tq_sb, tk_sb = 512, 256
    assert s % tq_moba == 0 and tq_moba % MOBA_BLOCK == 0 and s % tq_sb == 0 and WINDOW % tq_nsa == 0
    assert SLC_TOPN <= s // SLC_BLOCK <= LANES
    tbl_moba, tbl_nsa = _bias_tables(rel_bias)
    x2 = x.reshape(t, d)
    for layer in range(w_in.shape[0]):
        p = _layer_params(w_in[layer], moba_q_norm[layer], moba_k_norm[layer], nsa_q_norm[layer],
                          nsa_k_norm[layer], nsa_cmp_pos[layer], nsa_cmp_w1[layer], nsa_cmp_w2[layer])
        g_attn = attn_norm[layer].reshape(1, d)
        kmat, feat, kc_in, vc_in, gates_t = _inproj(x2, g_attn, p["w"], p["wa"], p["gain"], tm)
        o_a = _sb_attention(kmat, feat, b, s, tq_sb, tk_sb)
        o_b = _moba_attention(kmat, feat, tbl_moba, b, s, tq_moba)
        chunk = CMP_STRIDE * LANES
        kc, vc = _compress(kc_in.reshape(b, s // CMP_STRIDE, chunk), vc_in.reshape(b, s // CMP_STRIDE, chunk),
                           p["wk"], p["wv"], p["pos"], p["w2"], p["kn"])
        o_c = _nsa_attention(kmat, feat, kc, vc, gates_t, tbl_nsa, b, s, tq_nsa)
        x2 = _merge(x2, g_attn, o_a, o_b, o_c, p["wg"], w_branch[layer].astype(BF16),
                    w_out[layer].astype(BF16), tm)
        x2 = _ffn(x2, ffn_norm[layer].reshape(1, d), w_gate_up[layer].astype(BF16),
                  w_down[layer].astype(BF16), tm)
    return x2.reshape(b, s, d)
```
